```python
import math
import jax, jax.numpy as jnp
from jax import lax
import numpy as np

D_MODEL = 1024
BATCH = 16
SEQ = 2048
DEPTH = 1

CHUNK = 64
D_MIX = D_MODEL
A_HEADS = 8
A_HEAD_DIM = 64
A_WIDTH = A_HEADS * A_HEAD_DIM
IDX_HEADS = 8
IDX_DIM = 32
TOPK_MAX = 256
B_HEADS = 4
B_KEY_DIM = 64
B_VAL_DIM = 128
B_WIDTH = B_HEADS * B_VAL_DIM
B_KEY_WIDTH = B_HEADS * B_KEY_DIM
GATE_RANK = 16
GATE_NORMALIZER = 16.0
N_BUCKETS = 32
MAX_DISTANCE = 128
D_FF = ((8 * D_MODEL + 3 * 256 - 1) // (3 * 256)) * 256
EPS = 1e-6
NEG = -1e30

SPLIT_SIZES = (
    A_WIDTH, A_WIDTH, A_WIDTH,
    IDX_HEADS * IDX_DIM, IDX_DIM, IDX_HEADS,
    B_KEY_WIDTH, B_KEY_WIDTH, B_WIDTH,
    GATE_RANK, B_WIDTH,
)
D_IN = sum(SPLIT_SIZES)

kernel_name = "hymba_dsa_gla_streaming_block"


def rmsnorm(x, g):
    xf = x.astype(jnp.float32)
    y = xf * lax.rsqrt(jnp.mean(xf * xf, axis=-1, keepdims=True) + EPS)
    return (y * g.astype(jnp.float32)).astype(x.dtype)


def t5_bucket(rel):
    half = N_BUCKETS // 2
    max_exact = half // 2
    ret = jnp.where(rel > 0, half, 0)
    n = jnp.abs(rel)
    nf = jnp.maximum(n, 1).astype(jnp.float32)
    large = max_exact + (jnp.log(nf / max_exact) / math.log(MAX_DISTANCE / max_exact)
                         * (half - max_exact)).astype(jnp.int32)
    large = jnp.minimum(large, half - 1)
    return ret + jnp.where(n < max_exact, n, large)


def dsa_mixer(q, k, v, q_idx, k_idx, w_idx, rel_bias):
    bsz, seq = q.shape[0], q.shape[1]
    topk = min(TOPK_MAX, seq // 4)
    n_blk = seq // CHUNK
    key_pos = jnp.arange(seq)
    k_idx_f = k_idx.astype(jnp.float32)

    def block(args):
        n, qb, qib, wb = args
        limit = (n + 1) * CHUNK
        qpos = n * CHUNK + jnp.arange(CHUNK)
        dots = jax.nn.relu(jnp.einsum('bqhd,bsd->bqhs', qib.astype(jnp.float32), k_idx_f)
                           * IDX_DIM ** -0.5)
        score = jnp.einsum('bqhs,bqh->bqs', dots, wb.astype(jnp.float32) * IDX_HEADS ** -0.5)
        score = jnp.where(key_pos[None, None, :] < limit, score, NEG)
        _, sel = lax.top_k(score, topk)
        valid = sel < limit
        kg = jax.vmap(lambda kb, ib: kb[ib])(k, sel)
        vg = jax.vmap(lambda vb, ib: vb[ib])(v, sel)
        bias = rel_bias[t5_bucket(sel - qpos[None, :, None])]
        logits = (jnp.einsum('bqhd,bqkhd->bqhk', qb, kg).astype(jnp.float32) * A_HEAD_DIM ** -0.5
                  + jnp.swapaxes(bias, 2, 3).astype(jnp.float32))
        logits = jnp.where(valid[:, :, None, :], logits, NEG)
        p = jax.nn.softmax(logits, axis=-1).astype(v.dtype)
        return jnp.einsum('bqhk,bqkhd->bqhd', p, vg)

    def to_blocks(t):
        return jnp.swapaxes(t.reshape(bsz, n_blk, CHUNK, *t.shape[2:]), 0, 1)

    out = lax.map(block, (jnp.arange(n_blk), to_blocks(q), to_blocks(q_idx), to_blocks(w_idx)))
    return jnp.swapaxes(out, 0, 1).reshape(bsz, seq, A_WIDTH)


def gla_mixer(q, k, v, log_a):
    bsz, seq, nh, dk = q.shape
    nc = seq // CHUNK

    def chunks(t):
        return t.astype(jnp.float32).reshape(bsz, nc, CHUNK, nh, -1).transpose(1, 0, 3, 2, 4)

    qc = chunks(q) * dk ** -0.5
    kc, vc, gc = chunks(k), chunks(v), chunks(log_a)
    bcum = jnp.cumsum(gc, axis=3)
    blast = bcum[..., -1:, :]
    q_in = qc * jnp.exp(bcum)
    k_in = kc * jnp.exp(-bcum)
    k_dec = kc * jnp.exp(blast - bcum)
    causal = jnp.tril(jnp.ones((CHUNK, CHUNK), dtype=bool))
    att = jnp.where(causal, jnp.einsum('nbhtd,nbhsd->nbhts', q_in, k_in), 0.0)
    o_intra = jnp.einsum('nbhts,nbhsv->nbhtv', att, vc)

    def step(state, inp):
        qi, kd, vv, bl = inp
        o = jnp.einsum('bhtd,bhdv->bhtv', qi, state)
        state = state * jnp.exp(bl)[:, :, 0, :, None] + jnp.einsum('bhsd,bhsv->bhdv', kd, vv)
        return state, o

    s0 = jnp.zeros((bsz, nh, dk, vc.shape[-1]), jnp.float32)
    _, o_inter = lax.scan(step, s0, (q_in, k_dec, vc, blast))
    o = o_intra + o_inter
    return o.transpose(1, 0, 3, 2, 4).reshape(bsz, seq, nh, -1)


def setup_inputs(seed: int = 0) -> dict:
    key = jax.random.key(seed)
    ks = jax.random.split(key, 16)
    f32 = jnp.float32
    nrm = lambda k, shape, s: jax.random.normal(k, shape, f32) * s
    return {
        "x": nrm(ks[0], (BATCH, SEQ, D_MODEL), 1.0),
        "norm_mix": 1.0 + nrm(ks[1], (DEPTH, D_MODEL), 0.02),
        "w_in": nrm(ks[2], (DEPTH, D_MODEL, D_IN), D_MODEL ** -0.5),
        "w_gate_up": nrm(ks[3], (DEPTH, GATE_RANK, B_KEY_WIDTH), GATE_RANK ** -0.5),
        "b_gate": nrm(ks[4], (DEPTH, B_KEY_WIDTH), 0.01),
        "gla_norm": 1.0 + nrm(ks[5], (DEPTH, B_WIDTH), 0.02),
        "w_out": nrm(ks[6], (DEPTH, D_MIX, D_MODEL), D_MIX ** -0.5),
        "rel_bias": nrm(ks[7], (N_BUCKETS, A_HEADS), 0.2),
        "norm_ffn": 1.0 + nrm(ks[8], (DEPTH, D_MODEL), 0.02),
        "w_ffn_gate": nrm(ks[9], (DEPTH, D_MODEL, D_FF), D_MODEL ** -0.5),
        "w_ffn_up": nrm(ks[10], (DEPTH, D_MODEL, D_FF), D_MODEL ** -0.5),
        "w_ffn_down": nrm(ks[11], (DEPTH, D_FF, D_MODEL), D_FF ** -0.5),
        "norm_final": 1.0 + nrm(ks[12], (D_MODEL,), 0.02),
    }


def reference(x, norm_mix, w_in, w_gate_up, b_gate, gla_norm, w_out, rel_bias,
              norm_ffn, w_ffn_gate, w_ffn_up, w_ffn_down, norm_final):
    bsz, seq, _ = x.shape
    offsets = [int(o) for o in np.cumsum(SPLIT_SIZES)[:-1]]
    for l in range(DEPTH):
        h = rmsnorm(x, norm_mix[l])
        proj = h @ w_in[l]
        (qa, ka, va, qi, ki, wi, qb, kb, vb, gdown, ogate) = jnp.split(proj, offsets, axis=-1)
        o_a = dsa_mixer(qa.reshape(bsz, seq, A_HEADS, A_HEAD_DIM),
                        ka.reshape(bsz, seq, A_HEADS, A_HEAD_DIM),
                        va.reshape(bsz, seq, A_HEADS, A_HEAD_DIM),
                        qi.reshape(bsz, seq, IDX_HEADS, IDX_DIM), ki, wi, rel_bias)
        log_a = jax.nn.log_sigmoid((gdown @ w_gate_up[l] + b_gate[l]).astype(jnp.float32)) / GATE_NORMALIZER
        o_b = gla_mixer(qb.reshape(bsz, seq, B_HEADS, B_KEY_DIM),
                        kb.reshape(bsz, seq, B_HEADS, B_KEY_DIM),
                        vb.reshape(bsz, seq, B_HEADS, B_VAL_DIM),
                        log_a.reshape(bsz, seq, B_HEADS, B_KEY_DIM))
        o_b = rmsnorm(o_b, gla_norm[l].reshape(B_HEADS, B_VAL_DIM)).reshape(bsz, seq, B_WIDTH)
        o_b = (o_b * jax.nn.silu(ogate.astype(jnp.float32))).astype(x.dtype)
        mix = jnp.concatenate([o_a.astype(x.dtype), o_b], axis=-1) @ w_out[l]
        x = x + mix
        h = rmsnorm(x, norm_ffn[l])
        x = x + (jax.nn.silu(h @ w_ffn_gate[l]) * (h @ w_ffn_up[l])) @ w_ffn_down[l]
    return rmsnorm(x, norm_final)
```

```python
import functools
import math

import numpy as np
import jax
import jax.numpy as jnp
from jax import lax
from jax.experimental import pallas as pl
from jax.experimental.pallas import tpu as pltpu

F32 = jnp.float32
BF16 = jnp.bfloat16

D_MODEL = 1024
CHUNK = 64
A_HEADS = 8
A_HEAD_DIM = 64
A_WIDTH = A_HEADS * A_HEAD_DIM
IDX_HEADS = 8
IDX_DIM = 32
IDX_WIDTH = IDX_HEADS * IDX_DIM
TOPK_MAX = 256
B_HEADS = 4
B_KEY_DIM = 64
B_VAL_DIM = 128
B_WIDTH = B_HEADS * B_VAL_DIM
B_KEY_WIDTH = B_HEADS * B_KEY_DIM
GATE_RANK = 16
GATE_NORMALIZER = 16.0
N_BUCKETS = 32
MAX_DISTANCE = 128
D_FF = 2816
EPS = 1e-6
NEG = -1e30

LANES = 128
VMEM_LIMIT = 56 * 1024 * 1024

TM_PROJ = 512
TQ = 256
TS_GLA = 512
TM_FFN = 512

A_COLS = 2 * A_WIDTH + A_HEADS * LANES
IDX_COLS = IDX_WIDTH + IDX_WIDTH + LANES
B_COLS = 2 * B_KEY_WIDTH + B_WIDTH + B_WIDTH + LANES
W_COLS = A_COLS + IDX_COLS + B_COLS

NT_DIMS = (((1,), (1,)), ((), ()))
TN_DIMS = (((0,), (0,)), ((), ()))


def _dot(a, b):
    return jnp.dot(a, b, preferred_element_type=F32)


def _dot_nt(a, b):
    return lax.dot_general(a, b, NT_DIMS, preferred_element_type=F32)


def _dot_tn(a, b):
    return lax.dot_general(a, b, TN_DIMS, preferred_element_type=F32)


def _rms(x, g):
    return x * lax.rsqrt(jnp.mean(x * x, axis=-1, keepdims=True) + EPS) * g


def _resident(shape):
    return pl.BlockSpec(shape, lambda *_: (0,) * len(shape), pipeline_mode=pl.Buffered(1))


def _t5_bucket_np(rel):
    half = N_BUCKETS // 2
    max_exact = half // 2
    ret = np.where(rel > 0, half, 0)
    n = np.abs(rel)
    nf = np.maximum(n, 1).astype(np.float32)
    large = max_exact + (np.log(nf / np.float32(max_exact))
                         / np.float32(math.log(MAX_DISTANCE / max_exact))
                         * np.float32(half - max_exact)).astype(np.int32)
    large = np.minimum(large, half - 1)
    return (ret + np.where(n < max_exact, n, large)).astype(np.int32)


FAR_BUCKET = int(_t5_bucket_np(np.array([-2 * TQ]))[0])


def _bias_kernel(rb_ref, bkt_ref, o_ref):
    t = pl.program_id(0)
    h = pl.program_id(1)
    bk = bkt_ref[0]
    acc = jnp.zeros(bk.shape, F32)
    for b in range(N_BUCKETS):
        acc = jnp.where(bk == b, rb_ref[b, h], acc)
    acc = acc - rb_ref[FAR_BUCKET, h]
    o_ref[0, 0] = jnp.where(t == 0, 0.0, acc)


def _bias_tables(rel_bias):
    kl = np.arange(TQ)[:, None]
    ql = np.arange(TQ)[None, :]
    buckets = np.stack([_t5_bucket_np(kl - ql - TQ), _t5_bucket_np(kl - ql)])
    return pl.pallas_call(
        _bias_kernel,
        grid=(3, A_HEADS),
        in_specs=[pl.BlockSpec(memory_space=pltpu.SMEM),
                  pl.BlockSpec((1, TQ, TQ), lambda t, h: (jnp.maximum(t - 1, 0), 0, 0))],
        out_specs=pl.BlockSpec((1, 1, TQ, TQ), lambda t, h: (t, h, 0, 0)),
        out_shape=jax.ShapeDtypeStruct((3, A_HEADS, TQ, TQ), F32),
        name="bias_tables",
    )(rel_bias, jnp.asarray(buckets))


def _in_proj_kernel(x_ref, g_ref, w_ref, a_ref, i_ref, b_ref):
    h = _rms(x_ref[...], g_ref[...]).astype(BF16)
    c0 = 0
    a_ref[:, 0:A_WIDTH] = (_dot(h, w_ref[:, c0:c0 + A_WIDTH]) * A_HEAD_DIM ** -0.5).astype(BF16)
    c0 += A_WIDTH
    a_ref[:, A_WIDTH:2 * A_WIDTH] = _dot(h, w_ref[:, c0:c0 + A_WIDTH]).astype(BF16)
    c0 += A_WIDTH
    vw = A_HEADS * LANES
    lane = lax.broadcasted_iota(jnp.int32, (1, vw), 1)
    ones = jnp.where(lane % LANES >= A_HEAD_DIM, 1.0, 0.0)
    a_ref[:, 2 * A_WIDTH:] = (_dot(h, w_ref[:, c0:c0 + vw]) + ones).astype(BF16)
    c0 += vw
    i_ref[...] = _dot(h, w_ref[:, c0:c0 + IDX_COLS])
    c0 += IDX_COLS
    b_ref[...] = _dot(h, w_ref[:, c0:c0 + B_COLS])


def _pack_w_in(w_in):
    sizes = (A_WIDTH, A_WIDTH, A_WIDTH, IDX_WIDTH, IDX_DIM, IDX_HEADS,
             B_KEY_WIDTH, B_KEY_WIDTH, B_WIDTH, GATE_RANK, B_WIDTH)
    offs = np.cumsum((0,) + sizes)
    qa, ka, va, qi, ki, wi, qb, kb, vb, gd, og = [w_in[:, offs[j]:offs[j + 1]] for j in range(len(sizes))]
    d = w_in.shape[0]
    va_aug = jnp.pad(va.reshape(d, A_HEADS, A_HEAD_DIM), ((0, 0), (0, 0), (0, LANES - A_HEAD_DIM)))
    cols = [qa, ka, va_aug.reshape(d, A_HEADS * LANES),
            qi, jnp.tile(ki, (1, IDX_HEADS)), jnp.pad(wi, ((0, 0), (0, LANES - IDX_HEADS))),
            qb, kb, vb, og, jnp.pad(gd, ((0, 0), (0, LANES - GATE_RANK)))]
    return jnp.concatenate(cols, axis=1).astype(BF16)


def _in_proj(x2d, g, w_packed):
    t = x2d.shape[0]
    return pl.pallas_call(
        _in_proj_kernel,
        grid=(t // TM_PROJ,),
        in_specs=[pl.BlockSpec((TM_PROJ, D_MODEL), lambda r: (r, 0)),
                  _resident((1, D_MODEL)),
                  _resident((D_MODEL, W_COLS))],
        out_specs=[pl.BlockSpec((TM_PROJ, A_COLS), lambda r: (r, 0)),
                   pl.BlockSpec((TM_PROJ, IDX_COLS), lambda r: (r, 0)),
                   pl.BlockSpec((TM_PROJ, B_COLS), lambda r: (r, 0))],
        out_shape=[jax.ShapeDtypeStruct((t, A_COLS), BF16),
                   jax.ShapeDtypeStruct((t, IDX_COLS), F32),
                   jax.ShapeDtypeStruct((t, B_COLS), F32)],
        compiler_params=pltpu.CompilerParams(dimension_semantics=("parallel",),
                                             vmem_limit_bytes=VMEM_LIMIT),
        name="in_proj",
    )(x2d, g, w_packed)


def _colsum(x):
    r, c = x.shape
    return jnp.sum(jnp.sum(x.reshape(r // 8, 8, c), axis=0), axis=0, keepdims=True)


def _colmin(x):
    r, c = x.shape
    return jnp.min(jnp.min(x.reshape(r // 8, 8, c), axis=0), axis=0, keepdims=True)


def _colmax(x):
    r, c = x.shape
    return jnp.max(jnp.max(x.reshape(r // 8, 8, c), axis=0), axis=0, keepdims=True)


N_BISECT = 20


def _dsa_kernel(q_ref, k_ref, v_ref, iq_ref, iw_ref, ik_ref, bias_ref, o_ref,
                qm_ref, s_ref, mb_ref, lg_ref):
    i = pl.program_id(1)
    nk = i + 1
    big = jnp.float32(3e38)
    kk = jnp.float32(TOPK_MAX)

    def blk(kc):
        return pl.ds(pl.multiple_of(kc * TQ, TQ), TQ)

    row = lax.broadcasted_iota(jnp.int32, (TQ, TQ), 0)
    col = lax.broadcasted_iota(jnp.int32, (TQ, TQ), 1)
    adm_diag = row < (col // CHUNK + 1) * CHUNK

    lane_i = lax.broadcasted_iota(jnp.int32, (1, IDX_WIDTH), 1)
    iq = iq_ref[...]
    for h in range(IDX_HEADS):
        qm_ref[h] = jnp.where(lane_i // IDX_DIM == h, iq, 0.0).astype(BF16)
    w_t = iw_ref[...].T * (IDX_DIM ** -0.5 * IDX_HEADS ** -0.5)

    def score_blk(kc, carry):
        kic = ik_ref[blk(kc), :].astype(BF16)
        acc = jnp.zeros((TQ, TQ), F32)
        for h in range(IDX_HEADS):
            d = _dot_nt(kic, qm_ref[h])
            acc = acc + jnp.maximum(d, 0.0) * w_t[h:h + 1, :]
        s_ref[blk(kc), :] = jnp.where(jnp.logical_or(kc < i, adm_diag), acc, NEG)
        return carry

    lax.fori_loop(0, nk, score_blk, 0)

    @pl.when(i == 0)
    def _():
        mb_ref[0:TQ, :] = jnp.where(adm_diag, 0.0, NEG)

    @pl.when(i > 0)
    def _():
        def minmax(kc, c):
            s = s_ref[blk(kc), :]
            return jnp.minimum(c[0], _colmin(s)), jnp.maximum(c[1], _colmax(s))
        lo, hi = lax.fori_loop(0, i, minmax, (jnp.full((1, TQ), big), jnp.full((1, TQ), -big)))
        hi = jnp.maximum(hi, _colmax(s_ref[blk(i), :]))

        def count_ge(t):
            def body(kc, acc):
                return acc + _colsum(jnp.where(s_ref[blk(kc), :] >= t, 1.0, 0.0))
            return lax.fori_loop(0, nk, body, jnp.zeros((1, TQ), F32))

        def bisect(_, c):
            lo, hi = c
            mid = 0.5 * lo + 0.5 * hi
            ge = count_ge(mid) >= kk
            return jnp.where(ge, mid, lo), jnp.where(ge, hi, mid)
        lo, hi = lax.fori_loop(0, N_BISECT, bisect, (lo, hi))

        def min_ge(kc, acc):
            s = s_ref[blk(kc), :]
            return jnp.minimum(acc, _colmin(jnp.where(s >= lo, s, big)))
        v0 = lax.fori_loop(0, nk, min_ge, jnp.full((1, TQ), big))

        def above(v):
            def body(kc, c):
                s = s_ref[blk(kc), :]
                gt = s > v
                return (c[0] + _colsum(jnp.where(gt, 1.0, 0.0)),
                        jnp.minimum(c[1], _colmin(jnp.where(gt, s, big))))
            return lax.fori_loop(0, nk, body, (jnp.zeros((1, TQ), F32), jnp.full((1, TQ), big)))

        def peel_cond(c):
            return c[2] > 0.5

        def peel(c):
            v, _, _ = c
            c_gt, nxt = above(v)
            done = c_gt < kk
            v_new = jnp.where(done, v, nxt)
            pending = jnp.max(jnp.where(done, 0.0, 1.0))
            return v_new, c_gt, pending
        thr, c_gt, _ = lax.while_loop(peel_cond, peel, (v0, jnp.zeros((1, TQ), F32), jnp.float32(1.0)))
        need = kk - c_gt

        ltri = (row >= col).astype(BF16)

        def select(kc, carry):
            s = s_ref[blk(kc), :]
            tie = s == thr
            pref = _dot(ltri, jnp.where(tie, 1.0, 0.0).astype(BF16))
            sel = jnp.logical_or(s > thr, jnp.logical_and(tie, carry + pref <= need))
            mb_ref[blk(kc), :] = jnp.where(sel, 0.0, NEG)
            return carry + pref[TQ - 1:TQ, :]
        lax.fori_loop(0, nk, select, jnp.zeros((1, TQ), F32))

    lane_p = lax.broadcasted_iota(jnp.int32, (1, LANES), 1)
    first_half = lane_p < A_HEAD_DIM
    for pair in range(A_HEADS // 2):
        cs = slice(pair * LANES, (pair + 1) * LANES)
        q_pair = q_ref[:, cs]
        outs = []
        for sub in range(2):
            h = 2 * pair + sub
            qm = jnp.where(first_half == (sub == 0), q_pair, jnp.zeros_like(q_pair))

            def logits(kc, m, h=h, qm=qm, cs=cs):
                t = jnp.maximum(kc - i + 2, 0)
                l = _dot_nt(k_ref[blk(kc), cs], qm) + mb_ref[blk(kc), :] + bias_ref[t, h]
                lg_ref[blk(kc), :] = l
                return jnp.maximum(m, _colmax(l))
            m = lax.fori_loop(0, nk, logits, jnp.full((1, TQ), -big))

            def pv(kc, acc, h=h, m=m):
                p = jnp.exp(lg_ref[blk(kc), :] - m).astype(BF16)
                return acc + _dot_tn(p, v_ref[blk(kc), h * LANES:(h + 1) * LANES])
            acc = lax.fori_loop(0, nk, pv, jnp.zeros((TQ, LANES), F32))
            rolled = pltpu.roll(acc, A_HEAD_DIM, axis=1)
            outs.append(acc / rolled if sub == 0 else rolled / acc)
        o_ref[:, cs] = jnp.where(first_half, outs[0], outs[1]).astype(o_ref.dtype)


def _dsa(a_out, idx_out, bias_tab, bsz, seq):
    nq = seq // TQ
    kcol = A_WIDTH // A_WIDTH
    vcol = 2 * A_WIDTH // (A_HEADS * LANES)
    return pl.pallas_call(
        _dsa_kernel,
        grid=(bsz, nq),
        in_specs=[pl.BlockSpec((TQ, A_WIDTH), lambda b, i: (b * nq + i, 0)),
                  pl.BlockSpec((seq, A_WIDTH), lambda b, i: (b, kcol)),
                  pl.BlockSpec((seq, A_HEADS * LANES), lambda b, i: (b, vcol)),
                  pl.BlockSpec((TQ, IDX_WIDTH), lambda b, i: (b * nq + i, 0)),
                  pl.BlockSpec((TQ, LANES), lambda b, i: (b * nq + i, 2 * IDX_WIDTH // LANES)),
                  pl.BlockSpec((seq, IDX_WIDTH), lambda b, i: (b, 1)),
                  _resident((3, A_HEADS, TQ, TQ))],
        out_specs=pl.BlockSpec((TQ, A_WIDTH), lambda b, i: (b * nq + i, 0)),
        out_shape=jax.ShapeDtypeStruct((bsz * seq, A_WIDTH), BF16),
        scratch_shapes=[pltpu.VMEM((IDX_HEADS, TQ, IDX_WIDTH), BF16),
                        pltpu.VMEM((seq, TQ), F32),
                        pltpu.VMEM((seq, TQ), F32),
                        pltpu.VMEM((seq, TQ), F32)],
        compiler_params=pltpu.CompilerParams(dimension_semantics=("parallel", "arbitrary"),
                                             vmem_limit_bytes=VMEM_LIMIT),
        name="dsa",
    )(a_out, a_out, a_out, idx_out, idx_out, idx_out, bias_tab)


def _gla_kernel(qk_ref, v_ref, og_ref, gd_ref, wgu_ref, bg_ref, gn_ref, mcum_ref, mtot_ref,
                o_ref, st_ref, ob_ref):
    @pl.when(pl.program_id(1) == 0)
    def _():
        st_ref[...] = jnp.zeros_like(st_ref)

    z = _dot(gd_ref[...].astype(BF16), wgu_ref[...]) + bg_ref[...]
    g = -(jnp.maximum(-z, 0.0) + jnp.log1p(jnp.exp(-jnp.abs(z)))) * (1.0 / GATE_NORMALIZER)
    g_hi = g.astype(BF16)
    g_lo = (g - g_hi.astype(F32)).astype(BF16)
    mcum = mcum_ref[...]
    mtot = mtot_ref[...]
    bcum = _dot(mcum, g_hi) + _dot(mcum, g_lo)
    btot = _dot(mtot, g_hi) + _dot(mtot, g_lo)

    q = qk_ref[:, 0:B_KEY_WIDTH]
    k = qk_ref[:, B_KEY_WIDTH:2 * B_KEY_WIDTH]
    q_in = q * B_KEY_DIM ** -0.5 * jnp.exp(bcum)
    k_in = (k * jnp.exp(-bcum)).astype(BF16)
    k_dec = k * jnp.exp(btot - bcum)
    e_tot = jnp.exp(btot)

    lane = lax.broadcasted_iota(jnp.int32, (1, LANES), 1)
    halves = (lane < B_KEY_DIM, lane >= B_KEY_DIM)
    r_i = lax.broadcasted_iota(jnp.int32, (CHUNK, CHUNK), 0)
    c_i = lax.broadcasted_iota(jnp.int32, (CHUNK, CHUNK), 1)
    causal = r_i >= c_i

    for pair in range(B_HEADS // 2):
        cs = slice(pair * LANES, (pair + 1) * LANES)
        qp, kd, ki = q_in[:, cs], k_dec[:, cs], k_in[:, cs]
        qm = [jnp.where(hm, qp, 0.0).astype(BF16) for hm in halves]
        kdm = [jnp.where(hm, kd, 0.0).astype(BF16) for hm in halves]
        state = st_ref[pair]
        for c in range(TS_GLA // CHUNK):
            rs = slice(c * CHUNK, (c + 1) * CHUNK)
            sb = state.astype(BF16)
            upd = jnp.zeros_like(state)
            for sub in range(2):
                h = 2 * pair + sub
                vh = v_ref[rs, h * B_VAL_DIM:(h + 1) * B_VAL_DIM].astype(BF16)
                att = jnp.where(causal, _dot_nt(qm[sub][rs], ki[rs]), 0.0).astype(BF16)
                ob_ref[rs, h * B_VAL_DIM:(h + 1) * B_VAL_DIM] = _dot(att, vh) + _dot_nt(qm[sub][rs], sb)
                upd = upd + _dot_tn(vh, kdm[sub][rs])
            state = state * e_tot[c * CHUNK:c * CHUNK + 1, cs] + upd
        st_ref[pair] = state

    og = og_ref[...]
    gate = og * jax.nn.sigmoid(og)
    gn = gn_ref[...]
    for h in range(B_HEADS):
        cs = slice(h * B_VAL_DIM, (h + 1) * B_VAL_DIM)
        o_ref[:, cs] = (_rms(ob_ref[:, cs], gn[:, cs]) * gate[:, cs]).astype(o_ref.dtype)


def _gla(b_out, w_gate_up, b_gate, gla_norm, bsz, seq):
    ns = seq // TS_GLA
    idx = np.arange(TS_GLA)
    same = (idx[:, None] // CHUNK) == (idx[None, :] // CHUNK)
    mcum = jnp.asarray(same & (idx[None, :] <= idx[:, None]), BF16)
    mtot = jnp.asarray(same, BF16)
    wgu = jnp.pad(w_gate_up, ((0, LANES - GATE_RANK), (0, 0))).astype(BF16)
    return pl.pallas_call(
        _gla_kernel,
        grid=(bsz, ns),
        in_specs=[pl.BlockSpec((TS_GLA, 2 * B_KEY_WIDTH), lambda b, j: (b * ns + j, 0)),
                  pl.BlockSpec((TS_GLA, B_WIDTH), lambda b, j: (b * ns + j, 1)),
                  pl.BlockSpec((TS_GLA, B_WIDTH), lambda b, j: (b * ns + j, 2)),
                  pl.BlockSpec((TS_GLA, LANES), lambda b, j: (b * ns + j, (2 * B_KEY_WIDTH + 2 * B_WIDTH) // LANES)),
                  _resident((LANES, B_KEY_WIDTH)),
                  _resident((1, B_KEY_WIDTH)),
                  _resident((1, B_WIDTH)),
                  _resident((TS_GLA, TS_GLA)),
                  _resident((TS_GLA, TS_GLA))],
        out_specs=pl.BlockSpec((TS_GLA, B_WIDTH), lambda b, j: (b * ns + j, 0)),
        out_shape=jax.ShapeDtypeStruct((bsz * seq, B_WIDTH), BF16),
        scratch_shapes=[pltpu.VMEM((B_HEADS // 2, B_VAL_DIM, LANES), F32),
                        pltpu.VMEM((TS_GLA, B_WIDTH), F32)],
        compiler_params=pltpu.CompilerParams(dimension_semantics=("parallel", "arbitrary"),
                                             vmem_limit_bytes=VMEM_LIMIT),
        name="gla",
    )(b_out, b_out, b_out, b_out, wgu, b_gate.reshape(1, -1), gla_norm.reshape(1, -1), mcum, mtot)


def _out_ffn_kernel(x_ref, oa_ref, ob_ref, wo_ref, nf_ref, wg_ref, wu_ref, wd_ref, nl_ref, o_ref):
    mix = jnp.concatenate([oa_ref[...], ob_ref[...]], axis=1)
    x1 = x_ref[...] + _dot(mix, wo_ref[...])
    h = _rms(x1, nf_ref[...]).astype(BF16)
    g = _dot(h, wg_ref[...])
    u = _dot(h, wu_ref[...])
    a = (g * jax.nn.sigmoid(g) * u).astype(BF16)
    o_ref[...] = _rms(x1 + _dot(a, wd_ref[...]), nl_ref[...])


def _out_ffn(x2d, o_a, o_b, w_out, norm_ffn, w_gate, w_up, w_down, norm_final):
    t = x2d.shape[0]
    row = lambda r: (r, 0)
    return pl.pallas_call(
        _out_ffn_kernel,
        grid=(t // TM_FFN,),
        in_specs=[pl.BlockSpec((TM_FFN, D_MODEL), row),
                  pl.BlockSpec((TM_FFN, A_WIDTH), row),
                  pl.BlockSpec((TM_FFN, B_WIDTH), row),
                  _resident((A_WIDTH + B_WIDTH, D_MODEL)),
                  _resident((1, D_MODEL)),
                  _resident((D_MODEL, D_FF)),
                  _resident((D_MODEL, D_FF)),
                  _resident((D_FF, D_MODEL)),
                  _resident((1, D_MODEL))],
        out_specs=pl.BlockSpec((TM_FFN, D_MODEL), row),
        out_shape=jax.ShapeDtypeStruct((t, D_MODEL), F32),
        compiler_params=pltpu.CompilerParams(dimension_semantics=("parallel",),
                                             vmem_limit_bytes=VMEM_LIMIT),
        name="out_ffn",
    )(x2d, o_a, o_b, w_out, norm_ffn, w_gate, w_up, w_down, norm_final)


def kernel(x, norm_mix, w_in, w_gate_up, b_gate, gla_norm, w_out, rel_bias,
           norm_ffn, w_ffn_gate, w_ffn_up, w_ffn_down, norm_final):
    bsz, seq, d = x.shape
    depth = w_in.shape[0]
    assert d == D_MODEL and seq % TS_GLA == 0 and seq % TQ == 0 and seq // 4 >= TOPK_MAX
    bias_tab = _bias_tables(rel_bias)
    x2d = x.reshape(bsz * seq, d)
    for l in range(depth):
        a_out, idx_out, b_out = _in_proj(x2d, norm_mix[l].reshape(1, d), _pack_w_in(w_in[l]))
        o_a = _dsa(a_out, idx_out, bias_tab, bsz, seq)
        o_b = _gla(b_out, w_gate_up[l], b_gate[l], gla_norm[l], bsz, seq)
        last = l == depth - 1
        assert last, "fused final norm assumes a single layer"
        x2d = _out_ffn(x2d, o_a, o_b, w_out[l].astype(BF16), norm_ffn[l].reshape(1, d),
                       w_ffn_gate[l].astype(BF16), w_ffn_up[l].astype(BF16),
                       w_ffn_down[l].astype(BF16), norm_final.reshape(1, d))
    return x2d.reshape(bsz, seq, d)
```

```python
import math

import numpy as np
import jax
import jax.numpy as jnp
from jax import lax
from jax.experimental import pallas as pl
from jax.experimental.pallas import tpu as pltpu

F32 = jnp.float32
BF16 = jnp.bfloat16

D_MODEL = 1024
CHUNK = 64
A_HEADS = 8
A_HEAD_DIM = 64
A_WIDTH = A_HEADS * A_HEAD_DIM
IDX_HEADS = 8
IDX_DIM = 32
IDX_WIDTH = IDX_HEADS * IDX_DIM
TOPK_MAX = 256
B_HEADS = 4
B_KEY_DIM = 64
B_VAL_DIM = 128
B_WIDTH = B_HEADS * B_VAL_DIM
B_KEY_WIDTH = B_HEADS * B_KEY_DIM
GATE_RANK = 16
GATE_NORMALIZER = 16.0
N_BUCKETS = 32
MAX_DISTANCE = 128
D_FF = 2816
EPS = 1e-6
NEG = -1e30

LANES = 128
VMEM_LIMIT = 56 * 1024 * 1024

TM_PROJ = 512
TQ = 256
TS_GLA = 512
TM_FFN = 512

A_COLS = 2 * A_WIDTH
VT_ROWS = A_HEADS * LANES
IDX_COLS = IDX_WIDTH + IDX_WIDTH + LANES
B_COLS = 2 * B_KEY_WIDTH + B_WIDTH + B_WIDTH + LANES
W_COLS = A_COLS + IDX_COLS + B_COLS

NT_DIMS = (((1,), (1,)), ((), ()))
TN_DIMS = (((0,), (0,)), ((), ()))


def _dot(a, b):
    return jnp.dot(a, b, preferred_element_type=F32)


def _dot_nt(a, b):
    return lax.dot_general(a, b, NT_DIMS, preferred_element_type=F32)


def _dot_tn(a, b):
    return lax.dot_general(a, b, TN_DIMS, preferred_element_type=F32)


def _rms(x, g):
    return x * lax.rsqrt(jnp.mean(x * x, axis=-1, keepdims=True) + EPS) * g


def _resident(shape):
    return pl.BlockSpec(shape, lambda *_: (0,) * len(shape), pipeline_mode=pl.Buffered(1))


def _t5_bucket_np(rel):
    half = N_BUCKETS // 2
    max_exact = half // 2
    ret = np.where(rel > 0, half, 0)
    n = np.abs(rel)
    nf = np.maximum(n, 1).astype(np.float32)
    large = max_exact + (np.log(nf / np.float32(max_exact))
                         / np.float32(math.log(MAX_DISTANCE / max_exact))
                         * np.float32(half - max_exact)).astype(np.int32)
    large = np.minimum(large, half - 1)
    return (ret + np.where(n < max_exact, n, large)).astype(np.int32)


FAR_BUCKET = int(_t5_bucket_np(np.array([-2 * TQ]))[0])


def _bias_kernel(rb_ref, bkt_ref, o_ref):
    t = pl.program_id(0)
    h = pl.program_id(1)
    bk = bkt_ref[0]
    acc = jnp.zeros(bk.shape, F32)
    for b in range(N_BUCKETS):
        acc = jnp.where(bk == b, rb_ref[b, h], acc)
    acc = acc - rb_ref[FAR_BUCKET, h]
    o_ref[0, 0] = jnp.where(t == 0, 0.0, acc)


def _bias_tables(rel_bias):
    kl = np.arange(TQ)[:, None]
    ql = np.arange(TQ)[None, :]
    buckets = np.stack([_t5_bucket_np(kl - ql - TQ), _t5_bucket_np(kl - ql)])
    return pl.pallas_call(
        _bias_kernel,
        grid=(3, A_HEADS),
        in_specs=[pl.BlockSpec(memory_space=pltpu.SMEM),
                  pl.BlockSpec((1, TQ, TQ), lambda t, h: (jnp.maximum(t - 1, 0), 0, 0))],
        out_specs=pl.BlockSpec((1, 1, TQ, TQ), lambda t, h: (t, h, 0, 0)),
        out_shape=jax.ShapeDtypeStruct((3, A_HEADS, TQ, TQ), F32),
        name="bias_tables",
    )(rel_bias, jnp.asarray(buckets))


def _in_proj_kernel(x_ref, g_ref, w_ref, wv_ref, a_ref, vt_ref, i_ref, b_ref):
    h = _rms(x_ref[...], g_ref[...]).astype(BF16)
    c0 = 0
    a_ref[:, 0:A_WIDTH] = (_dot(h, w_ref[:, c0:c0 + A_WIDTH]) * A_HEAD_DIM ** -0.5).astype(BF16)
    c0 += A_WIDTH
    a_ref[:, A_WIDTH:2 * A_WIDTH] = _dot(h, w_ref[:, c0:c0 + A_WIDTH]).astype(BF16)
    c0 += A_WIDTH
    row = lax.broadcasted_iota(jnp.int32, (VT_ROWS, 1), 0)
    ones = jnp.where(row % LANES >= A_HEAD_DIM, 1.0, 0.0)
    vt_ref[...] = (_dot_nt(wv_ref[...], h) + ones).astype(BF16)
    i_ref[...] = _dot(h, w_ref[:, c0:c0 + IDX_COLS])
    c0 += IDX_COLS
    b_ref[...] = _dot(h, w_ref[:, c0:c0 + B_COLS])


def _pack_w_in(w_in):
    sizes = (A_WIDTH, A_WIDTH, A_WIDTH, IDX_WIDTH, IDX_DIM, IDX_HEADS,
             B_KEY_WIDTH, B_KEY_WIDTH, B_WIDTH, GATE_RANK, B_WIDTH)
    offs = np.cumsum((0,) + sizes)
    qa, ka, va, qi, ki, wi, qb, kb, vb, gd, og = [w_in[:, offs[j]:offs[j + 1]] for j in range(len(sizes))]
    d = w_in.shape[0]
    va_aug = jnp.pad(va.reshape(d, A_HEADS, A_HEAD_DIM), ((0, 0), (0, 0), (0, LANES - A_HEAD_DIM)))
    cols = [qa, ka,
            qi, jnp.tile(ki, (1, IDX_HEADS)), jnp.pad(wi, ((0, 0), (0, LANES - IDX_HEADS))),
            qb, kb, vb, og, jnp.pad(gd, ((0, 0), (0, LANES - GATE_RANK)))]
    return jnp.concatenate(cols, axis=1).astype(BF16), va_aug.reshape(d, VT_ROWS).T.astype(BF16)


def _in_proj(x2d, g, w_packed, wv_t):
    t = x2d.shape[0]
    return pl.pallas_call(
        _in_proj_kernel,
        grid=(t // TM_PROJ,),
        in_specs=[pl.BlockSpec((TM_PROJ, D_MODEL), lambda r: (r, 0)),
                  _resident((1, D_MODEL)),
                  _resident((D_MODEL, W_COLS)),
                  _resident((VT_ROWS, D_MODEL))],
        out_specs=[pl.BlockSpec((TM_PROJ, A_COLS), lambda r: (r, 0)),
                   pl.BlockSpec((VT_ROWS, TM_PROJ), lambda r: (0, r)),
                   pl.BlockSpec((TM_PROJ, IDX_COLS), lambda r: (r, 0)),
                   pl.BlockSpec((TM_PROJ, B_COLS), lambda r: (r, 0))],
        out_shape=[jax.ShapeDtypeStruct((t, A_COLS), BF16),
                   jax.ShapeDtypeStruct((VT_ROWS, t), BF16),
                   jax.ShapeDtypeStruct((t, IDX_COLS), F32),
                   jax.ShapeDtypeStruct((t, B_COLS), F32)],
        compiler_params=pltpu.CompilerParams(dimension_semantics=("parallel",),
                                             vmem_limit_bytes=VMEM_LIMIT),
        name="in_proj",
    )(x2d, g, w_packed, wv_t)


def _rows8(x, op):
    r, c = x.shape
    return op(x.reshape(r // 8, 8, c), axis=0)


def _colmax(x):
    return jnp.max(_rows8(x, jnp.max), axis=0, keepdims=True)


N_BISECT = 20


def _dsa_kernel(q_ref, k_ref, vt_ref, iq_ref, iw_ref, ik_ref, bias_ref, o_ref,
                qi_ref, qa_ref, s_ref, mb_ref, m_ref, acc_ref):
    i = pl.program_id(1)
    nk = i + 1
    npair = (nk + 1) // 2
    big = jnp.float32(3e38)
    kk = jnp.float32(TOPK_MAX)

    def blk(kc):
        return pl.ds(pl.multiple_of(kc * TQ, TQ), TQ)

    def blk2(j):
        return pl.ds(pl.multiple_of(j * 2 * TQ, 2 * TQ), 2 * TQ)

    row = lax.broadcasted_iota(jnp.int32, (TQ, TQ), 0)
    col = lax.broadcasted_iota(jnp.int32, (TQ, TQ), 1)
    adm_diag = row < (col // CHUNK + 1) * CHUNK

    lane_i = lax.broadcasted_iota(jnp.int32, (1, IDX_WIDTH), 1)
    iq = iq_ref[...]
    for h in range(IDX_HEADS):
        qi_ref[h] = jnp.where(lane_i // IDX_DIM == h, iq, 0.0).astype(BF16)
    lane_p = lax.broadcasted_iota(jnp.int32, (1, LANES), 1)
    for h in range(A_HEADS):
        q_pair = q_ref[:, (h // 2) * LANES:(h // 2 + 1) * LANES]
        qa_ref[h] = jnp.where((lane_p < A_HEAD_DIM) == (h % 2 == 0), q_pair, jnp.zeros_like(q_pair))
    w_t = iw_ref[...].T * (IDX_DIM ** -0.5 * IDX_HEADS ** -0.5)

    def score_blk(kc, carry):
        kic = ik_ref[blk(kc), :].astype(BF16)
        acc = jnp.zeros((TQ, TQ), F32)
        for h in range(IDX_HEADS):
            acc = acc + jnp.maximum(_dot_nt(kic, qi_ref[h]), 0.0) * w_t[h:h + 1, :]
        s_ref[blk(kc), :] = jnp.where(jnp.logical_or(kc < i, adm_diag), acc, NEG)
        return carry

    lax.fori_loop(0, nk, score_blk, 0)

    @pl.when(i == 0)
    def _():
        mb_ref[0:TQ, :] = jnp.where(adm_diag, 0.0, NEG)

    @pl.when(i > 0)
    def _():
        @pl.when(nk % 2 == 1)
        def _():
            s_ref[blk(nk), :] = jnp.full((TQ, TQ), NEG, F32)

        def minmax(kc, c):
            s = s_ref[blk(kc), :]
            return jnp.minimum(c[0], _rows8(s, jnp.min)), jnp.maximum(c[1], _rows8(s, jnp.max))
        lo8, hi8 = lax.fori_loop(0, i, minmax, (jnp.full((8, TQ), big), jnp.full((8, TQ), -big)))
        lo = jnp.min(lo8, axis=0, keepdims=True)
        hi = jnp.maximum(jnp.max(hi8, axis=0, keepdims=True), _colmax(s_ref[blk(i), :]))

        def count_ge(t):
            def body(j, acc):
                return acc + _rows8(jnp.where(s_ref[blk2(j), :] >= t, 1.0, 0.0), jnp.sum)
            return jnp.sum(lax.fori_loop(0, npair, body, jnp.zeros((8, TQ), F32)), axis=0, keepdims=True)

        def bisect(_, c):
            lo, hi = c
            mid = 0.5 * lo + 0.5 * hi
            ge = count_ge(mid) >= kk
            return jnp.where(ge, mid, lo), jnp.where(ge, hi, mid)
        lo, hi = lax.fori_loop(0, N_BISECT, bisect, (lo, hi))

        def min_ge(j, acc):
            s = s_ref[blk2(j), :]
            return jnp.minimum(acc, _rows8(jnp.where(s >= lo, s, big), jnp.min))
        v0 = jnp.min(lax.fori_loop(0, npair, min_ge, jnp.full((8, TQ), big)), axis=0, keepdims=True)

        def above(v):
            def body(j, c):
                s = s_ref[blk2(j), :]
                gt = s > v
                return (c[0] + _rows8(jnp.where(gt, 1.0, 0.0), jnp.sum),
                        jnp.minimum(c[1], _rows8(jnp.where(gt, s, big), jnp.min)))
            c8, n8 = lax.fori_loop(0, npair, body, (jnp.zeros((8, TQ), F32), jnp.full((8, TQ), big)))
            return jnp.sum(c8, axis=0, keepdims=True), jnp.min(n8, axis=0, keepdims=True)

        def peel_cond(c):
            return c[2] > 0.5

        def peel(c):
            v, _, _ = c
            c_gt, nxt = above(v)
            done = c_gt < kk
            v_new = jnp.where(done, v, nxt)
            pending = jnp.max(jnp.where(done, 0.0, 1.0))
            return v_new, c_gt, pending
        thr, c_gt, _ = lax.while_loop(peel_cond, peel, (v0, jnp.zeros((1, TQ), F32), jnp.float32(1.0)))
        need = kk - c_gt

        ltri = (row >= col).astype(BF16)

        def select(kc, carry):
            s = s_ref[blk(kc), :]
            tie = s == thr
            pref = _dot(ltri, jnp.where(tie, 1.0, 0.0).astype(BF16))
            sel = jnp.logical_or(s > thr, jnp.logical_and(tie, carry + pref <= need))
            mb_ref[blk(kc), :] = jnp.where(sel, 0.0, NEG)
            return carry + pref[TQ - 1:TQ, :]
        lax.fori_loop(0, nk, select, jnp.zeros((1, TQ), F32))

    m_ref[...] = jnp.full(m_ref.shape, -big, F32)
    acc_ref[...] = jnp.zeros(acc_ref.shape, F32)

    def attend(kc, near):
        mb = mb_ref[blk(kc), :]
        ls, m_olds, m_news = [], [], []
        for h in range(A_HEADS):
            cs = slice((h // 2) * LANES, (h // 2 + 1) * LANES)
            l = _dot_nt(k_ref[blk(kc), cs], qa_ref[h]) + mb
            if near:
                l = l + bias_ref[kc - i + 2, h]
            m_old = m_ref[h, 0:1, :]
            ls.append(l)
            m_olds.append(m_old)
            m_news.append(jnp.maximum(m_old, _colmax(l)))
        for h in range(A_HEADS):
            p = jnp.exp(ls[h] - m_news[h]).astype(BF16)
            acc_ref[h] = (acc_ref[h] * jnp.exp(m_olds[h] - m_news[h])
                          + _dot(vt_ref[h * LANES:(h + 1) * LANES, blk(kc)], p))
            m_ref[h, 0:1, :] = m_news[h]

    def far_blk(kc, carry):
        attend(kc, False)
        return carry

    def near_blk(kc, carry):
        attend(kc, True)
        return carry

    n_far = jnp.maximum(i - 1, 0)
    lax.fori_loop(0, n_far, far_blk, 0)
    lax.fori_loop(n_far, nk, near_blk, 0)

    outs = [acc_ref[h, 0:A_HEAD_DIM, :] / acc_ref[h, A_HEAD_DIM:A_HEAD_DIM + 1, :] for h in range(A_HEADS)]
    o_ref[...] = jnp.concatenate(outs, axis=0).T.astype(o_ref.dtype)


def _dsa(a_out, vt_out, idx_out, bias_tab, bsz, seq):
    nq = seq // TQ
    return pl.pallas_call(
        _dsa_kernel,
        grid=(bsz, nq),
        in_specs=[pl.BlockSpec((TQ, A_WIDTH), lambda b, i: (b * nq + i, 0)),
                  pl.BlockSpec((seq, A_WIDTH), lambda b, i: (b, 1)),
                  pl.BlockSpec((VT_ROWS, seq), lambda b, i: (0, b)),
                  pl.BlockSpec((TQ, IDX_WIDTH), lambda b, i: (b * nq + i, 0)),
                  pl.BlockSpec((TQ, LANES), lambda b, i: (b * nq + i, 2 * IDX_WIDTH // LANES)),
                  pl.BlockSpec((seq, IDX_WIDTH), lambda b, i: (b, 1)),
                  _resident((3, A_HEADS, TQ, TQ))],
        out_specs=pl.BlockSpec((TQ, A_WIDTH), lambda b, i: (b * nq + i, 0)),
        out_shape=jax.ShapeDtypeStruct((bsz * seq, A_WIDTH), BF16),
        scratch_shapes=[pltpu.VMEM((IDX_HEADS, TQ, IDX_WIDTH), BF16),
                        pltpu.VMEM((A_HEADS, TQ, LANES), BF16),
                        pltpu.VMEM((seq, TQ), F32),
                        pltpu.VMEM((seq, TQ), F32),
                        pltpu.VMEM((A_HEADS, 8, TQ), F32),
                        pltpu.VMEM((A_HEADS, LANES, TQ), F32)],
        compiler_params=pltpu.CompilerParams(dimension_semantics=("parallel", "arbitrary"),
                                             vmem_limit_bytes=VMEM_LIMIT),
        name="dsa",
    )(a_out, a_out, vt_out, idx_out, idx_out, idx_out, bias_tab)


def _gla_kernel(qk_ref, v_ref, og_ref, gd_ref, wgu_ref, bg_ref, gn_ref, mcum_ref, mtot_ref,
                o_ref, st_ref, ob_ref):
    @pl.when(pl.program_id(1) == 0)
    def _():
        st_ref[...] = jnp.zeros_like(st_ref)

    z = _dot(gd_ref[...].astype(BF16), wgu_ref[...]) + bg_ref[...]
    g = -(jnp.maximum(-z, 0.0) + jnp.log1p(jnp.exp(-jnp.abs(z)))) * (1.0 / GATE_NORMALIZER)
    g_hi = g.astype(BF16)
    g_lo = (g - g_hi.astype(F32)).astype(BF16)
    mcum = mcum_ref[...]
    mtot = mtot_ref[...]
    bcum = _dot(mcum, g_hi) + _dot(mcum, g_lo)
    btot = _dot(mtot, g_hi) + _dot(mtot, g_lo)

    q = qk_ref[:, 0:B_KEY_WIDTH]
    k = qk_ref[:, B_KEY_WIDTH:2 * B_KEY_WIDTH]
    q_in = q * B_KEY_DIM ** -0.5 * jnp.exp(bcum)
    k_in = (k * jnp.exp(-bcum)).astype(BF16)
    k_dec = k * jnp.exp(btot - bcum)
    e_tot = jnp.exp(btot)

    lane = lax.broadcasted_iota(jnp.int32, (1, LANES), 1)
    halves = (lane < B_KEY_DIM, lane >= B_KEY_DIM)
    r_i = lax.broadcasted_iota(jnp.int32, (CHUNK, CHUNK), 0)
    c_i = lax.broadcasted_iota(jnp.int32, (CHUNK, CHUNK), 1)
    causal = r_i >= c_i

    for pair in range(B_HEADS // 2):
        cs = slice(pair * LANES, (pair + 1) * LANES)
        qp, kd, ki = q_in[:, cs], k_dec[:, cs], k_in[:, cs]
        qm = [jnp.where(hm, qp, 0.0).astype(BF16) for hm in halves]
        kdm = [jnp.where(hm, kd, 0.0).astype(BF16) for hm in halves]
        state = st_ref[pair]
        for c in range(TS_GLA // CHUNK):
            rs = slice(c * CHUNK, (c + 1) * CHUNK)
            sb = state.astype(BF16)
            upd = jnp.zeros_like(state)
            for sub in range(2):
                h = 2 * pair + sub
                vh = v_ref[rs, h * B_VAL_DIM:(h + 1) * B_VAL_DIM].astype(BF16)
                att = jnp.where(causal, _dot_nt(qm[sub][rs], ki[rs]), 0.0).astype(BF16)
                ob_ref[rs, h * B_VAL_DIM:(h + 1) * B_VAL_DIM] = _dot(att, vh) + _dot_nt(qm[sub][rs], sb)
                upd = upd + _dot_tn(vh, kdm[sub][rs])
            state = state * e_tot[c * CHUNK:c * CHUNK + 1, cs] + upd
        st_ref[pair] = state

    og = og_ref[...]
    gate = og * jax.nn.sigmoid(og)
    gn = gn_ref[...]
    for h in range(B_HEADS):
        cs = slice(h * B_VAL_DIM, (h + 1) * B_VAL_DIM)
        o_ref[:, cs] = (_rms(ob_ref[:, cs], gn[:, cs]) * gate[:, cs]).astype(o_ref.dtype)


def _gla(b_out, w_gate_up, b_gate, gla_norm, bsz, seq):
    ns = seq // TS_GLA
    idx = np.arange(TS_GLA)
    same = (idx[:, None] // CHUNK) == (idx[None, :] // CHUNK)
    mcum = jnp.asarray(same & (idx[None, :] <= idx[:, None]), BF16)
    mtot = jnp.asarray(same, BF16)
    wgu = jnp.pad(w_gate_up, ((0, LANES - GATE_RANK), (0, 0))).astype(BF16)
    return pl.pallas_call(
        _gla_kernel,
        grid=(bsz, ns),
        in_specs=[pl.BlockSpec((TS_GLA, 2 * B_KEY_WIDTH), lambda b, j: (b * ns + j, 0)),
                  pl.BlockSpec((TS_GLA, B_WIDTH), lambda b, j: (b * ns + j, 1)),
                  pl.BlockSpec((TS_GLA, B_WIDTH), lambda b, j: (b * ns + j, 2)),
                  pl.BlockSpec((TS_GLA, LANES), lambda b, j: (b * ns + j, (2 * B_KEY_WIDTH + 2 * B_WIDTH) // LANES)),
                  _resident((LANES, B_KEY_WIDTH)),
                  _resident((1, B_KEY_WIDTH)),
                  _resident((1, B_WIDTH)),
                  _resident((TS_GLA, TS_GLA)),
                  _resident((TS_GLA, TS_GLA))],
        out_specs=pl.BlockSpec((TS_GLA, B_WIDTH), lambda b, j: (b * ns + j, 0)),
        out_shape=jax.ShapeDtypeStruct((bsz * seq, B_WIDTH), BF16),
        scratch_shapes=[pltpu.VMEM((B_HEADS // 2, B_VAL_DIM, LANES), F32),
                        pltpu.VMEM((TS_GLA, B_WIDTH), F32)],
        compiler_params=pltpu.CompilerParams(dimension_semantics=("parallel", "arbitrary"),
                                             vmem_limit_bytes=VMEM_LIMIT),
        name="gla",
    )(b_out, b_out, b_out, b_out, wgu, b_gate.reshape(1, -1), gla_norm.reshape(1, -1), mcum, mtot)


def _out_ffn_kernel(x_ref, oa_ref, ob_ref, wo_ref, nf_ref, wg_ref, wu_ref, wd_ref, nl_ref, o_ref):
    mix = jnp.concatenate([oa_ref[...], ob_ref[...]], axis=1)
    x1 = x_ref[...] + _dot(mix, wo_ref[...])
    h = _rms(x1, nf_ref[...]).astype(BF16)
    g = _dot(h, wg_ref[...])
    u = _dot(h, wu_ref[...])
    a = (g * jax.nn.sigmoid(g) * u).astype(BF16)
    o_ref[...] = _rms(x1 + _dot(a, wd_ref[...]), nl_ref[...])


def _out_ffn(x2d, o_a, o_b, w_out, norm_ffn, w_gate, w_up, w_down, norm_final):
    t = x2d.shape[0]
    row = lambda r: (r, 0)
    return pl.pallas_call(
        _out_ffn_kernel,
        grid=(t // TM_FFN,),
        in_specs=[pl.BlockSpec((TM_FFN, D_MODEL), row),
                  pl.BlockSpec((TM_FFN, A_WIDTH), row),
                  pl.BlockSpec((TM_FFN, B_WIDTH), row),
                  _resident((A_WIDTH + B_WIDTH, D_MODEL)),
                  _resident((1, D_MODEL)),
                  _resident((D_MODEL, D_FF)),
                  _resident((D_MODEL, D_FF)),
                  _resident((D_FF, D_MODEL)),
                  _resident((1, D_MODEL))],
        out_specs=pl.BlockSpec((TM_FFN, D_MODEL), row),
        out_shape=jax.ShapeDtypeStruct((t, D_MODEL), F32),
        compiler_params=pltpu.CompilerParams(dimension_semantics=("parallel",),
                                             vmem_limit_bytes=VMEM_LIMIT),
        name="out_ffn",
    )(x2d, o_a, o_b, w_out, norm_ffn, w_gate, w_up, w_down, norm_final)


def kernel(x, norm_mix, w_in, w_gate_up, b_gate, gla_norm, w_out, rel_bias,
           norm_ffn, w_ffn_gate, w_ffn_up, w_ffn_down, norm_final):
    bsz, seq, d = x.shape
    assert w_in.shape[0] == 1, "fused final norm assumes a single layer"
    assert d == D_MODEL and seq % TS_GLA == 0 and seq % TQ == 0 and seq // 4 >= TOPK_MAX
    bias_tab = _bias_tables(rel_bias)
    x2d = x.reshape(bsz * seq, d)
    a_out, vt_out, idx_out, b_out = _in_proj(x2d, norm_mix[0].reshape(1, d), *_pack_w_in(w_in[0]))
    o_a = _dsa(a_out, vt_out, idx_out, bias_tab, bsz, seq)
    o_b = _gla(b_out, w_gate_up[0], b_gate[0], gla_norm[0], bsz, seq)
    out = _out_ffn(x2d, o_a, o_b, w_out[0].astype(BF16), norm_ffn[0].reshape(1, d),
                   w_ffn_gate[0].astype(BF16), w_ffn_up[0].astype(BF16),
                   w_ffn_down[0].astype(BF16), norm_final.reshape(1, d))
    return out.reshape(bsz, seq, d)
```

```python
import math

import numpy as np
import jax
import jax.numpy as jnp
from jax import lax
from jax.experimental import pallas as pl
from jax.experimental.pallas import tpu as pltpu

F32 = jnp.float32
BF16 = jnp.bfloat16

D_MODEL = 1024
CHUNK = 64
A_HEADS = 8
A_HEAD_DIM = 64
A_WIDTH = A_HEADS * A_HEAD_DIM
IDX_HEADS = 8
IDX_DIM = 32
IDX_WIDTH = IDX_HEADS * IDX_DIM
TOPK_MAX = 256
B_HEADS = 4
B_KEY_DIM = 64
B_VAL_DIM = 128
B_WIDTH = B_HEADS * B_VAL_DIM
B_KEY_WIDTH = B_HEADS * B_KEY_DIM
GATE_RANK = 16
GATE_NORMALIZER = 16.0
N_BUCKETS = 32
MAX_DISTANCE = 128
D_FF = 2816
EPS = 1e-6
NEG = -1e30
LOG2E = math.log2(math.e)

LANES = 128
VMEM_LIMIT = 56 * 1024 * 1024

TM_PROJ = 512
TQ = 256
TS_GLA = 512
TM_FFN = 512

A_COLS = 2 * A_WIDTH
VT_ROWS = A_HEADS * LANES
IDX_COLS = IDX_WIDTH + IDX_WIDTH + LANES
B_COLS = 2 * B_KEY_WIDTH + B_WIDTH + B_WIDTH + LANES
W_COLS = A_COLS + IDX_COLS + B_COLS

NT_DIMS = (((1,), (1,)), ((), ()))
TN_DIMS = (((0,), (0,)), ((), ()))


def _dot(a, b):
    return jnp.dot(a, b, preferred_element_type=F32)


def _dot_nt(a, b):
    return lax.dot_general(a, b, NT_DIMS, preferred_element_type=F32)


def _dot_tn(a, b):
    return lax.dot_general(a, b, TN_DIMS, preferred_element_type=F32)


def _rms(x, g):
    return x * lax.rsqrt(jnp.mean(x * x, axis=-1, keepdims=True) + EPS) * g


def _resident(shape):
    return pl.BlockSpec(shape, lambda *_: (0,) * len(shape), pipeline_mode=pl.Buffered(1))


def _t5_bucket_np(rel):
    half = N_BUCKETS // 2
    max_exact = half // 2
    ret = np.where(rel > 0, half, 0)
    n = np.abs(rel)
    nf = np.maximum(n, 1).astype(np.float32)
    large = max_exact + (np.log(nf / np.float32(max_exact))
                         / np.float32(math.log(MAX_DISTANCE / max_exact))
                         * np.float32(half - max_exact)).astype(np.int32)
    large = np.minimum(large, half - 1)
    return (ret + np.where(n < max_exact, n, large)).astype(np.int32)


FAR_BUCKET = int(_t5_bucket_np(np.array([-2 * TQ]))[0])


def _bias_kernel(rb_ref, bkt_ref, o_ref):
    t = pl.program_id(0)
    h = pl.program_id(1)
    bk = bkt_ref[0]
    acc = jnp.zeros(bk.shape, F32)
    for b in range(N_BUCKETS):
        acc = jnp.where(bk == b, rb_ref[b, h], acc)
    acc = (acc - rb_ref[FAR_BUCKET, h]) * LOG2E
    o_ref[0, 0] = jnp.where(t == 0, 0.0, acc)


def _bias_tables(rel_bias):
    kl = np.arange(TQ)[:, None]
    ql = np.arange(TQ)[None, :]
    buckets = np.stack([_t5_bucket_np(kl - ql - TQ), _t5_bucket_np(kl - ql)])
    return pl.pallas_call(
        _bias_kernel,
        grid=(3, A_HEADS),
        in_specs=[pl.BlockSpec(memory_space=pltpu.SMEM),
                  pl.BlockSpec((1, TQ, TQ), lambda t, h: (jnp.maximum(t - 1, 0), 0, 0))],
        out_specs=pl.BlockSpec((1, 1, TQ, TQ), lambda t, h: (t, h, 0, 0)),
        out_shape=jax.ShapeDtypeStruct((3, A_HEADS, TQ, TQ), F32),
        name="bias_tables",
    )(rel_bias, jnp.asarray(buckets))


def _in_proj_kernel(x_ref, g_ref, w_ref, wv_ref, a_ref, vt_ref, i_ref, b_ref):
    h = _rms(x_ref[...], g_ref[...]).astype(BF16)
    c0 = 0
    a_ref[:, 0:A_WIDTH] = (_dot(h, w_ref[:, c0:c0 + A_WIDTH]) * (A_HEAD_DIM ** -0.5 * LOG2E)).astype(BF16)
    c0 += A_WIDTH
    a_ref[:, A_WIDTH:2 * A_WIDTH] = _dot(h, w_ref[:, c0:c0 + A_WIDTH]).astype(BF16)
    c0 += A_WIDTH
    row = lax.broadcasted_iota(jnp.int32, (VT_ROWS, 1), 0)
    ones = jnp.where(row % LANES >= A_HEAD_DIM, 1.0, 0.0)
    vt_ref[...] = (_dot_nt(wv_ref[...], h) + ones).astype(BF16)
    i_ref[...] = _dot(h, w_ref[:, c0:c0 + IDX_COLS])
    c0 += IDX_COLS
    b_ref[...] = _dot(h, w_ref[:, c0:c0 + B_COLS])


def _pack_w_in(w_in):
    sizes = (A_WIDTH, A_WIDTH, A_WIDTH, IDX_WIDTH, IDX_DIM, IDX_HEADS,
             B_KEY_WIDTH, B_KEY_WIDTH, B_WIDTH, GATE_RANK, B_WIDTH)
    offs = np.cumsum((0,) + sizes)
    qa, ka, va, qi, ki, wi, qb, kb, vb, gd, og = [w_in[:, offs[j]:offs[j + 1]] for j in range(len(sizes))]
    d = w_in.shape[0]
    va_aug = jnp.pad(va.reshape(d, A_HEADS, A_HEAD_DIM), ((0, 0), (0, 0), (0, LANES - A_HEAD_DIM)))
    cols = [qa, ka,
            qi, jnp.tile(ki, (1, IDX_HEADS)), jnp.pad(wi, ((0, 0), (0, LANES - IDX_HEADS))),
            qb, kb, vb, og, jnp.pad(gd, ((0, 0), (0, LANES - GATE_RANK)))]
    return jnp.concatenate(cols, axis=1).astype(BF16), va_aug.reshape(d, VT_ROWS).T.astype(BF16)


def _in_proj(x2d, g, w_packed, wv_t):
    t = x2d.shape[0]
    return pl.pallas_call(
        _in_proj_kernel,
        grid=(t // TM_PROJ,),
        in_specs=[pl.BlockSpec((TM_PROJ, D_MODEL), lambda r: (r, 0)),
                  _resident((1, D_MODEL)),
                  _resident((D_MODEL, W_COLS)),
                  _resident((VT_ROWS, D_MODEL))],
        out_specs=[pl.BlockSpec((TM_PROJ, A_COLS), lambda r: (r, 0)),
                   pl.BlockSpec((VT_ROWS, TM_PROJ), lambda r: (0, r)),
                   pl.BlockSpec((TM_PROJ, IDX_COLS), lambda r: (r, 0)),
                   pl.BlockSpec((TM_PROJ, B_COLS), lambda r: (r, 0))],
        out_shape=[jax.ShapeDtypeStruct((t, A_COLS), BF16),
                   jax.ShapeDtypeStruct((VT_ROWS, t), BF16),
                   jax.ShapeDtypeStruct((t, IDX_COLS), F32),
                   jax.ShapeDtypeStruct((t, B_COLS), F32)],
        compiler_params=pltpu.CompilerParams(dimension_semantics=("parallel",),
                                             vmem_limit_bytes=VMEM_LIMIT),
        name="in_proj",
    )(x2d, g, w_packed, wv_t)


def _rows8(x, op):
    r, c = x.shape
    if r % 32 == 0 and r > 32:
        return op(op(x.reshape(r // 32, 32, c), axis=0).reshape(4, 8, c), axis=0)
    return op(x.reshape(r // 8, 8, c), axis=0)


def _colmax(x):
    return jnp.max(_rows8(x, jnp.max), axis=0, keepdims=True)


N_BISECT = 20
QK_AHEAD = A_HEADS


def _dsa_kernel(q_ref, k_ref, vt_ref, iq_ref, iw_ref, ik_ref, bias_ref, o_ref,
                qi_ref, qa_ref, s_ref, mb_ref, m_ref, acc_ref):
    i = pl.program_id(1)
    nk = i + 1
    npair = (nk + 1) // 2
    big = jnp.float32(3e38)
    kk = jnp.float32(TOPK_MAX)

    def blk(kc):
        return pl.ds(pl.multiple_of(kc * TQ, TQ), TQ)

    def blk2(j):
        return pl.ds(pl.multiple_of(j * 2 * TQ, 2 * TQ), 2 * TQ)

    row = lax.broadcasted_iota(jnp.int32, (TQ, TQ), 0)
    col = lax.broadcasted_iota(jnp.int32, (TQ, TQ), 1)
    adm_diag = row < (col // CHUNK + 1) * CHUNK

    lane_i = lax.broadcasted_iota(jnp.int32, (1, IDX_WIDTH), 1)
    iq = iq_ref[...]
    for h in range(IDX_HEADS):
        qi_ref[h] = jnp.where(lane_i // IDX_DIM == h, iq, 0.0).astype(BF16)
    lane_p = lax.broadcasted_iota(jnp.int32, (1, LANES), 1)
    for h in range(A_HEADS):
        q_pair = q_ref[:, (h // 2) * LANES:(h // 2 + 1) * LANES]
        qa_ref[h] = jnp.where((lane_p < A_HEAD_DIM) == (h % 2 == 0), q_pair, jnp.zeros_like(q_pair))
    w_t = iw_ref[...].T * (IDX_DIM ** -0.5 * IDX_HEADS ** -0.5)

    def score_blk(kc, carry):
        kic = ik_ref[blk(kc), :].astype(BF16)
        acc = jnp.zeros((TQ, TQ), F32)
        for h in range(IDX_HEADS):
            acc = acc + jnp.maximum(_dot_nt(kic, qi_ref[h]), 0.0) * w_t[h:h + 1, :]
        s_ref[blk(kc), :] = jnp.where(jnp.logical_or(kc < i, adm_diag), acc, NEG)
        return carry

    lax.fori_loop(0, nk, score_blk, 0)

    @pl.when(i == 0)
    def _():
        mb_ref[0:TQ, :] = jnp.where(adm_diag, 0.0, NEG)

    @pl.when(i > 0)
    def _():
        @pl.when(nk % 2 == 1)
        def _():
            s_ref[blk(nk), :] = jnp.full((TQ, TQ), NEG, F32)

        def minmax(kc, c):
            s = s_ref[blk(kc), :]
            return jnp.minimum(c[0], _rows8(s, jnp.min)), jnp.maximum(c[1], _rows8(s, jnp.max))
        lo8, hi8 = lax.fori_loop(0, i, minmax, (jnp.full((8, TQ), big), jnp.full((8, TQ), -big)))
        lo = jnp.min(lo8, axis=0, keepdims=True)
        hi = jnp.maximum(jnp.max(hi8, axis=0, keepdims=True), _colmax(s_ref[blk(i), :]))

        def count_ge(t):
            def body(j, acc):
                return acc + _rows8(jnp.where(s_ref[blk2(j), :] >= t, 1.0, 0.0), jnp.sum)
            return jnp.sum(lax.fori_loop(0, npair, body, jnp.zeros((8, TQ), F32)), axis=0, keepdims=True)

        def bisect(_, c):
            lo, hi = c
            mid = 0.5 * lo + 0.5 * hi
            ge = count_ge(mid) >= kk
            return jnp.where(ge, mid, lo), jnp.where(ge, hi, mid)
        lo, hi = lax.fori_loop(0, N_BISECT, bisect, (lo, hi))

        def min_ge(j, acc):
            s = s_ref[blk2(j), :]
            return jnp.minimum(acc, _rows8(jnp.where(s >= lo, s, big), jnp.min))
        v0 = jnp.min(lax.fori_loop(0, npair, min_ge, jnp.full((8, TQ), big)), axis=0, keepdims=True)

        def above(v):
            def body(j, c):
                s = s_ref[blk2(j), :]
                gt = s > v
                return (c[0] + _rows8(jnp.where(gt, 1.0, 0.0), jnp.sum),
                        jnp.minimum(c[1], _rows8(jnp.where(gt, s, big), jnp.min)))
            c8, n8 = lax.fori_loop(0, npair, body, (jnp.zeros((8, TQ), F32), jnp.full((8, TQ), big)))
            return jnp.sum(c8, axis=0, keepdims=True), jnp.min(n8, axis=0, keepdims=True)

        def peel_cond(c):
            return c[2] > 0.5

        def peel(c):
            v, _, _ = c
            c_gt, nxt = above(v)
            done = c_gt < kk
            v_new = jnp.where(done, v, nxt)
            pending = jnp.max(jnp.where(done, 0.0, 1.0))
            return v_new, c_gt, pending
        thr, c_gt, _ = lax.while_loop(peel_cond, peel, (v0, jnp.zeros((1, TQ), F32), jnp.float32(1.0)))
        need = kk - c_gt

        ltri = (row >= col).astype(BF16)

        def select(kc, carry):
            s = s_ref[blk(kc), :]
            tie = s == thr
            pref = _dot(ltri, jnp.where(tie, 1.0, 0.0).astype(BF16))
            sel = jnp.logical_or(s > thr, jnp.logical_and(tie, carry + pref <= need))
            mb_ref[blk(kc), :] = jnp.where(sel, 0.0, NEG)
            return carry + pref[TQ - 1:TQ, :]
        lax.fori_loop(0, nk, select, jnp.zeros((1, TQ), F32))

    m_ref[...] = jnp.full(m_ref.shape, -big, F32)
    acc_ref[...] = jnp.zeros(acc_ref.shape, F32)

    def attend(kc, near):
        mb = mb_ref[blk(kc), :]
        ls, m_olds, m_news = [], [], []

        def logits(h):
            cs = slice((h // 2) * LANES, (h // 2 + 1) * LANES)
            l = _dot_nt(k_ref[blk(kc), cs], qa_ref[h]) + mb
            if near:
                l = l + bias_ref[kc - i + 2, h]
            m_old = m_ref[h, 0:1, :]
            ls.append(l)
            m_olds.append(m_old)
            m_news.append(jnp.maximum(m_old, _colmax(l)))

        for h in range(QK_AHEAD):
            logits(h)
        for h in range(A_HEADS):
            if h + QK_AHEAD < A_HEADS:
                logits(h + QK_AHEAD)
            p = jnp.exp2(ls[h] - m_news[h]).astype(BF16)
            acc_ref[h] = (acc_ref[h] * jnp.exp2(m_olds[h] - m_news[h])
                          + _dot(vt_ref[h * LANES:(h + 1) * LANES, blk(kc)], p))
            m_ref[h, 0:1, :] = m_news[h]

    def far_blk(kc, carry):
        attend(kc, False)
        return carry

    def near_blk(kc, carry):
        attend(kc, True)
        return carry

    n_far = jnp.maximum(i - 1, 0)
    lax.fori_loop(0, n_far, far_blk, 0)
    lax.fori_loop(n_far, nk, near_blk, 0)

    outs = [acc_ref[h, 0:A_HEAD_DIM, :] / acc_ref[h, A_HEAD_DIM:A_HEAD_DIM + 1, :] for h in range(A_HEADS)]
    o_ref[...] = jnp.concatenate(outs, axis=0).T.astype(o_ref.dtype)


def _dsa(a_out, vt_out, idx_out, bias_tab, bsz, seq):
    nq = seq // TQ
    return pl.pallas_call(
        _dsa_kernel,
        grid=(bsz, nq),
        in_specs=[pl.BlockSpec((TQ, A_WIDTH), lambda b, i: (b * nq + i, 0)),
                  pl.BlockSpec((seq, A_WIDTH), lambda b, i: (b, 1)),
                  pl.BlockSpec((VT_ROWS, seq), lambda b, i: (0, b)),
                  pl.BlockSpec((TQ, IDX_WIDTH), lambda b, i: (b * nq + i, 0)),
                  pl.BlockSpec((TQ, LANES), lambda b, i: (b * nq + i, 2 * IDX_WIDTH // LANES)),
                  pl.BlockSpec((seq, IDX_WIDTH), lambda b, i: (b, 1)),
                  _resident((3, A_HEADS, TQ, TQ))],
        out_specs=pl.BlockSpec((TQ, A_WIDTH), lambda b, i: (b * nq + i, 0)),
        out_shape=jax.ShapeDtypeStruct((bsz * seq, A_WIDTH), BF16),
        scratch_shapes=[pltpu.VMEM((IDX_HEADS, TQ, IDX_WIDTH), BF16),
                        pltpu.VMEM((A_HEADS, TQ, LANES), BF16),
                        pltpu.VMEM((seq, TQ), F32),
                        pltpu.VMEM((seq, TQ), F32),
                        pltpu.VMEM((A_HEADS, 8, TQ), F32),
                        pltpu.VMEM((A_HEADS, LANES, TQ), F32)],
        compiler_params=pltpu.CompilerParams(dimension_semantics=("parallel", "arbitrary"),
                                             vmem_limit_bytes=VMEM_LIMIT),
        name="dsa",
    )(a_out, a_out, vt_out, idx_out, idx_out, idx_out, bias_tab)


def _gla_kernel(qk_ref, v_ref, og_ref, gd_ref, wgu_ref, bg_ref, gn_ref, mcum_ref, mtot_ref,
                o_ref, st_ref, ob_ref):
    @pl.when(pl.program_id(1) == 0)
    def _():
        st_ref[...] = jnp.zeros_like(st_ref)

    z = _dot(gd_ref[...].astype(BF16), wgu_ref[...]) + bg_ref[...]
    g = -(jnp.maximum(-z, 0.0) + jnp.log1p(jnp.exp(-jnp.abs(z)))) * (1.0 / GATE_NORMALIZER)
    g_hi = g.astype(BF16)
    g_lo = (g - g_hi.astype(F32)).astype(BF16)
    mcum = mcum_ref[...]
    mtot = mtot_ref[...]
    bcum = _dot(mcum, g_hi) + _dot(mcum, g_lo)
    btot = _dot(mtot, g_hi) + _dot(mtot, g_lo)

    q = qk_ref[:, 0:B_KEY_WIDTH]
    k = qk_ref[:, B_KEY_WIDTH:2 * B_KEY_WIDTH]
    q_in = q * B_KEY_DIM ** -0.5 * jnp.exp(bcum)
    k_in = (k * jnp.exp(-bcum)).astype(BF16)
    k_dec = k * jnp.exp(btot - bcum)
    e_tot = jnp.exp(btot)

    lane = lax.broadcasted_iota(jnp.int32, (1, LANES), 1)
    halves = (lane < B_KEY_DIM, lane >= B_KEY_DIM)
    r_i = lax.broadcasted_iota(jnp.int32, (CHUNK, CHUNK), 0)
    c_i = lax.broadcasted_iota(jnp.int32, (CHUNK, CHUNK), 1)
    causal = r_i >= c_i

    for pair in range(B_HEADS // 2):
        cs = slice(pair * LANES, (pair + 1) * LANES)
        qp, kd, ki = q_in[:, cs], k_dec[:, cs], k_in[:, cs]
        qm = [jnp.where(hm, qp, 0.0).astype(BF16) for hm in halves]
        kdm = [jnp.where(hm, kd, 0.0).astype(BF16) for hm in halves]
        state = st_ref[pair]
        for c in range(TS_GLA // CHUNK):
            rs = slice(c * CHUNK, (c + 1) * CHUNK)
            sb = state.astype(BF16)
            upd = jnp.zeros_like(state)
            for sub in range(2):
                h = 2 * pair + sub
                vh = v_ref[rs, h * B_VAL_DIM:(h + 1) * B_VAL_DIM].astype(BF16)
                att = jnp.where(causal, _dot_nt(qm[sub][rs], ki[rs]), 0.0).astype(BF16)
                ob_ref[rs, h * B_VAL_DIM:(h + 1) * B_VAL_DIM] = _dot(att, vh) + _dot_nt(qm[sub][rs], sb)
                upd = upd + _dot_tn(vh, kdm[sub][rs])
            state = state * e_tot[c * CHUNK:c * CHUNK + 1, cs] + upd
        st_ref[pair] = state

    og = og_ref[...]
    gate = og * jax.nn.sigmoid(og)
    gn = gn_ref[...]
    for h in range(B_HEADS):
        cs = slice(h * B_VAL_DIM, (h + 1) * B_VAL_DIM)
        o_ref[:, cs] = (_rms(ob_ref[:, cs], gn[:, cs]) * gate[:, cs]).astype(o_ref.dtype)


def _gla(b_out, w_gate_up, b_gate, gla_norm, bsz, seq):
    ns = seq // TS_GLA
    idx = np.arange(TS_GLA)
    same = (idx[:, None] // CHUNK) == (idx[None, :] // CHUNK)
    mcum = jnp.asarray(same & (idx[None, :] <= idx[:, None]), BF16)
    mtot = jnp.asarray(same, BF16)
    wgu = jnp.pad(w_gate_up, ((0, LANES - GATE_RANK), (0, 0))).astype(BF16)
    return pl.pallas_call(
        _gla_kernel,
        grid=(bsz, ns),
        in_specs=[pl.BlockSpec((TS_GLA, 2 * B_KEY_WIDTH), lambda b, j: (b * ns + j, 0)),
                  pl.BlockSpec((TS_GLA, B_WIDTH), lambda b, j: (b * ns + j, 1)),
                  pl.BlockSpec((TS_GLA, B_WIDTH), lambda b, j: (b * ns + j, 2)),
                  pl.BlockSpec((TS_GLA, LANES), lambda b, j: (b * ns + j, (2 * B_KEY_WIDTH + 2 * B_WIDTH) // LANES)),
                  _resident((LANES, B_KEY_WIDTH)),
                  _resident((1, B_KEY_WIDTH)),
                  _resident((1, B_WIDTH)),
                  _resident((TS_GLA, TS_GLA)),
                  _resident((TS_GLA, TS_GLA))],
        out_specs=pl.BlockSpec((TS_GLA, B_WIDTH), lambda b, j: (b * ns + j, 0)),
        out_shape=jax.ShapeDtypeStruct((bsz * seq, B_WIDTH), BF16),
        scratch_shapes=[pltpu.VMEM((B_HEADS // 2, B_VAL_DIM, LANES), F32),
                        pltpu.VMEM((TS_GLA, B_WIDTH), F32)],
        compiler_params=pltpu.CompilerParams(dimension_semantics=("parallel", "arbitrary"),
                                             vmem_limit_bytes=VMEM_LIMIT),
        name="gla",
    )(b_out, b_out, b_out, b_out, wgu, b_gate.reshape(1, -1), gla_norm.reshape(1, -1), mcum, mtot)


def _out_ffn_kernel(x_ref, oa_ref, ob_ref, wo_ref, nf_ref, wg_ref, wu_ref, wd_ref, nl_ref, o_ref):
    mix = jnp.concatenate([oa_ref[...], ob_ref[...]], axis=1)
    x1 = x_ref[...] + _dot(mix, wo_ref[...])
    h = _rms(x1, nf_ref[...]).astype(BF16)
    g = _dot(h, wg_ref[...])
    u = _dot(h, wu_ref[...])
    a = (g * jax.nn.sigmoid(g) * u).astype(BF16)
    o_ref[...] = _rms(x1 + _dot(a, wd_ref[...]), nl_ref[...])


def _out_ffn(x2d, o_a, o_b, w_out, norm_ffn, w_gate, w_up, w_down, norm_final):
    t = x2d.shape[0]
    row = lambda r: (r, 0)
    return pl.pallas_call(
        _out_ffn_kernel,
        grid=(t // TM_FFN,),
        in_specs=[pl.BlockSpec((TM_FFN, D_MODEL), row),
                  pl.BlockSpec((TM_FFN, A_WIDTH), row),
                  pl.BlockSpec((TM_FFN, B_WIDTH), row),
                  _resident((A_WIDTH + B_WIDTH, D_MODEL)),
                  _resident((1, D_MODEL)),
                  _resident((D_MODEL, D_FF)),
                  _resident((D_MODEL, D_FF)),
                  _resident((D_FF, D_MODEL)),
                  _resident((1, D_MODEL))],
        out_specs=pl.BlockSpec((TM_FFN, D_MODEL), row),
        out_shape=jax.ShapeDtypeStruct((t, D_MODEL), F32),
        compiler_params=pltpu.CompilerParams(dimension_semantics=("parallel",),
                                             vmem_limit_bytes=VMEM_LIMIT),
        name="out_ffn",
    )(x2d, o_a, o_b, w_out, norm_ffn, w_gate, w_up, w_down, norm_final)


def kernel(x, norm_mix, w_in, w_gate_up, b_gate, gla_norm, w_out, rel_bias,
           norm_ffn, w_ffn_gate, w_ffn_up, w_ffn_down, norm_final):
    bsz, seq, d = x.shape
    assert w_in.shape[0] == 1, "fused final norm assumes a single layer"
    assert d == D_MODEL and seq % TS_GLA == 0 and seq % TQ == 0 and seq // 4 >= TOPK_MAX
    bias_tab = _bias_tables(rel_bias)
    x2d = x.reshape(bsz * seq, d)
    a_out, vt_out, idx_out, b_out = _in_proj(x2d, norm_mix[0].reshape(1, d), *_pack_w_in(w_in[0]))
    o_a = _dsa(a_out, vt_out, idx_out, bias_tab, bsz, seq)
    o_b = _gla(b_out, w_gate_up[0], b_gate[0], gla_norm[0], bsz, seq)
    out = _out_ffn(x2d, o_a, o_b, w_out[0].astype(BF16), norm_ffn[0].reshape(1, d),
                   w_ffn_gate[0].astype(BF16), w_ffn_up[0].astype(BF16),
                   w_ffn_down[0].astype(BF16), norm_final.reshape(1, d))
    return out.reshape(bsz, seq, d)
```

```python
import math

import numpy as np
import jax
import jax.numpy as jnp
from jax import lax
from jax.experimental import pallas as pl
from jax.experimental.pallas import tpu as pltpu

F32 = jnp.float32
BF16 = jnp.bfloat16

D_MODEL = 1024
CHUNK = 64
A_HEADS = 8
A_HEAD_DIM = 64
A_WIDTH = A_HEADS * A_HEAD_DIM
IDX_HEADS = 8
IDX_DIM = 32
IDX_WIDTH = IDX_HEADS * IDX_DIM
TOPK_MAX = 256
B_HEADS = 4
B_KEY_DIM = 64
B_VAL_DIM = 128
B_WIDTH = B_HEADS * B_VAL_DIM
B_KEY_WIDTH = B_HEADS * B_KEY_DIM
GATE_RANK = 16
GATE_NORMALIZER = 16.0
N_BUCKETS = 32
MAX_DISTANCE = 128
D_FF = 2816
EPS = 1e-6
NEG = -1e30
LOG2E = math.log2(math.e)

LANES = 128
VMEM_LIMIT = 56 * 1024 * 1024

TM_PROJ = 512
TQ = 256
TS_GLA = 512
TM_FFN = 512

A_COLS = 2 * A_WIDTH
VT_ROWS = A_HEADS * LANES
IDX_COLS = IDX_WIDTH + IDX_WIDTH + LANES
B_COLS = 2 * B_KEY_WIDTH + B_WIDTH + B_WIDTH + LANES
W_COLS = A_COLS + IDX_COLS + B_COLS

NT_DIMS = (((1,), (1,)), ((), ()))
TN_DIMS = (((0,), (0,)), ((), ()))


def _dot(a, b):
    return jnp.dot(a, b, preferred_element_type=F32)


def _dot_nt(a, b):
    return lax.dot_general(a, b, NT_DIMS, preferred_element_type=F32)


def _dot_tn(a, b):
    return lax.dot_general(a, b, TN_DIMS, preferred_element_type=F32)


def _rms(x, g):
    return x * lax.rsqrt(jnp.mean(x * x, axis=-1, keepdims=True) + EPS) * g


def _resident(shape):
    return pl.BlockSpec(shape, lambda *_: (0,) * len(shape), pipeline_mode=pl.Buffered(1))


def _t5_bucket_np(rel):
    half = N_BUCKETS // 2
    max_exact = half // 2
    ret = np.where(rel > 0, half, 0)
    n = np.abs(rel)
    nf = np.maximum(n, 1).astype(np.float32)
    large = max_exact + (np.log(nf / np.float32(max_exact))
                         / np.float32(math.log(MAX_DISTANCE / max_exact))
                         * np.float32(half - max_exact)).astype(np.int32)
    large = np.minimum(large, half - 1)
    return (ret + np.where(n < max_exact, n, large)).astype(np.int32)


FAR_BUCKET = int(_t5_bucket_np(np.array([-2 * TQ]))[0])


def _bias_kernel(rb_ref, bkt_ref, o_ref):
    t = pl.program_id(0)
    h = pl.program_id(1)
    bk = bkt_ref[0]
    acc = jnp.zeros(bk.shape, F32)
    for b in range(N_BUCKETS):
        acc = jnp.where(bk == b, rb_ref[b, h], acc)
    acc = (acc - rb_ref[FAR_BUCKET, h]) * LOG2E
    o_ref[0, 0] = jnp.where(t == 0, 0.0, acc)


def _bias_tables(rel_bias):
    kl = np.arange(TQ)[:, None]
    ql = np.arange(TQ)[None, :]
    buckets = np.stack([_t5_bucket_np(kl - ql - TQ), _t5_bucket_np(kl - ql)])
    return pl.pallas_call(
        _bias_kernel,
        grid=(3, A_HEADS),
        in_specs=[pl.BlockSpec(memory_space=pltpu.SMEM),
                  pl.BlockSpec((1, TQ, TQ), lambda t, h: (jnp.maximum(t - 1, 0), 0, 0))],
        out_specs=pl.BlockSpec((1, 1, TQ, TQ), lambda t, h: (t, h, 0, 0)),
        out_shape=jax.ShapeDtypeStruct((3, A_HEADS, TQ, TQ), F32),
        name="bias_tables",
    )(rel_bias, jnp.asarray(buckets))


def _in_proj_kernel(x_ref, g_ref, w_ref, wv_ref, a_ref, vt_ref, i_ref, b_ref):
    h = _rms(x_ref[...], g_ref[...]).astype(BF16)
    c0 = 0
    a_ref[:, 0:A_WIDTH] = (_dot(h, w_ref[:, c0:c0 + A_WIDTH]) * (A_HEAD_DIM ** -0.5 * LOG2E)).astype(BF16)
    c0 += A_WIDTH
    a_ref[:, A_WIDTH:2 * A_WIDTH] = _dot(h, w_ref[:, c0:c0 + A_WIDTH]).astype(BF16)
    c0 += A_WIDTH
    row = lax.broadcasted_iota(jnp.int32, (VT_ROWS, 1), 0)
    ones = jnp.where(row % LANES >= A_HEAD_DIM, 1.0, 0.0)
    vt_ref[...] = (_dot_nt(wv_ref[...], h) + ones).astype(BF16)
    i_ref[...] = _dot(h, w_ref[:, c0:c0 + IDX_COLS])
    c0 += IDX_COLS
    b_ref[...] = _dot(h, w_ref[:, c0:c0 + B_COLS])


def _pack_w_in(w_in):
    sizes = (A_WIDTH, A_WIDTH, A_WIDTH, IDX_WIDTH, IDX_DIM, IDX_HEADS,
             B_KEY_WIDTH, B_KEY_WIDTH, B_WIDTH, GATE_RANK, B_WIDTH)
    offs = np.cumsum((0,) + sizes)
    qa, ka, va, qi, ki, wi, qb, kb, vb, gd, og = [w_in[:, offs[j]:offs[j + 1]] for j in range(len(sizes))]
    d = w_in.shape[0]
    va_aug = jnp.pad(va.reshape(d, A_HEADS, A_HEAD_DIM), ((0, 0), (0, 0), (0, LANES - A_HEAD_DIM)))
    cols = [qa, ka,
            qi, jnp.tile(ki, (1, IDX_HEADS)), jnp.pad(wi, ((0, 0), (0, LANES - IDX_HEADS))),
            qb, kb, vb, og, jnp.pad(gd, ((0, 0), (0, LANES - GATE_RANK)))]
    return jnp.concatenate(cols, axis=1).astype(BF16), va_aug.reshape(d, VT_ROWS).T.astype(BF16)


def _in_proj(x2d, g, w_packed, wv_t):
    t = x2d.shape[0]
    return pl.pallas_call(
        _in_proj_kernel,
        grid=(t // TM_PROJ,),
        in_specs=[pl.BlockSpec((TM_PROJ, D_MODEL), lambda r: (r, 0)),
                  _resident((1, D_MODEL)),
                  _resident((D_MODEL, W_COLS)),
                  _resident((VT_ROWS, D_MODEL))],
        out_specs=[pl.BlockSpec((TM_PROJ, A_COLS), lambda r: (r, 0)),
                   pl.BlockSpec((VT_ROWS, TM_PROJ), lambda r: (0, r)),
                   pl.BlockSpec((TM_PROJ, IDX_COLS), lambda r: (r, 0)),
                   pl.BlockSpec((TM_PROJ, B_COLS), lambda r: (r, 0))],
        out_shape=[jax.ShapeDtypeStruct((t, A_COLS), BF16),
                   jax.ShapeDtypeStruct((VT_ROWS, t), BF16),
                   jax.ShapeDtypeStruct((t, IDX_COLS), F32),
                   jax.ShapeDtypeStruct((t, B_COLS), F32)],
        compiler_params=pltpu.CompilerParams(dimension_semantics=("parallel",),
                                             vmem_limit_bytes=VMEM_LIMIT),
        name="in_proj",
    )(x2d, g, w_packed, wv_t)


def _rows8(x, op):
    r, c = x.shape
    if r % 32 == 0 and r > 32:
        return op(op(x.reshape(r // 32, 32, c), axis=0).reshape(4, 8, c), axis=0)
    return op(x.reshape(r // 8, 8, c), axis=0)


def _colmax(x):
    return jnp.max(_rows8(x, jnp.max), axis=0, keepdims=True)


N_BISECT = 20


def _dsa_kernel(q_ref, k_ref, vt_ref, iq_ref, iw_ref, ik_ref, bias_ref, o_ref,
                qi_ref, qa_ref, s_ref, mb_ref, m_ref, acc_ref, lg_ref, mo_ref, mn_ref):
    i = pl.program_id(1)
    nk = i + 1
    npair = (nk + 1) // 2
    big = jnp.float32(3e38)
    kk = jnp.float32(TOPK_MAX)

    def blk(kc):
        start = kc * TQ
        return pl.ds(start if isinstance(kc, int) else pl.multiple_of(start, TQ), TQ)

    def blk2(j):
        return pl.ds(pl.multiple_of(j * 2 * TQ, 2 * TQ), 2 * TQ)

    row = lax.broadcasted_iota(jnp.int32, (TQ, TQ), 0)
    col = lax.broadcasted_iota(jnp.int32, (TQ, TQ), 1)
    adm_diag = row < (col // CHUNK + 1) * CHUNK

    lane_i = lax.broadcasted_iota(jnp.int32, (1, IDX_WIDTH), 1)
    iq = iq_ref[...]
    for h in range(IDX_HEADS):
        qi_ref[h] = jnp.where(lane_i // IDX_DIM == h, iq, 0.0).astype(BF16)
    lane_p = lax.broadcasted_iota(jnp.int32, (1, LANES), 1)
    for h in range(A_HEADS):
        q_pair = q_ref[:, (h // 2) * LANES:(h // 2 + 1) * LANES]
        qa_ref[h] = jnp.where((lane_p < A_HEAD_DIM) == (h % 2 == 0), q_pair, jnp.zeros_like(q_pair))
    w_t = iw_ref[...].T * (IDX_DIM ** -0.5 * IDX_HEADS ** -0.5)

    def score_blk(kc, c):
        kic = ik_ref[blk(kc), :].astype(BF16)
        acc = jnp.zeros((TQ, TQ), F32)
        for h in range(IDX_HEADS):
            acc = acc + jnp.maximum(_dot_nt(kic, qi_ref[h]), 0.0) * w_t[h:h + 1, :]
        adm = jnp.logical_or(kc < i, adm_diag)
        s = jnp.where(adm, acc, NEG)
        s_ref[blk(kc), :] = s
        return (jnp.minimum(c[0], _rows8(jnp.where(adm, acc, big), jnp.min)),
                jnp.maximum(c[1], _rows8(s, jnp.max)))

    lo8, hi8 = lax.fori_loop(0, nk, score_blk, (jnp.full((8, TQ), big), jnp.full((8, TQ), -big)))

    @pl.when(i == 0)
    def _():
        mb_ref[0:TQ, :] = jnp.where(adm_diag, 0.0, NEG)

    @pl.when(i > 0)
    def _():
        @pl.when(nk % 2 == 1)
        def _():
            s_ref[blk(nk), :] = jnp.full((TQ, TQ), NEG, F32)

        lo = jnp.min(lo8, axis=0, keepdims=True)
        hi = jnp.max(hi8, axis=0, keepdims=True)
        q_col = lax.broadcasted_iota(jnp.int32, (1, TQ), 1)
        c_lo = (i * TQ + (q_col // CHUNK + 1) * CHUNK).astype(F32)

        def count_ge(t):
            def body(j, acc):
                return acc + _rows8(jnp.where(s_ref[blk2(j), :] >= t, 1.0, 0.0), jnp.sum)
            return jnp.sum(lax.fori_loop(0, npair, body, jnp.zeros((8, TQ), F32)), axis=0, keepdims=True)

        def bisect(_, c):
            lo, hi, c_lo = c
            mid = 0.5 * lo + 0.5 * hi
            cnt = count_ge(mid)
            ge = cnt >= kk
            return jnp.where(ge, mid, lo), jnp.where(ge, hi, mid), jnp.where(ge, cnt, c_lo)
        lo, hi, c_lo = lax.fori_loop(0, N_BISECT, bisect, (lo, hi, c_lo))
        unresolved = jnp.max(c_lo)

        @pl.when(unresolved <= kk)
        def _():
            def select(kc, carry):
                mb_ref[blk(kc), :] = jnp.where(s_ref[blk(kc), :] >= lo, 0.0, NEG)
                return carry
            lax.fori_loop(0, nk, select, 0)

        @pl.when(unresolved > kk)
        def _():
            def min_ge(j, acc):
                s = s_ref[blk2(j), :]
                return jnp.minimum(acc, _rows8(jnp.where(s >= lo, s, big), jnp.min))
            v0 = jnp.min(lax.fori_loop(0, npair, min_ge, jnp.full((8, TQ), big)), axis=0, keepdims=True)

            def above(v):
                def body(j, c):
                    s = s_ref[blk2(j), :]
                    gt = s > v
                    return (c[0] + _rows8(jnp.where(gt, 1.0, 0.0), jnp.sum),
                            jnp.minimum(c[1], _rows8(jnp.where(gt, s, big), jnp.min)))
                c8, n8 = lax.fori_loop(0, npair, body, (jnp.zeros((8, TQ), F32), jnp.full((8, TQ), big)))
                return jnp.sum(c8, axis=0, keepdims=True), jnp.min(n8, axis=0, keepdims=True)

            def peel_cond(c):
                return c[2] > 0.5

            def peel(c):
                v, _, _ = c
                c_gt, nxt = above(v)
                done = c_gt < kk
                v_new = jnp.where(done, v, nxt)
                pending = jnp.max(jnp.where(done, 0.0, 1.0))
                return v_new, c_gt, pending
            thr, c_gt, _ = lax.while_loop(peel_cond, peel, (v0, jnp.zeros((1, TQ), F32), jnp.float32(1.0)))
            need = kk - c_gt

            ltri = (row >= col).astype(BF16)

            def select(kc, carry):
                s = s_ref[blk(kc), :]
                tie = s == thr
                pref = _dot(ltri, jnp.where(tie, 1.0, 0.0).astype(BF16))
                sel = jnp.logical_or(s > thr, jnp.logical_and(tie, carry + pref <= need))
                mb_ref[blk(kc), :] = jnp.where(sel, 0.0, NEG)
                return carry + pref[TQ - 1:TQ, :]
            lax.fori_loop(0, nk, select, jnp.zeros((1, TQ), F32))

    m_ref[...] = jnp.full(m_ref.shape, -big, F32)
    acc_ref[...] = jnp.zeros(acc_ref.shape, F32)

    def qk_stage(kc, slot, near):
        mb = mb_ref[blk(kc), :]
        for h in range(A_HEADS):
            cs = slice((h // 2) * LANES, (h // 2 + 1) * LANES)
            l = _dot_nt(k_ref[blk(kc), cs], qa_ref[h]) + mb
            if near:
                l = l + bias_ref[kc - i + 2, h]
            lg_ref[slot, h] = l
            m_old = m_ref[h, 0:1, :]
            m_new = jnp.maximum(m_old, _colmax(l))
            mo_ref[slot, h, 0:1, :] = m_old
            mn_ref[slot, h, 0:1, :] = m_new
            m_ref[h, 0:1, :] = m_new

    def pv_stage(kc, slot):
        for h in range(A_HEADS):
            m_new = mn_ref[slot, h, 0:1, :]
            p = jnp.exp2(lg_ref[slot, h] - m_new).astype(BF16)
            acc_ref[h] = (acc_ref[h] * jnp.exp2(mo_ref[slot, h, 0:1, :] - m_new)
                          + _dot(vt_ref[h * LANES:(h + 1) * LANES, blk(kc)], p))

    n_far = jnp.maximum(i - 1, 0)

    @pl.when(n_far > 0)
    def _():
        qk_stage(0, 0, False)
        n_body = (n_far - 1) // 2

        def body(j, carry):
            qk_stage(2 * j + 1, 1, False)
            pv_stage(2 * j, 0)
            qk_stage(2 * j + 2, 0, False)
            pv_stage(2 * j + 1, 1)
            return carry
        lax.fori_loop(0, n_body, body, 0)
        last = 2 * n_body

        @pl.when(n_far - last == 2)
        def _():
            qk_stage(last + 1, 1, False)
        pv_stage(last, 0)

        @pl.when(n_far - last == 2)
        def _():
            pv_stage(last + 1, 1)

    @pl.when(i > 0)
    def _():
        qk_stage(i - 1, 0, True)
        qk_stage(i, 1, True)
        pv_stage(i - 1, 0)
        pv_stage(i, 1)

    @pl.when(i == 0)
    def _():
        qk_stage(0, 0, True)
        pv_stage(0, 0)

    outs = [acc_ref[h, 0:A_HEAD_DIM, :] / acc_ref[h, A_HEAD_DIM:A_HEAD_DIM + 1, :] for h in range(A_HEADS)]
    o_ref[...] = jnp.concatenate(outs, axis=0).T.astype(o_ref.dtype)


def _dsa(a_out, vt_out, idx_out, bias_tab, bsz, seq):
    nq = seq // TQ
    return pl.pallas_call(
        _dsa_kernel,
        grid=(bsz, nq),
        in_specs=[pl.BlockSpec((TQ, A_WIDTH), lambda b, i: (b * nq + i, 0)),
                  pl.BlockSpec((seq, A_WIDTH), lambda b, i: (b, 1)),
                  pl.BlockSpec((VT_ROWS, seq), lambda b, i: (0, b)),
                  pl.BlockSpec((TQ, IDX_WIDTH), lambda b, i: (b * nq + i, 0)),
                  pl.BlockSpec((TQ, LANES), lambda b, i: (b * nq + i, 2 * IDX_WIDTH // LANES)),
                  pl.BlockSpec((seq, IDX_WIDTH), lambda b, i: (b, 1)),
                  _resident((3, A_HEADS, TQ, TQ))],
        out_specs=pl.BlockSpec((TQ, A_WIDTH), lambda b, i: (b * nq + i, 0)),
        out_shape=jax.ShapeDtypeStruct((bsz * seq, A_WIDTH), BF16),
        scratch_shapes=[pltpu.VMEM((IDX_HEADS, TQ, IDX_WIDTH), BF16),
                        pltpu.VMEM((A_HEADS, TQ, LANES), BF16),
                        pltpu.VMEM((seq, TQ), F32),
                        pltpu.VMEM((seq, TQ), F32),
                        pltpu.VMEM((A_HEADS, 8, TQ), F32),
                        pltpu.VMEM((A_HEADS, LANES, TQ), F32),
                        pltpu.VMEM((2, A_HEADS, TQ, TQ), F32),
                        pltpu.VMEM((2, A_HEADS, 8, TQ), F32),
                        pltpu.VMEM((2, A_HEADS, 8, TQ), F32)],
        compiler_params=pltpu.CompilerParams(dimension_semantics=("parallel", "arbitrary"),
                                             vmem_limit_bytes=VMEM_LIMIT),
        name="dsa",
    )(a_out, a_out, vt_out, idx_out, idx_out, idx_out, bias_tab)


def _gla_kernel(qk_ref, v_ref, og_ref, gd_ref, wgu_ref, bg_ref, gn_ref, mcum_ref, mtot_ref,
                o_ref, st_ref, ob_ref):
    @pl.when(pl.program_id(1) == 0)
    def _():
        st_ref[...] = jnp.zeros_like(st_ref)

    z = _dot(gd_ref[...].astype(BF16), wgu_ref[...]) + bg_ref[...]
    g = -(jnp.maximum(-z, 0.0) + jnp.log1p(jnp.exp(-jnp.abs(z)))) * (1.0 / GATE_NORMALIZER)
    g_hi = g.astype(BF16)
    g_lo = (g - g_hi.astype(F32)).astype(BF16)
    mcum = mcum_ref[...]
    mtot = mtot_ref[...]
    bcum = _dot(mcum, g_hi) + _dot(mcum, g_lo)
    btot = _dot(mtot, g_hi) + _dot(mtot, g_lo)

    q = qk_ref[:, 0:B_KEY_WIDTH]
    k = qk_ref[:, B_KEY_WIDTH:2 * B_KEY_WIDTH]
    q_in = q * B_KEY_DIM ** -0.5 * jnp.exp(bcum)
    k_in = (k * jnp.exp(-bcum)).astype(BF16)
    k_dec = k * jnp.exp(btot - bcum)
    e_tot = jnp.exp(btot)

    lane = lax.broadcasted_iota(jnp.int32, (1, LANES), 1)
    halves = (lane < B_KEY_DIM, lane >= B_KEY_DIM)
    r_i = lax.broadcasted_iota(jnp.int32, (CHUNK, CHUNK), 0)
    c_i = lax.broadcasted_iota(jnp.int32, (CHUNK, CHUNK), 1)
    causal = r_i >= c_i

    for pair in range(B_HEADS // 2):
        cs = slice(pair * LANES, (pair + 1) * LANES)
        qp, kd, ki = q_in[:, cs], k_dec[:, cs], k_in[:, cs]
        qm = [jnp.where(hm, qp, 0.0).astype(BF16) for hm in halves]
        kdm = [jnp.where(hm, kd, 0.0).astype(BF16) for hm in halves]
        state = st_ref[pair]
        for c in range(TS_GLA // CHUNK):
            rs = slice(c * CHUNK, (c + 1) * CHUNK)
            sb = state.astype(BF16)
            upd = jnp.zeros_like(state)
            for sub in range(2):
                h = 2 * pair + sub
                vh = v_ref[rs, h * B_VAL_DIM:(h + 1) * B_VAL_DIM].astype(BF16)
                att = jnp.where(causal, _dot_nt(qm[sub][rs], ki[rs]), 0.0).astype(BF16)
                ob_ref[rs, h * B_VAL_DIM:(h + 1) * B_VAL_DIM] = _dot(att, vh) + _dot_nt(qm[sub][rs], sb)
                upd = upd + _dot_tn(vh, kdm[sub][rs])
            state = state * e_tot[c * CHUNK:c * CHUNK + 1, cs] + upd
        st_ref[pair] = state

    og = og_ref[...]
    gate = og * jax.nn.sigmoid(og)
    gn = gn_ref[...]
    for h in range(B_HEADS):
        cs = slice(h * B_VAL_DIM, (h + 1) * B_VAL_DIM)
        o_ref[:, cs] = (_rms(ob_ref[:, cs], gn[:, cs]) * gate[:, cs]).astype(o_ref.dtype)


def _gla(b_out, w_gate_up, b_gate, gla_norm, bsz, seq):
    ns = seq // TS_GLA
    idx = np.arange(TS_GLA)
    same = (idx[:, None] // CHUNK) == (idx[None, :] // CHUNK)
    mcum = jnp.asarray(same & (idx[None, :] <= idx[:, None]), BF16)
    mtot = jnp.asarray(same, BF16)
    wgu = jnp.pad(w_gate_up, ((0, LANES - GATE_RANK), (0, 0))).astype(BF16)
    return pl.pallas_call(
        _gla_kernel,
        grid=(bsz, ns),
        in_specs=[pl.BlockSpec((TS_GLA, 2 * B_KEY_WIDTH), lambda b, j: (b * ns + j, 0)),
                  pl.BlockSpec((TS_GLA, B_WIDTH), lambda b, j: (b * ns + j, 1)),
                  pl.BlockSpec((TS_GLA, B_WIDTH), lambda b, j: (b * ns + j, 2)),
                  pl.BlockSpec((TS_GLA, LANES), lambda b, j: (b * ns + j, (2 * B_KEY_WIDTH + 2 * B_WIDTH) // LANES)),
                  _resident((LANES, B_KEY_WIDTH)),
                  _resident((1, B_KEY_WIDTH)),
                  _resident((1, B_WIDTH)),
                  _resident((TS_GLA, TS_GLA)),
                  _resident((TS_GLA, TS_GLA))],
        out_specs=pl.BlockSpec((TS_GLA, B_WIDTH), lambda b, j: (b * ns + j, 0)),
        out_shape=jax.ShapeDtypeStruct((bsz * seq, B_WIDTH), BF16),
        scratch_shapes=[pltpu.VMEM((B_HEADS // 2, B_VAL_DIM, LANES), F32),
                        pltpu.VMEM((TS_GLA, B_WIDTH), F32)],
        compiler_params=pltpu.CompilerParams(dimension_semantics=("parallel", "arbitrary"),
                                             vmem_limit_bytes=VMEM_LIMIT),
        name="gla",
    )(b_out, b_out, b_out, b_out, wgu, b_gate.reshape(1, -1), gla_norm.reshape(1, -1), mcum, mtot)


def _out_ffn_kernel(x_ref, oa_ref, ob_ref, wo_ref, nf_ref, wg_ref, wu_ref, wd_ref, nl_ref, o_ref):
    mix = jnp.concatenate([oa_ref[...], ob_ref[...]], axis=1)
    x1 = x_ref[...] + _dot(mix, wo_ref[...])
    h = _rms(x1, nf_ref[...]).astype(BF16)
    g = _dot(h, wg_ref[...])
    u = _dot(h, wu_ref[...])
    a = (g * jax.nn.sigmoid(g) * u).astype(BF16)
    o_ref[...] = _rms(x1 + _dot(a, wd_ref[...]), nl_ref[...])


def _out_ffn(x2d, o_a, o_b, w_out, norm_ffn, w_gate, w_up, w_down, norm_final):
    t = x2d.shape[0]
    row = lambda r: (r, 0)
    return pl.pallas_call(
        _out_ffn_kernel,
        grid=(t // TM_FFN,),
        in_specs=[pl.BlockSpec((TM_FFN, D_MODEL), row),
                  pl.BlockSpec((TM_FFN, A_WIDTH), row),
                  pl.BlockSpec((TM_FFN, B_WIDTH), row),
                  _resident((A_WIDTH + B_WIDTH, D_MODEL)),
                  _resident((1, D_MODEL)),
                  _resident((D_MODEL, D_FF)),
                  _resident((D_MODEL, D_FF)),
                  _resident((D_FF, D_MODEL)),
                  _resident((1, D_MODEL))],
        out_specs=pl.BlockSpec((TM_FFN, D_MODEL), row),
        out_shape=jax.ShapeDtypeStruct((t, D_MODEL), F32),
        compiler_params=pltpu.CompilerParams(dimension_semantics=("parallel",),
                                             vmem_limit_bytes=VMEM_LIMIT),
        name="out_ffn",
    )(x2d, o_a, o_b, w_out, norm_ffn, w_gate, w_up, w_down, norm_final)


def kernel(x, norm_mix, w_in, w_gate_up, b_gate, gla_norm, w_out, rel_bias,
           norm_ffn, w_ffn_gate, w_ffn_up, w_ffn_down, norm_final):
    bsz, seq, d = x.shape
    assert w_in.shape[0] == 1, "fused final norm assumes a single layer"
    assert d == D_MODEL and seq % TS_GLA == 0 and seq % TQ == 0 and seq // 4 >= TOPK_MAX
    bias_tab = _bias_tables(rel_bias)
    x2d = x.reshape(bsz * seq, d)
    a_out, vt_out, idx_out, b_out = _in_proj(x2d, norm_mix[0].reshape(1, d), *_pack_w_in(w_in[0]))
    o_a = _dsa(a_out, vt_out, idx_out, bias_tab, bsz, seq)
    o_b = _gla(b_out, w_gate_up[0], b_gate[0], gla_norm[0], bsz, seq)
    out = _out_ffn(x2d, o_a, o_b, w_out[0].astype(BF16), norm_ffn[0].reshape(1, d),
                   w_ffn_gate[0].astype(BF16), w_ffn_up[0].astype(BF16),
                   w_ffn_down[0].astype(BF16), norm_final.reshape(1, d))
    return out.reshape(bsz, seq, d)
```

```python
import math

import numpy as np
import jax
import jax.numpy as jnp
from jax import lax
from jax.experimental import pallas as pl
from jax.experimental.pallas import tpu as pltpu

F32 = jnp.float32
BF16 = jnp.bfloat16

D_MODEL = 1024
CHUNK = 64
A_HEADS = 8
A_HEAD_DIM = 64
A_WIDTH = A_HEADS * A_HEAD_DIM
IDX_HEADS = 8
IDX_DIM = 32
IDX_WIDTH = IDX_HEADS * IDX_DIM
TOPK_MAX = 256
B_HEADS = 4
B_KEY_DIM = 64
B_VAL_DIM = 128
B_WIDTH = B_HEADS * B_VAL_DIM
B_KEY_WIDTH = B_HEADS * B_KEY_DIM
GATE_RANK = 16
GATE_NORMALIZER = 16.0
N_BUCKETS = 32
MAX_DISTANCE = 128
D_FF = 2816
EPS = 1e-6
NEG = -1e30
LOG2E = math.log2(math.e)

LANES = 128
VMEM_LIMIT = 56 * 1024 * 1024

TM_PROJ = 512
TQ = 256
TS_GLA = 512
TM_FFN = 512

A_COLS = 2 * A_WIDTH
VT_HEAD = A_HEAD_DIM + 16
VT_ROWS = A_HEADS * VT_HEAD
IDX_COLS = IDX_WIDTH + IDX_WIDTH + LANES
B_COLS = 2 * B_KEY_WIDTH + B_WIDTH + B_WIDTH + LANES
W_COLS = A_COLS + IDX_COLS + B_COLS

NT_DIMS = (((1,), (1,)), ((), ()))
TN_DIMS = (((0,), (0,)), ((), ()))


def _dot(a, b):
    return jnp.dot(a, b, preferred_element_type=F32)


def _dot_nt(a, b):
    return lax.dot_general(a, b, NT_DIMS, preferred_element_type=F32)


def _dot_tn(a, b):
    return lax.dot_general(a, b, TN_DIMS, preferred_element_type=F32)


def _rms(x, g):
    return x * lax.rsqrt(jnp.mean(x * x, axis=-1, keepdims=True) + EPS) * g


def _resident(shape):
    return pl.BlockSpec(shape, lambda *_: (0,) * len(shape), pipeline_mode=pl.Buffered(1))


def _t5_bucket_np(rel):
    half = N_BUCKETS // 2
    max_exact = half // 2
    ret = np.where(rel > 0, half, 0)
    n = np.abs(rel)
    nf = np.maximum(n, 1).astype(np.float32)
    large = max_exact + (np.log(nf / np.float32(max_exact))
                         / np.float32(math.log(MAX_DISTANCE / max_exact))
                         * np.float32(half - max_exact)).astype(np.int32)
    large = np.minimum(large, half - 1)
    return (ret + np.where(n < max_exact, n, large)).astype(np.int32)


FAR_BUCKET = int(_t5_bucket_np(np.array([-2 * TQ]))[0])


def _bias_kernel(rb_ref, bkt_ref, o_ref):
    t = pl.program_id(0)
    h = pl.program_id(1)
    bk = bkt_ref[0]
    acc = jnp.zeros(bk.shape, F32)
    for b in range(N_BUCKETS):
        acc = jnp.where(bk == b, rb_ref[b, h], acc)
    acc = (acc - rb_ref[FAR_BUCKET, h]) * LOG2E
    o_ref[0, 0] = jnp.where(t == 0, 0.0, acc)


def _bias_tables(rel_bias):
    kl = np.arange(TQ)[:, None]
    ql = np.arange(TQ)[None, :]
    buckets = np.stack([_t5_bucket_np(kl - ql - TQ), _t5_bucket_np(kl - ql)])
    return pl.pallas_call(
        _bias_kernel,
        grid=(3, A_HEADS),
        in_specs=[pl.BlockSpec(memory_space=pltpu.SMEM),
                  pl.BlockSpec((1, TQ, TQ), lambda t, h: (jnp.maximum(t - 1, 0), 0, 0))],
        out_specs=pl.BlockSpec((1, 1, TQ, TQ), lambda t, h: (t, h, 0, 0)),
        out_shape=jax.ShapeDtypeStruct((3, A_HEADS, TQ, TQ), F32),
        name="bias_tables",
    )(rel_bias, jnp.asarray(buckets))


def _in_proj_kernel(x_ref, g_ref, w_ref, wv_ref, a_ref, vt_ref, i_ref, b_ref):
    h = _rms(x_ref[...], g_ref[...]).astype(BF16)
    c0 = 0
    a_ref[:, 0:A_WIDTH] = (_dot(h, w_ref[:, c0:c0 + A_WIDTH]) * (A_HEAD_DIM ** -0.5 * LOG2E)).astype(BF16)
    c0 += A_WIDTH
    a_ref[:, A_WIDTH:2 * A_WIDTH] = _dot(h, w_ref[:, c0:c0 + A_WIDTH]).astype(BF16)
    c0 += A_WIDTH
    vt = _dot_nt(wv_ref[...], h).astype(BF16)
    ones = jnp.ones((VT_HEAD - A_HEAD_DIM, vt.shape[1]), BF16)
    for hd in range(A_HEADS):
        vt_ref[hd * VT_HEAD:hd * VT_HEAD + A_HEAD_DIM, :] = vt[hd * A_HEAD_DIM:(hd + 1) * A_HEAD_DIM, :]
        vt_ref[hd * VT_HEAD + A_HEAD_DIM:(hd + 1) * VT_HEAD, :] = ones
    i_ref[...] = _dot(h, w_ref[:, c0:c0 + IDX_COLS])
    c0 += IDX_COLS
    b_ref[...] = _dot(h, w_ref[:, c0:c0 + B_COLS])


def _pack_w_in(w_in):
    sizes = (A_WIDTH, A_WIDTH, A_WIDTH, IDX_WIDTH, IDX_DIM, IDX_HEADS,
             B_KEY_WIDTH, B_KEY_WIDTH, B_WIDTH, GATE_RANK, B_WIDTH)
    offs = np.cumsum((0,) + sizes)
    qa, ka, va, qi, ki, wi, qb, kb, vb, gd, og = [w_in[:, offs[j]:offs[j + 1]] for j in range(len(sizes))]
    d = w_in.shape[0]
    cols = [qa, ka,
            qi, jnp.tile(ki, (1, IDX_HEADS)), jnp.pad(wi, ((0, 0), (0, LANES - IDX_HEADS))),
            qb, kb, vb, og, jnp.pad(gd, ((0, 0), (0, LANES - GATE_RANK)))]
    return jnp.concatenate(cols, axis=1).astype(BF16), va.T.astype(BF16)


def _in_proj(x2d, g, w_packed, wv_t):
    t = x2d.shape[0]
    return pl.pallas_call(
        _in_proj_kernel,
        grid=(t // TM_PROJ,),
        in_specs=[pl.BlockSpec((TM_PROJ, D_MODEL), lambda r: (r, 0)),
                  _resident((1, D_MODEL)),
                  _resident((D_MODEL, W_COLS)),
                  _resident((A_WIDTH, D_MODEL))],
        out_specs=[pl.BlockSpec((TM_PROJ, A_COLS), lambda r: (r, 0)),
                   pl.BlockSpec((VT_ROWS, TM_PROJ), lambda r: (0, r)),
                   pl.BlockSpec((TM_PROJ, IDX_COLS), lambda r: (r, 0)),
                   pl.BlockSpec((TM_PROJ, B_COLS), lambda r: (r, 0))],
        out_shape=[jax.ShapeDtypeStruct((t, A_COLS), BF16),
                   jax.ShapeDtypeStruct((VT_ROWS, t), BF16),
                   jax.ShapeDtypeStruct((t, IDX_COLS), F32),
                   jax.ShapeDtypeStruct((t, B_COLS), F32)],
        compiler_params=pltpu.CompilerParams(dimension_semantics=("parallel",),
                                             vmem_limit_bytes=VMEM_LIMIT),
        name="in_proj",
    )(x2d, g, w_packed, wv_t)


def _rows8(x, op):
    r, c = x.shape
    if r % 32 == 0 and r > 32:
        return op(op(x.reshape(r // 32, 32, c), axis=0).reshape(4, 8, c), axis=0)
    return op(x.reshape(r // 8, 8, c), axis=0)


def _colmax(x):
    return jnp.max(_rows8(x, jnp.max), axis=0, keepdims=True)


N_BISECT = 20


def _dsa_kernel(q_ref, k_ref, vt_ref, iq_ref, iw_ref, ik_ref, bias_ref, o_ref,
                qi_ref, qa_ref, s_ref, mb_ref, m_ref, acc_ref, lg_ref, mo_ref, mn_ref):
    i = pl.program_id(1)
    nk = i + 1
    npair = (nk + 1) // 2
    big = jnp.float32(3e38)
    kk = jnp.float32(TOPK_MAX)

    def blk(kc):
        start = kc * TQ
        return pl.ds(start if isinstance(kc, int) else pl.multiple_of(start, TQ), TQ)

    def blk2(j):
        return pl.ds(pl.multiple_of(j * 2 * TQ, 2 * TQ), 2 * TQ)

    row = lax.broadcasted_iota(jnp.int32, (TQ, TQ), 0)
    col = lax.broadcasted_iota(jnp.int32, (TQ, TQ), 1)
    adm_diag = row < (col // CHUNK + 1) * CHUNK

    lane_i = lax.broadcasted_iota(jnp.int32, (1, IDX_WIDTH), 1)
    iq = iq_ref[...]
    for h in range(IDX_HEADS):
        qi_ref[h] = jnp.where(lane_i // IDX_DIM == h, iq, 0.0).astype(BF16)
    lane_p = lax.broadcasted_iota(jnp.int32, (1, LANES), 1)
    for h in range(A_HEADS):
        q_pair = q_ref[:, (h // 2) * LANES:(h // 2 + 1) * LANES]
        qa_ref[h] = jnp.where((lane_p < A_HEAD_DIM) == (h % 2 == 0), q_pair, jnp.zeros_like(q_pair))
    w_t = iw_ref[...].T * (IDX_DIM ** -0.5 * IDX_HEADS ** -0.5)

    def score_blk(kc, c):
        kic = ik_ref[blk(kc), :].astype(BF16)
        acc = jnp.zeros((TQ, TQ), F32)
        for h in range(IDX_HEADS):
            acc = acc + jnp.maximum(_dot_nt(kic, qi_ref[h]), 0.0) * w_t[h:h + 1, :]
        adm = jnp.logical_or(kc < i, adm_diag)
        s = jnp.where(adm, acc, NEG)
        s_ref[blk(kc), :] = s
        return (jnp.minimum(c[0], _rows8(jnp.where(adm, acc, big), jnp.min)),
                jnp.maximum(c[1], _rows8(s, jnp.max)))

    def score_two(j, c):
        return score_blk(2 * j + 1, score_blk(2 * j, c))

    c0 = lax.fori_loop(0, nk // 2, score_two, (jnp.full((8, TQ), big), jnp.full((8, TQ), -big)))
    lo8, hi8 = lax.cond(nk % 2 == 1, lambda c: score_blk(nk - 1, c), lambda c: c, c0)

    @pl.when(i == 0)
    def _():
        mb_ref[0:TQ, :] = jnp.where(adm_diag, 0.0, NEG)

    @pl.when(i > 0)
    def _():
        @pl.when(nk % 2 == 1)
        def _():
            s_ref[blk(nk), :] = jnp.full((TQ, TQ), NEG, F32)

        lo = jnp.min(lo8, axis=0, keepdims=True)
        hi = jnp.max(hi8, axis=0, keepdims=True)
        q_col = lax.broadcasted_iota(jnp.int32, (1, TQ), 1)
        c_lo = (i * TQ + (q_col // CHUNK + 1) * CHUNK).astype(F32)

        def count_ge(t):
            def body(j, acc):
                return acc + _rows8(jnp.where(s_ref[blk2(j), :] >= t, 1.0, 0.0), jnp.sum)
            return jnp.sum(lax.fori_loop(0, npair, body, jnp.zeros((8, TQ), F32)), axis=0, keepdims=True)

        def bisect(_, c):
            lo, hi, c_lo = c
            mid = 0.5 * lo + 0.5 * hi
            cnt = count_ge(mid)
            ge = cnt >= kk
            return jnp.where(ge, mid, lo), jnp.where(ge, hi, mid), jnp.where(ge, cnt, c_lo)
        lo, hi, c_lo = lax.fori_loop(0, N_BISECT, bisect, (lo, hi, c_lo))
        unresolved = jnp.max(c_lo)

        @pl.when(unresolved <= kk)
        def _():
            def select(kc, carry):
                mb_ref[blk(kc), :] = jnp.where(s_ref[blk(kc), :] >= lo, 0.0, NEG)
                return carry
            lax.fori_loop(0, nk, select, 0)

        @pl.when(unresolved > kk)
        def _():
            def min_ge(j, acc):
                s = s_ref[blk2(j), :]
                return jnp.minimum(acc, _rows8(jnp.where(s >= lo, s, big), jnp.min))
            v0 = jnp.min(lax.fori_loop(0, npair, min_ge, jnp.full((8, TQ), big)), axis=0, keepdims=True)

            def above(v):
                def body(j, c):
                    s = s_ref[blk2(j), :]
                    gt = s > v
                    return (c[0] + _rows8(jnp.where(gt, 1.0, 0.0), jnp.sum),
                            jnp.minimum(c[1], _rows8(jnp.where(gt, s, big), jnp.min)))
                c8, n8 = lax.fori_loop(0, npair, body, (jnp.zeros((8, TQ), F32), jnp.full((8, TQ), big)))
                return jnp.sum(c8, axis=0, keepdims=True), jnp.min(n8, axis=0, keepdims=True)

            def peel_cond(c):
                return c[2] > 0.5

            def peel(c):
                v, _, _ = c
                c_gt, nxt = above(v)
                done = c_gt < kk
                v_new = jnp.where(done, v, nxt)
                pending = jnp.max(jnp.where(done, 0.0, 1.0))
                return v_new, c_gt, pending
            thr, c_gt, _ = lax.while_loop(peel_cond, peel, (v0, jnp.zeros((1, TQ), F32), jnp.float32(1.0)))
            need = kk - c_gt

            ltri = (row >= col).astype(BF16)

            def select(kc, carry):
                s = s_ref[blk(kc), :]
                tie = s == thr
                pref = _dot(ltri, jnp.where(tie, 1.0, 0.0).astype(BF16))
                sel = jnp.logical_or(s > thr, jnp.logical_and(tie, carry + pref <= need))
                mb_ref[blk(kc), :] = jnp.where(sel, 0.0, NEG)
                return carry + pref[TQ - 1:TQ, :]
            lax.fori_loop(0, nk, select, jnp.zeros((1, TQ), F32))

    m_ref[...] = jnp.full(m_ref.shape, -big, F32)
    acc_ref[...] = jnp.zeros(acc_ref.shape, F32)

    def qk_stage(kc, slot, near):
        mb = mb_ref[blk(kc), :]
        for h in range(A_HEADS):
            cs = slice((h // 2) * LANES, (h // 2 + 1) * LANES)
            l = _dot_nt(k_ref[blk(kc), cs], qa_ref[h]) + mb
            if near:
                l = l + bias_ref[kc - i + 2, h]
            lg_ref[slot, h] = l
            m_old = m_ref[h, 0:1, :]
            m_new = jnp.maximum(m_old, _colmax(l))
            mo_ref[slot, h, 0:1, :] = m_old
            mn_ref[slot, h, 0:1, :] = m_new
            m_ref[h, 0:1, :] = m_new

    def pv_stage(kc, slot):
        for h in range(A_HEADS):
            m_new = mn_ref[slot, h, 0:1, :]
            p = jnp.exp2(lg_ref[slot, h] - m_new).astype(BF16)
            acc_ref[h] = (acc_ref[h] * jnp.exp2(mo_ref[slot, h, 0:1, :] - m_new)
                          + _dot(vt_ref[h * VT_HEAD:(h + 1) * VT_HEAD, blk(kc)], p))

    n_far = jnp.maximum(i - 1, 0)

    @pl.when(n_far > 0)
    def _():
        qk_stage(0, 0, False)
        n_body = (n_far - 1) // 2

        def body(j, carry):
            qk_stage(2 * j + 1, 1, False)
            pv_stage(2 * j, 0)
            qk_stage(2 * j + 2, 0, False)
            pv_stage(2 * j + 1, 1)
            return carry
        lax.fori_loop(0, n_body, body, 0)
        last = 2 * n_body

        @pl.when(n_far - last == 2)
        def _():
            qk_stage(last + 1, 1, False)
        pv_stage(last, 0)

        @pl.when(n_far - last == 2)
        def _():
            pv_stage(last + 1, 1)

    @pl.when(i > 0)
    def _():
        qk_stage(i - 1, 0, True)
        qk_stage(i, 1, True)
        pv_stage(i - 1, 0)
        pv_stage(i, 1)

    @pl.when(i == 0)
    def _():
        qk_stage(0, 0, True)
        pv_stage(0, 0)

    outs = [acc_ref[h, 0:A_HEAD_DIM, :] / acc_ref[h, A_HEAD_DIM:A_HEAD_DIM + 1, :] for h in range(A_HEADS)]
    o_ref[...] = jnp.concatenate(outs, axis=0).T.astype(o_ref.dtype)


def _dsa(a_out, vt_out, idx_out, bias_tab, bsz, seq):
    nq = seq // TQ
    return pl.pallas_call(
        _dsa_kernel,
        grid=(bsz, nq),
        in_specs=[pl.BlockSpec((TQ, A_WIDTH), lambda b, i: (b * nq + i, 0)),
                  pl.BlockSpec((seq, A_WIDTH), lambda b, i: (b, 1)),
                  pl.BlockSpec((VT_ROWS, seq), lambda b, i: (0, b)),
                  pl.BlockSpec((TQ, IDX_WIDTH), lambda b, i: (b * nq + i, 0)),
                  pl.BlockSpec((TQ, LANES), lambda b, i: (b * nq + i, 2 * IDX_WIDTH // LANES)),
                  pl.BlockSpec((seq, IDX_WIDTH), lambda b, i: (b, 1)),
                  _resident((3, A_HEADS, TQ, TQ))],
        out_specs=pl.BlockSpec((TQ, A_WIDTH), lambda b, i: (b * nq + i, 0)),
        out_shape=jax.ShapeDtypeStruct((bsz * seq, A_WIDTH), BF16),
        scratch_shapes=[pltpu.VMEM((IDX_HEADS, TQ, IDX_WIDTH), BF16),
                        pltpu.VMEM((A_HEADS, TQ, LANES), BF16),
                        pltpu.VMEM((seq, TQ), F32),
                        pltpu.VMEM((seq, TQ), F32),
                        pltpu.VMEM((A_HEADS, 8, TQ), F32),
                        pltpu.VMEM((A_HEADS, VT_HEAD, TQ), F32),
                        pltpu.VMEM((2, A_HEADS, TQ, TQ), F32),
                        pltpu.VMEM((2, A_HEADS, 8, TQ), F32),
                        pltpu.VMEM((2, A_HEADS, 8, TQ), F32)],
        compiler_params=pltpu.CompilerParams(dimension_semantics=("parallel", "arbitrary"),
                                             vmem_limit_bytes=VMEM_LIMIT),
        name="dsa",
    )(a_out, a_out, vt_out, idx_out, idx_out, idx_out, bias_tab)


def _gla_kernel(qk_ref, v_ref, og_ref, gd_ref, wgu_ref, bg_ref, gn_ref, mcum_ref, mtot_ref,
                o_ref, st_ref, ob_ref):
    @pl.when(pl.program_id(1) == 0)
    def _():
        st_ref[...] = jnp.zeros_like(st_ref)

    z = _dot(gd_ref[...].astype(BF16), wgu_ref[...]) + bg_ref[...]
    g = -(jnp.maximum(-z, 0.0) + jnp.log1p(jnp.exp(-jnp.abs(z)))) * (1.0 / GATE_NORMALIZER)
    g_hi = g.astype(BF16)
    g_lo = (g - g_hi.astype(F32)).astype(BF16)
    mcum = mcum_ref[...]
    mtot = mtot_ref[...]
    bcum = _dot(mcum, g_hi) + _dot(mcum, g_lo)
    btot = _dot(mtot, g_hi) + _dot(mtot, g_lo)

    q = qk_ref[:, 0:B_KEY_WIDTH]
    k = qk_ref[:, B_KEY_WIDTH:2 * B_KEY_WIDTH]
    q_in = q * B_KEY_DIM ** -0.5 * jnp.exp(bcum)
    k_in = (k * jnp.exp(-bcum)).astype(BF16)
    k_dec = k * jnp.exp(btot - bcum)
    e_tot = jnp.exp(btot)

    lane = lax.broadcasted_iota(jnp.int32, (1, LANES), 1)
    halves = (lane < B_KEY_DIM, lane >= B_KEY_DIM)
    r_i = lax.broadcasted_iota(jnp.int32, (CHUNK, CHUNK), 0)
    c_i = lax.broadcasted_iota(jnp.int32, (CHUNK, CHUNK), 1)
    causal = r_i >= c_i

    n_chunks = TS_GLA // CHUNK
    rows = [slice(c * CHUNK, (c + 1) * CHUNK) for c in range(n_chunks)]
    qm, kdm, vh, att, upd = {}, {}, {}, {}, {}
    for pair in range(B_HEADS // 2):
        cs = slice(pair * LANES, (pair + 1) * LANES)
        for sub in range(2):
            h = 2 * pair + sub
            qm[h] = jnp.where(halves[sub], q_in[:, cs], 0.0).astype(BF16)
            kdm[h] = jnp.where(halves[sub], k_dec[:, cs], 0.0).astype(BF16)
            vh[h] = v_ref[:, h * B_VAL_DIM:(h + 1) * B_VAL_DIM].astype(BF16)
    for h in range(B_HEADS):
        ki = k_in[:, (h // 2) * LANES:(h // 2 + 1) * LANES]
        for c in range(n_chunks):
            att[h, c] = jnp.where(causal, _dot_nt(qm[h][rows[c]], ki[rows[c]]), 0.0).astype(BF16)
    for pair in range(B_HEADS // 2):
        for c in range(n_chunks):
            upd[pair, c] = (_dot_tn(vh[2 * pair][rows[c]], kdm[2 * pair][rows[c]])
                            + _dot_tn(vh[2 * pair + 1][rows[c]], kdm[2 * pair + 1][rows[c]]))
    for h in range(B_HEADS):
        for c in range(n_chunks):
            ob_ref[rows[c], h * B_VAL_DIM:(h + 1) * B_VAL_DIM] = _dot(att[h, c], vh[h][rows[c]])
    for pair in range(B_HEADS // 2):
        cs = slice(pair * LANES, (pair + 1) * LANES)
        state = st_ref[pair]
        for c in range(n_chunks):
            sb = state.astype(BF16)
            for h in (2 * pair, 2 * pair + 1):
                ob_ref[rows[c], h * B_VAL_DIM:(h + 1) * B_VAL_DIM] += _dot_nt(qm[h][rows[c]], sb)
            state = state * e_tot[c * CHUNK:c * CHUNK + 1, cs] + upd[pair, c]
        st_ref[pair] = state

    og = og_ref[...]
    gate = og * jax.nn.sigmoid(og)
    gn = gn_ref[...]
    for h in range(B_HEADS):
        cs = slice(h * B_VAL_DIM, (h + 1) * B_VAL_DIM)
        o_ref[:, cs] = (_rms(ob_ref[:, cs], gn[:, cs]) * gate[:, cs]).astype(o_ref.dtype)


def _gla(b_out, w_gate_up, b_gate, gla_norm, bsz, seq):
    ns = seq // TS_GLA
    idx = np.arange(TS_GLA)
    same = (idx[:, None] // CHUNK) == (idx[None, :] // CHUNK)
    mcum = jnp.asarray(same & (idx[None, :] <= idx[:, None]), BF16)
    mtot = jnp.asarray(same, BF16)
    wgu = jnp.pad(w_gate_up, ((0, LANES - GATE_RANK), (0, 0))).astype(BF16)
    return pl.pallas_call(
        _gla_kernel,
        grid=(bsz, ns),
        in_specs=[pl.BlockSpec((TS_GLA, 2 * B_KEY_WIDTH), lambda b, j: (b * ns + j, 0)),
                  pl.BlockSpec((TS_GLA, B_WIDTH), lambda b, j: (b * ns + j, 1)),
                  pl.BlockSpec((TS_GLA, B_WIDTH), lambda b, j: (b * ns + j, 2)),
                  pl.BlockSpec((TS_GLA, LANES), lambda b, j: (b * ns + j, (2 * B_KEY_WIDTH + 2 * B_WIDTH) // LANES)),
                  _resident((LANES, B_KEY_WIDTH)),
                  _resident((1, B_KEY_WIDTH)),
                  _resident((1, B_WIDTH)),
                  _resident((TS_GLA, TS_GLA)),
                  _resident((TS_GLA, TS_GLA))],
        out_specs=pl.BlockSpec((TS_GLA, B_WIDTH), lambda b, j: (b * ns + j, 0)),
        out_shape=jax.ShapeDtypeStruct((bsz * seq, B_WIDTH), BF16),
        scratch_shapes=[pltpu.VMEM((B_HEADS // 2, B_VAL_DIM, LANES), F32),
                        pltpu.VMEM((TS_GLA, B_WIDTH), F32)],
        compiler_params=pltpu.CompilerParams(dimension_semantics=("parallel", "arbitrary"),
                                             vmem_limit_bytes=VMEM_LIMIT),
        name="gla",
    )(b_out, b_out, b_out, b_out, wgu, b_gate.reshape(1, -1), gla_norm.reshape(1, -1), mcum, mtot)


def _out_ffn_kernel(x_ref, oa_ref, ob_ref, wo_ref, nf_ref, wg_ref, wu_ref, wd_ref, nl_ref, o_ref):
    mix = jnp.concatenate([oa_ref[...], ob_ref[...]], axis=1)
    x1 = x_ref[...] + _dot(mix, wo_ref[...])
    h = _rms(x1, nf_ref[...]).astype(BF16)
    g = _dot(h, wg_ref[...])
    u = _dot(h, wu_ref[...])
    a = (g * jax.nn.sigmoid(g) * u).astype(BF16)
    o_ref[...] = _rms(x1 + _dot(a, wd_ref[...]), nl_ref[...])


def _out_ffn(x2d, o_a, o_b, w_out, norm_ffn, w_gate, w_up, w_down, norm_final):
    t = x2d.shape[0]
    row = lambda r: (r, 0)
    return pl.pallas_call(
        _out_ffn_kernel,
        grid=(t // TM_FFN,),
        in_specs=[pl.BlockSpec((TM_FFN, D_MODEL), row),
                  pl.BlockSpec((TM_FFN, A_WIDTH), row),
                  pl.BlockSpec((TM_FFN, B_WIDTH), row),
                  _resident((A_WIDTH + B_WIDTH, D_MODEL)),
                  _resident((1, D_MODEL)),
                  _resident((D_MODEL, D_FF)),
                  _resident((D_MODEL, D_FF)),
                  _resident((D_FF, D_MODEL)),
                  _resident((1, D_MODEL))],
        out_specs=pl.BlockSpec((TM_FFN, D_MODEL), row),
        out_shape=jax.ShapeDtypeStruct((t, D_MODEL), F32),
        compiler_params=pltpu.CompilerParams(dimension_semantics=("parallel",),
                                             vmem_limit_bytes=VMEM_LIMIT),
        name="out_ffn",
    )(x2d, o_a, o_b, w_out, norm_ffn, w_gate, w_up, w_down, norm_final)


def kernel(x, norm_mix, w_in, w_gate_up, b_gate, gla_norm, w_out, rel_bias,
           norm_ffn, w_ffn_gate, w_ffn_up, w_ffn_down, norm_final):
    bsz, seq, d = x.shape
    assert w_in.shape[0] == 1, "fused final norm assumes a single layer"
    assert d == D_MODEL and seq % TS_GLA == 0 and seq % TQ == 0 and seq // 4 >= TOPK_MAX
    bias_tab = _bias_tables(rel_bias)
    x2d = x.reshape(bsz * seq, d)
    a_out, vt_out, idx_out, b_out = _in_proj(x2d, norm_mix[0].reshape(1, d), *_pack_w_in(w_in[0]))
    o_a = _dsa(a_out, vt_out, idx_out, bias_tab, bsz, seq)
    o_b = _gla(b_out, w_gate_up[0], b_gate[0], gla_norm[0], bsz, seq)
    out = _out_ffn(x2d, o_a, o_b, w_out[0].astype(BF16), norm_ffn[0].reshape(1, d),
                   w_ffn_gate[0].astype(BF16), w_ffn_up[0].astype(BF16),
                   w_ffn_down[0].astype(BF16), norm_final.reshape(1, d))
    return out.reshape(bsz, seq, d)
```

```python
import math

import numpy as np
import jax
import jax.numpy as jnp
from jax import lax
from jax.experimental import pallas as pl
from jax.experimental.pallas import tpu as pltpu

F32 = jnp.float32
BF16 = jnp.bfloat16

D_MODEL = 1024
CHUNK = 64
A_HEADS = 8
A_HEAD_DIM = 64
A_WIDTH = A_HEADS * A_HEAD_DIM
IDX_HEADS = 8
IDX_DIM = 32
IDX_WIDTH = IDX_HEADS * IDX_DIM
TOPK_MAX = 256
B_HEADS = 4
B_KEY_DIM = 64
B_VAL_DIM = 128
B_WIDTH = B_HEADS * B_VAL_DIM
B_KEY_WIDTH = B_HEADS * B_KEY_DIM
GATE_RANK = 16
GATE_NORMALIZER = 16.0
N_BUCKETS = 32
MAX_DISTANCE = 128
D_FF = 2816
EPS = 1e-6
NEG = -1e30
LOG2E = math.log2(math.e)

LANES = 128
VMEM_LIMIT = 56 * 1024 * 1024

TM_PROJ = 512
TQ = 256
TS_GLA = 512
TM_FFN = 512

A_COLS = 2 * A_WIDTH
VT_HEAD = A_HEAD_DIM + 16
VT_ROWS = A_HEADS * VT_HEAD
IDX_COLS = IDX_WIDTH + IDX_WIDTH + LANES
B_COLS = 2 * B_KEY_WIDTH + B_WIDTH + B_WIDTH + LANES
W_COLS = A_COLS + IDX_COLS + B_COLS

NT_DIMS = (((1,), (1,)), ((), ()))
TN_DIMS = (((0,), (0,)), ((), ()))


def _dot(a, b):
    return jnp.dot(a, b, preferred_element_type=F32)


def _dot_nt(a, b):
    return lax.dot_general(a, b, NT_DIMS, preferred_element_type=F32)


def _dot_tn(a, b):
    return lax.dot_general(a, b, TN_DIMS, preferred_element_type=F32)


def _rms(x, g):
    return x * lax.rsqrt(jnp.mean(x * x, axis=-1, keepdims=True) + EPS) * g


def _resident(shape):
    return pl.BlockSpec(shape, lambda *_: (0,) * len(shape), pipeline_mode=pl.Buffered(1))


def _t5_bucket_np(rel):
    half = N_BUCKETS // 2
    max_exact = half // 2
    ret = np.where(rel > 0, half, 0)
    n = np.abs(rel)
    nf = np.maximum(n, 1).astype(np.float32)
    large = max_exact + (np.log(nf / np.float32(max_exact))
                         / np.float32(math.log(MAX_DISTANCE / max_exact))
                         * np.float32(half - max_exact)).astype(np.int32)
    large = np.minimum(large, half - 1)
    return (ret + np.where(n < max_exact, n, large)).astype(np.int32)


FAR_BUCKET = int(_t5_bucket_np(np.array([-2 * TQ]))[0])


def _bias_kernel(rb_ref, bkt_ref, o_ref):
    bk = bkt_ref[0]
    for h in range(A_HEADS):
        acc = jnp.zeros(bk.shape, F32)
        for b in range(N_BUCKETS):
            acc = jnp.where(bk == b, rb_ref[b, h], acc)
        o_ref[0, h] = (acc - rb_ref[FAR_BUCKET, h]) * LOG2E


def _bias_tables(rel_bias):
    kl = np.arange(TQ)[:, None]
    ql = np.arange(TQ)[None, :]
    buckets = np.stack([_t5_bucket_np(kl - ql - TQ), _t5_bucket_np(kl - ql)])
    return pl.pallas_call(
        _bias_kernel,
        grid=(2,),
        in_specs=[pl.BlockSpec(memory_space=pltpu.SMEM),
                  pl.BlockSpec((1, TQ, TQ), lambda t: (t, 0, 0))],
        out_specs=pl.BlockSpec((1, A_HEADS, TQ, TQ), lambda t: (t, 0, 0, 0)),
        out_shape=jax.ShapeDtypeStruct((2, A_HEADS, TQ, TQ), F32),
        name="bias_tables",
    )(rel_bias, jnp.asarray(buckets))


def _in_proj_kernel(x_ref, g_ref, w_ref, wv_ref, a_ref, vt_ref, i_ref, b_ref):
    h = _rms(x_ref[...], g_ref[...]).astype(BF16)
    c0 = 0
    a_ref[:, 0:A_WIDTH] = (_dot(h, w_ref[:, c0:c0 + A_WIDTH]) * (A_HEAD_DIM ** -0.5 * LOG2E)).astype(BF16)
    c0 += A_WIDTH
    a_ref[:, A_WIDTH:2 * A_WIDTH] = _dot(h, w_ref[:, c0:c0 + A_WIDTH]).astype(BF16)
    c0 += A_WIDTH
    vt = _dot_nt(wv_ref[...], h).astype(BF16)
    ones = jnp.ones((VT_HEAD - A_HEAD_DIM, vt.shape[1]), BF16)
    for hd in range(A_HEADS):
        vt_ref[hd * VT_HEAD:hd * VT_HEAD + A_HEAD_DIM, :] = vt[hd * A_HEAD_DIM:(hd + 1) * A_HEAD_DIM, :]
        vt_ref[hd * VT_HEAD + A_HEAD_DIM:(hd + 1) * VT_HEAD, :] = ones
    i_ref[...] = _dot(h, w_ref[:, c0:c0 + IDX_COLS])
    c0 += IDX_COLS
    b_ref[...] = _dot(h, w_ref[:, c0:c0 + B_COLS])


def _pack_w_in(w_in):
    sizes = (A_WIDTH, A_WIDTH, A_WIDTH, IDX_WIDTH, IDX_DIM, IDX_HEADS,
             B_KEY_WIDTH, B_KEY_WIDTH, B_WIDTH, GATE_RANK, B_WIDTH)
    offs = np.cumsum((0,) + sizes)
    qa, ka, va, qi, ki, wi, qb, kb, vb, gd, og = [w_in[:, offs[j]:offs[j + 1]] for j in range(len(sizes))]
    d = w_in.shape[0]
    cols = [qa, ka,
            qi, jnp.tile(ki, (1, IDX_HEADS)), jnp.pad(wi, ((0, 0), (0, LANES - IDX_HEADS))),
            qb, kb, vb, og, jnp.pad(gd, ((0, 0), (0, LANES - GATE_RANK)))]
    return jnp.concatenate(cols, axis=1).astype(BF16), va.T.astype(BF16)


def _in_proj(x2d, g, w_packed, wv_t):
    t = x2d.shape[0]
    return pl.pallas_call(
        _in_proj_kernel,
        grid=(t // TM_PROJ,),
        in_specs=[pl.BlockSpec((TM_PROJ, D_MODEL), lambda r: (r, 0)),
                  _resident((1, D_MODEL)),
                  _resident((D_MODEL, W_COLS)),
                  _resident((A_WIDTH, D_MODEL))],
        out_specs=[pl.BlockSpec((TM_PROJ, A_COLS), lambda r: (r, 0)),
                   pl.BlockSpec((VT_ROWS, TM_PROJ), lambda r: (0, r)),
                   pl.BlockSpec((TM_PROJ, IDX_COLS), lambda r: (r, 0)),
                   pl.BlockSpec((TM_PROJ, B_COLS), lambda r: (r, 0))],
        out_shape=[jax.ShapeDtypeStruct((t, A_COLS), BF16),
                   jax.ShapeDtypeStruct((VT_ROWS, t), BF16),
                   jax.ShapeDtypeStruct((t, IDX_COLS), F32),
                   jax.ShapeDtypeStruct((t, B_COLS), F32)],
        compiler_params=pltpu.CompilerParams(dimension_semantics=("parallel",),
                                             vmem_limit_bytes=VMEM_LIMIT),
        name="in_proj",
    )(x2d, g, w_packed, wv_t)


def _rows8(x, op):
    r, c = x.shape
    if r % 32 == 0 and r > 32:
        return op(op(x.reshape(r // 32, 32, c), axis=0).reshape(4, 8, c), axis=0)
    return op(x.reshape(r // 8, 8, c), axis=0)


def _tree_add(parts):
    while len(parts) > 1:
        parts = [a + b for a, b in zip(parts[0::2], parts[1::2])] + ([parts[-1]] if len(parts) % 2 else [])
    return parts[0]


def _colmax(x):
    return jnp.max(_rows8(x, jnp.max), axis=0, keepdims=True)


N_COARSE = 8
N_BISECT = 12


def _dsa_kernel(q_ref, k_ref, vt_ref, iq_ref, iw_ref, ik_ref, bias_ref, o_ref,
                qi_ref, qa_ref, s_ref, sb_ref, mb_ref, m_ref, acc_ref, lg_ref, mo_ref, mn_ref):
    i = pl.program_id(1)
    nk = i + 1
    npair = (nk + 1) // 2
    big = jnp.float32(3e38)
    kk = jnp.float32(TOPK_MAX)

    def blk(kc):
        start = kc * TQ
        return pl.ds(start if isinstance(kc, int) else pl.multiple_of(start, TQ), TQ)

    def blk2(j):
        return pl.ds(pl.multiple_of(j * 2 * TQ, 2 * TQ), 2 * TQ)

    row = lax.broadcasted_iota(jnp.int32, (TQ, TQ), 0)
    col = lax.broadcasted_iota(jnp.int32, (TQ, TQ), 1)
    adm_diag = row < (col // CHUNK + 1) * CHUNK

    lane_i = lax.broadcasted_iota(jnp.int32, (1, IDX_WIDTH), 1)
    iq = iq_ref[...]
    for h in range(IDX_HEADS):
        qi_ref[h] = jnp.where(lane_i // IDX_DIM == h, iq, 0.0).astype(BF16)
    lane_p = lax.broadcasted_iota(jnp.int32, (1, LANES), 1)
    for h in range(A_HEADS):
        q_pair = q_ref[:, (h // 2) * LANES:(h // 2 + 1) * LANES]
        qa_ref[h] = jnp.where((lane_p < A_HEAD_DIM) == (h % 2 == 0), q_pair, jnp.zeros_like(q_pair))
    w_t = iw_ref[...].T * (IDX_DIM ** -0.5 * IDX_HEADS ** -0.5)

    def score_blk(kc, c):
        kic = ik_ref[blk(kc), :].astype(BF16)
        acc = jnp.zeros((TQ, TQ), F32)
        for h in range(IDX_HEADS):
            acc = acc + jnp.maximum(_dot_nt(kic, qi_ref[h]), 0.0) * w_t[h:h + 1, :]
        adm = jnp.logical_or(kc < i, adm_diag)
        s = jnp.where(adm, acc, NEG)
        s_ref[blk(kc), :] = s
        sb_ref[blk(kc), :] = s.astype(BF16)
        return (jnp.minimum(c[0], _rows8(jnp.where(adm, acc, big), jnp.min)),
                jnp.maximum(c[1], _rows8(s, jnp.max)))

    def score_two(j, c):
        return score_blk(2 * j + 1, score_blk(2 * j, c))

    c0 = lax.fori_loop(0, nk // 2, score_two, (jnp.full((8, TQ), big), jnp.full((8, TQ), -big)))
    lo8, hi8 = lax.cond(nk % 2 == 1, lambda c: score_blk(nk - 1, c), lambda c: c, c0)

    @pl.when(i == 0)
    def _():
        mb_ref[0:TQ, :] = jnp.where(adm_diag, 0.0, NEG)

    @pl.when(i > 0)
    def _():
        @pl.when(nk % 2 == 1)
        def _():
            s_ref[blk(nk), :] = jnp.full((TQ, TQ), NEG, F32)
            sb_ref[blk(nk), :] = jnp.full((TQ, TQ), NEG, BF16)

        lo = jnp.min(lo8, axis=0, keepdims=True)
        hi = jnp.max(hi8, axis=0, keepdims=True)
        q_col = lax.broadcasted_iota(jnp.int32, (1, TQ), 1)
        c_lo = (i * TQ + (q_col // CHUNK + 1) * CHUNK).astype(F32)

        def count_ge(t):
            def body(j, acc):
                return acc + _rows8(jnp.where(s_ref[blk2(j), :] >= t, 1.0, 0.0), jnp.sum)
            return jnp.sum(lax.fori_loop(0, npair, body, jnp.zeros((8, TQ), F32)), axis=0, keepdims=True)

        def count_ge16(t16):
            def body(j, acc):
                x = sb_ref[blk2(j), :].reshape(2 * TQ // 64, 64, TQ)
                hit = jnp.where(x >= t16, jnp.ones_like(x), jnp.zeros_like(x))
                part = _tree_add([hit[k] for k in range(hit.shape[0])])
                return acc + _tree_add([part[16 * k:16 * (k + 1)] for k in range(4)])
            acc = lax.fori_loop(0, npair, body, jnp.zeros((16, TQ), BF16))
            return jnp.sum(acc.astype(F32), axis=0, keepdims=True)

        def coarse(_, c):
            lo, hi, moved = c
            mid = (0.5 * lo + 0.5 * hi).astype(BF16).astype(F32)
            t16 = jnp.tile(jnp.broadcast_to(mid, (16, TQ)).astype(BF16), (4, 1))[None]
            ge = count_ge16(t16) >= kk
            return jnp.where(ge, mid, lo), jnp.where(ge, hi, mid), jnp.where(ge, 1.0, moved)
        lo_c, hi, moved = lax.fori_loop(0, N_COARSE, coarse, (lo, hi, jnp.zeros((1, TQ), F32)))
        lo = jnp.where(moved > 0.5, lo_c - jnp.abs(lo_c) * 2.0 ** -7 - 1e-37, lo)
        c_lo = jnp.where(moved > 0.5, big, c_lo)

        def bisect(_, c):
            lo, hi, c_lo = c
            mid = 0.5 * lo + 0.5 * hi
            cnt = count_ge(mid)
            ge = cnt >= kk
            return jnp.where(ge, mid, lo), jnp.where(ge, hi, mid), jnp.where(ge, cnt, c_lo)
        lo, hi, c_lo = lax.fori_loop(0, N_BISECT, bisect, (lo, hi, c_lo))
        unresolved = jnp.max(c_lo)

        @pl.when(unresolved <= kk)
        def _():
            def select(kc, carry):
                mb_ref[blk(kc), :] = jnp.where(s_ref[blk(kc), :] >= lo, 0.0, NEG)
                return carry
            lax.fori_loop(0, nk, select, 0)

        @pl.when(unresolved > kk)
        def _():
            def min_ge(j, acc):
                s = s_ref[blk2(j), :]
                return jnp.minimum(acc, _rows8(jnp.where(s >= lo, s, big), jnp.min))
            v0 = jnp.min(lax.fori_loop(0, npair, min_ge, jnp.full((8, TQ), big)), axis=0, keepdims=True)

            def above(v):
                def body(j, c):
                    s = s_ref[blk2(j), :]
                    gt = s > v
                    return (c[0] + _rows8(jnp.where(gt, 1.0, 0.0), jnp.sum),
                            jnp.minimum(c[1], _rows8(jnp.where(gt, s, big), jnp.min)))
                c8, n8 = lax.fori_loop(0, npair, body, (jnp.zeros((8, TQ), F32), jnp.full((8, TQ), big)))
                return jnp.sum(c8, axis=0, keepdims=True), jnp.min(n8, axis=0, keepdims=True)

            def peel_cond(c):
                return c[2] > 0.5

            def peel(c):
                v, _, _ = c
                c_gt, nxt = above(v)
                done = c_gt < kk
                v_new = jnp.where(done, v, nxt)
                pending = jnp.max(jnp.where(done, 0.0, 1.0))
                return v_new, c_gt, pending
            thr, c_gt, _ = lax.while_loop(peel_cond, peel, (v0, jnp.zeros((1, TQ), F32), jnp.float32(1.0)))
            need = kk - c_gt

            ltri = (row >= col).astype(BF16)

            def select(kc, carry):
                s = s_ref[blk(kc), :]
                tie = s == thr
                pref = _dot(ltri, jnp.where(tie, 1.0, 0.0).astype(BF16))
                sel = jnp.logical_or(s > thr, jnp.logical_and(tie, carry + pref <= need))
                mb_ref[blk(kc), :] = jnp.where(sel, 0.0, NEG)
                return carry + pref[TQ - 1:TQ, :]
            lax.fori_loop(0, nk, select, jnp.zeros((1, TQ), F32))

    m_ref[...] = jnp.full(m_ref.shape, -big, F32)
    acc_ref[...] = jnp.zeros(acc_ref.shape, F32)

    def qk_stage(kc, slot, near):
        mb = mb_ref[blk(kc), :]
        for h in range(A_HEADS):
            cs = slice((h // 2) * LANES, (h // 2 + 1) * LANES)
            l = _dot_nt(k_ref[blk(kc), cs], qa_ref[h]) + mb
            if near:
                l = l + bias_ref[kc - i + 1, h]
            lg_ref[slot, h] = l
            m_old = m_ref[h, 0:1, :]
            m_new = jnp.maximum(m_old, _colmax(l))
            mo_ref[slot, h, 0:1, :] = m_old
            mn_ref[slot, h, 0:1, :] = m_new
            m_ref[h, 0:1, :] = m_new

    def pv_stage(kc, slot):
        for h in range(A_HEADS):
            m_new = mn_ref[slot, h, 0:1, :]
            p = jnp.exp2(lg_ref[slot, h] - m_new).astype(BF16)
            acc_ref[h] = (acc_ref[h] * jnp.exp2(mo_ref[slot, h, 0:1, :] - m_new)
                          + _dot(vt_ref[h * VT_HEAD:(h + 1) * VT_HEAD, blk(kc)], p))

    n_far = jnp.maximum(i - 1, 0)

    @pl.when(n_far > 0)
    def _():
        qk_stage(0, 0, False)
        n_body = (n_far - 1) // 2

        def body(j, carry):
            qk_stage(2 * j + 1, 1, False)
            pv_stage(2 * j, 0)
            qk_stage(2 * j + 2, 0, False)
            pv_stage(2 * j + 1, 1)
            return carry
        lax.fori_loop(0, n_body, body, 0)
        last = 2 * n_body

        @pl.when(n_far - last == 2)
        def _():
            qk_stage(last + 1, 1, False)
        pv_stage(last, 0)

        @pl.when(n_far - last == 2)
        def _():
            pv_stage(last + 1, 1)

    @pl.when(i > 0)
    def _():
        qk_stage(i - 1, 0, True)
        qk_stage(i, 1, True)
        pv_stage(i - 1, 0)
        pv_stage(i, 1)

    @pl.when(i == 0)
    def _():
        qk_stage(0, 0, True)
        pv_stage(0, 0)

    outs = [acc_ref[h, 0:A_HEAD_DIM, :] / acc_ref[h, A_HEAD_DIM:A_HEAD_DIM + 1, :] for h in range(A_HEADS)]
    o_ref[...] = jnp.concatenate(outs, axis=0).T.astype(o_ref.dtype)


def _dsa(a_out, vt_out, idx_out, bias_tab, bsz, seq):
    nq = seq // TQ
    return pl.pallas_call(
        _dsa_kernel,
        grid=(bsz, nq),
        in_specs=[pl.BlockSpec((TQ, A_WIDTH), lambda b, i: (b * nq + i, 0)),
                  pl.BlockSpec((seq, A_WIDTH), lambda b, i: (b, 1)),
                  pl.BlockSpec((VT_ROWS, seq), lambda b, i: (0, b)),
                  pl.BlockSpec((TQ, IDX_WIDTH), lambda b, i: (b * nq + i, 0)),
                  pl.BlockSpec((TQ, LANES), lambda b, i: (b * nq + i, 2 * IDX_WIDTH // LANES)),
                  pl.BlockSpec((seq, IDX_WIDTH), lambda b, i: (b, 1)),
                  _resident((2, A_HEADS, TQ, TQ))],
        out_specs=pl.BlockSpec((TQ, A_WIDTH), lambda b, i: (b * nq + i, 0)),
        out_shape=jax.ShapeDtypeStruct((bsz * seq, A_WIDTH), BF16),
        scratch_shapes=[pltpu.VMEM((IDX_HEADS, TQ, IDX_WIDTH), BF16),
                        pltpu.VMEM((A_HEADS, TQ, LANES), BF16),
                        pltpu.VMEM((seq, TQ), F32),
                        pltpu.VMEM((seq, TQ), BF16),
                        pltpu.VMEM((seq, TQ), F32),
                        pltpu.VMEM((A_HEADS, 8, TQ), F32),
                        pltpu.VMEM((A_HEADS, VT_HEAD, TQ), F32),
                        pltpu.VMEM((2, A_HEADS, TQ, TQ), F32),
                        pltpu.VMEM((2, A_HEADS, 8, TQ), F32),
                        pltpu.VMEM((2, A_HEADS, 8, TQ), F32)],
        compiler_params=pltpu.CompilerParams(dimension_semantics=("parallel", "arbitrary"),
                                             vmem_limit_bytes=VMEM_LIMIT),
        name="dsa",
    )(a_out, a_out, vt_out, idx_out, idx_out, idx_out, bias_tab)


def _gla_kernel(qk_ref, v_ref, og_ref, gd_ref, wgu_ref, bg_ref, gn_ref, mcum_ref, mtot_ref,
                o_ref, st_ref, ob_ref):
    @pl.when(pl.program_id(1) == 0)
    def _():
        st_ref[...] = jnp.zeros_like(st_ref)

    z = _dot(gd_ref[...].astype(BF16), wgu_ref[...]) + bg_ref[...]
    g = -(jnp.maximum(-z, 0.0) + jnp.log1p(jnp.exp(-jnp.abs(z)))) * (1.0 / GATE_NORMALIZER)
    g_hi = g.astype(BF16)
    g_lo = (g - g_hi.astype(F32)).astype(BF16)
    mcum = mcum_ref[...]
    mtot = mtot_ref[...]
    bcum = _dot(mcum, g_hi) + _dot(mcum, g_lo)
    btot = _dot(mtot, g_hi) + _dot(mtot, g_lo)

    q = qk_ref[:, 0:B_KEY_WIDTH]
    k = qk_ref[:, B_KEY_WIDTH:2 * B_KEY_WIDTH]
    q_in = q * B_KEY_DIM ** -0.5 * jnp.exp(bcum)
    k_in = (k * jnp.exp(-bcum)).astype(BF16)
    k_dec = k * jnp.exp(btot - bcum)
    e_tot = jnp.exp(btot)

    lane = lax.broadcasted_iota(jnp.int32, (1, LANES), 1)
    halves = (lane < B_KEY_DIM, lane >= B_KEY_DIM)
    r_i = lax.broadcasted_iota(jnp.int32, (CHUNK, CHUNK), 0)
    c_i = lax.broadcasted_iota(jnp.int32, (CHUNK, CHUNK), 1)
    causal = r_i >= c_i

    n_chunks = TS_GLA // CHUNK
    rows = [slice(c * CHUNK, (c + 1) * CHUNK) for c in range(n_chunks)]
    qm, kdm, vh, att, upd = {}, {}, {}, {}, {}
    for pair in range(B_HEADS // 2):
        cs = slice(pair * LANES, (pair + 1) * LANES)
        for sub in range(2):
            h = 2 * pair + sub
            qm[h] = jnp.where(halves[sub], q_in[:, cs], 0.0).astype(BF16)
            kdm[h] = jnp.where(halves[sub], k_dec[:, cs], 0.0).astype(BF16)
            vh[h] = v_ref[:, h * B_VAL_DIM:(h + 1) * B_VAL_DIM].astype(BF16)
    for h in range(B_HEADS):
        ki = k_in[:, (h // 2) * LANES:(h // 2 + 1) * LANES]
        for c in range(n_chunks):
            att[h, c] = jnp.where(causal, _dot_nt(qm[h][rows[c]], ki[rows[c]]), 0.0).astype(BF16)
    for pair in range(B_HEADS // 2):
        for c in range(n_chunks):
            upd[pair, c] = (_dot_tn(vh[2 * pair][rows[c]], kdm[2 * pair][rows[c]])
                            + _dot_tn(vh[2 * pair + 1][rows[c]], kdm[2 * pair + 1][rows[c]]))
    for h in range(B_HEADS):
        for c in range(n_chunks):
            ob_ref[rows[c], h * B_VAL_DIM:(h + 1) * B_VAL_DIM] = _dot(att[h, c], vh[h][rows[c]])
    for pair in range(B_HEADS // 2):
        cs = slice(pair * LANES, (pair + 1) * LANES)
        state = st_ref[pair]
        for c in range(n_chunks):
            sb = state.astype(BF16)
            for h in (2 * pair, 2 * pair + 1):
                ob_ref[rows[c], h * B_VAL_DIM:(h + 1) * B_VAL_DIM] += _dot_nt(qm[h][rows[c]], sb)
            state = state * e_tot[c * CHUNK:c * CHUNK + 1, cs] + upd[pair, c]
        st_ref[pair] = state

    og = og_ref[...]
    gate = og * jax.nn.sigmoid(og)
    gn = gn_ref[...]
    for h in range(B_HEADS):
        cs = slice(h * B_VAL_DIM, (h + 1) * B_VAL_DIM)
        o_ref[:, cs] = (_rms(ob_ref[:, cs], gn[:, cs]) * gate[:, cs]).astype(o_ref.dtype)


def _gla(b_out, w_gate_up, b_gate, gla_norm, bsz, seq):
    ns = seq // TS_GLA
    idx = np.arange(TS_GLA)
    same = (idx[:, None] // CHUNK) == (idx[None, :] // CHUNK)
    mcum = jnp.asarray(same & (idx[None, :] <= idx[:, None]), BF16)
    mtot = jnp.asarray(same, BF16)
    wgu = jnp.pad(w_gate_up, ((0, LANES - GATE_RANK), (0, 0))).astype(BF16)
    return pl.pallas_call(
        _gla_kernel,
        grid=(bsz, ns),
        in_specs=[pl.BlockSpec((TS_GLA, 2 * B_KEY_WIDTH), lambda b, j: (b * ns + j, 0)),
                  pl.BlockSpec((TS_GLA, B_WIDTH), lambda b, j: (b * ns + j, 1)),
                  pl.BlockSpec((TS_GLA, B_WIDTH), lambda b, j: (b * ns + j, 2)),
                  pl.BlockSpec((TS_GLA, LANES), lambda b, j: (b * ns + j, (2 * B_KEY_WIDTH + 2 * B_WIDTH) // LANES)),
                  _resident((LANES, B_KEY_WIDTH)),
                  _resident((1, B_KEY_WIDTH)),
                  _resident((1, B_WIDTH)),
                  _resident((TS_GLA, TS_GLA)),
                  _resident((TS_GLA, TS_GLA))],
        out_specs=pl.BlockSpec((TS_GLA, B_WIDTH), lambda b, j: (b * ns + j, 0)),
        out_shape=jax.ShapeDtypeStruct((bsz * seq, B_WIDTH), BF16),
        scratch_shapes=[pltpu.VMEM((B_HEADS // 2, B_VAL_DIM, LANES), F32),
                        pltpu.VMEM((TS_GLA, B_WIDTH), F32)],
        compiler_params=pltpu.CompilerParams(dimension_semantics=("parallel", "arbitrary"),
                                             vmem_limit_bytes=VMEM_LIMIT),
        name="gla",
    )(b_out, b_out, b_out, b_out, wgu, b_gate.reshape(1, -1), gla_norm.reshape(1, -1), mcum, mtot)


def _out_ffn_kernel(x_ref, oa_ref, ob_ref, wo_ref, nf_ref, wg_ref, wu_ref, wd_ref, nl_ref, o_ref):
    mix = jnp.concatenate([oa_ref[...], ob_ref[...]], axis=1)
    x1 = x_ref[...] + _dot(mix, wo_ref[...])
    h = _rms(x1, nf_ref[...]).astype(BF16)
    g = _dot(h, wg_ref[...])
    u = _dot(h, wu_ref[...])
    a = (g * jax.nn.sigmoid(g) * u).astype(BF16)
    o_ref[...] = _rms(x1 + _dot(a, wd_ref[...]), nl_ref[...])


def _out_ffn(x2d, o_a, o_b, w_out, norm_ffn, w_gate, w_up, w_down, norm_final):
    t = x2d.shape[0]
    row = lambda r: (r, 0)
    return pl.pallas_call(
        _out_ffn_kernel,
        grid=(t // TM_FFN,),
        in_specs=[pl.BlockSpec((TM_FFN, D_MODEL), row),
                  pl.BlockSpec((TM_FFN, A_WIDTH), row),
                  pl.BlockSpec((TM_FFN, B_WIDTH), row),
                  _resident((A_WIDTH + B_WIDTH, D_MODEL)),
                  _resident((1, D_MODEL)),
                  _resident((D_MODEL, D_FF)),
                  _resident((D_MODEL, D_FF)),
                  _resident((D_FF, D_MODEL)),
                  _resident((1, D_MODEL))],
        out_specs=pl.BlockSpec((TM_FFN, D_MODEL), row),
        out_shape=jax.ShapeDtypeStruct((t, D_MODEL), F32),
        compiler_params=pltpu.CompilerParams(dimension_semantics=("parallel",),
                                             vmem_limit_bytes=VMEM_LIMIT),
        name="out_ffn",
    )(x2d, o_a, o_b, w_out, norm_ffn, w_gate, w_up, w_down, norm_final)


def kernel(x, norm_mix, w_in, w_gate_up, b_gate, gla_norm, w_out, rel_bias,
           norm_ffn, w_ffn_gate, w_ffn_up, w_ffn_down, norm_final):
    bsz, seq, d = x.shape
    assert w_in.shape[0] == 1, "fused final norm assumes a single layer"
    assert d == D_MODEL and seq % TS_GLA == 0 and seq % TQ == 0 and seq // 4 >= TOPK_MAX
    bias_tab = _bias_tables(rel_bias)
    x2d = x.reshape(bsz * seq, d)
    a_out, vt_out, idx_out, b_out = _in_proj(x2d, norm_mix[0].reshape(1, d), *_pack_w_in(w_in[0]))
    o_a = _dsa(a_out, vt_out, idx_out, bias_tab, bsz, seq)
    o_b = _gla(b_out, w_gate_up[0], b_gate[0], gla_norm[0], bsz, seq)
    out = _out_ffn(x2d, o_a, o_b, w_out[0].astype(BF16), norm_ffn[0].reshape(1, d),
                   w_ffn_gate[0].astype(BF16), w_ffn_up[0].astype(BF16),
                   w_ffn_down[0].astype(BF16), norm_final.reshape(1, d))
    return out.reshape(bsz, seq, d)
```

```python
import math

import numpy as np
import jax
import jax.numpy as jnp
from jax import lax
from jax.experimental import pallas as pl
from jax.experimental.pallas import tpu as pltpu

F32 = jnp.float32
BF16 = jnp.bfloat16

D_MODEL = 1024
CHUNK = 64
A_HEADS = 8
A_HEAD_DIM = 64
A_WIDTH = A_HEADS * A_HEAD_DIM
IDX_HEADS = 8
IDX_DIM = 32
IDX_WIDTH = IDX_HEADS * IDX_DIM
TOPK_MAX = 256
B_HEADS = 4
B_KEY_DIM = 64
B_VAL_DIM = 128
B_WIDTH = B_HEADS * B_VAL_DIM
B_KEY_WIDTH = B_HEADS * B_KEY_DIM
GATE_RANK = 16
GATE_NORMALIZER = 16.0
N_BUCKETS = 32
MAX_DISTANCE = 128
D_FF = 2816
EPS = 1e-6
NEG = -1e30
LOG2E = math.log2(math.e)

LANES = 128
VMEM_LIMIT = 56 * 1024 * 1024

TM_PROJ = 512
TQ = 256
TS_GLA = 512
TM_FFN = 512

A_COLS = 2 * A_WIDTH
VT_HEAD = A_HEAD_DIM + 16
VT_ROWS = A_HEADS * VT_HEAD
IDX_COLS = IDX_WIDTH + LANES
IDX_W_LANE = IDX_DIM
B_COLS = 2 * B_KEY_WIDTH + B_WIDTH + B_WIDTH + LANES
W_COLS = A_COLS + IDX_COLS + B_COLS

NT_DIMS = (((1,), (1,)), ((), ()))
TN_DIMS = (((0,), (0,)), ((), ()))


def _dot(a, b):
    return jnp.dot(a, b, preferred_element_type=F32)


def _dot_nt(a, b):
    return lax.dot_general(a, b, NT_DIMS, preferred_element_type=F32)


def _dot_tn(a, b):
    return lax.dot_general(a, b, TN_DIMS, preferred_element_type=F32)


def _rms(x, g):
    return x * lax.rsqrt(jnp.mean(x * x, axis=-1, keepdims=True) + EPS) * g


def _resident(shape):
    return pl.BlockSpec(shape, lambda *_: (0,) * len(shape), pipeline_mode=pl.Buffered(1))


def _t5_bucket_np(rel):
    half = N_BUCKETS // 2
    max_exact = half // 2
    ret = np.where(rel > 0, half, 0)
    n = np.abs(rel)
    nf = np.maximum(n, 1).astype(np.float32)
    large = max_exact + (np.log(nf / np.float32(max_exact))
                         / np.float32(math.log(MAX_DISTANCE / max_exact))
                         * np.float32(half - max_exact)).astype(np.int32)
    large = np.minimum(large, half - 1)
    return (ret + np.where(n < max_exact, n, large)).astype(np.int32)


FAR_BUCKET = int(_t5_bucket_np(np.array([-2 * TQ]))[0])


def _bias_kernel(rb_ref, bkt_ref, o_ref):
    bk = bkt_ref[0]
    for h in range(A_HEADS):
        acc = jnp.zeros(bk.shape, F32)
        for b in range(N_BUCKETS):
            acc = jnp.where(bk == b, rb_ref[b, h], acc)
        o_ref[0, h] = (acc - rb_ref[FAR_BUCKET, h]) * LOG2E


def _bias_tables(rel_bias):
    kl = np.arange(TQ)[:, None]
    ql = np.arange(TQ)[None, :]
    buckets = np.stack([_t5_bucket_np(kl - ql - TQ), _t5_bucket_np(kl - ql)])
    return pl.pallas_call(
        _bias_kernel,
        grid=(2,),
        in_specs=[pl.BlockSpec(memory_space=pltpu.SMEM),
                  pl.BlockSpec((1, TQ, TQ), lambda t: (t, 0, 0))],
        out_specs=pl.BlockSpec((1, A_HEADS, TQ, TQ), lambda t: (t, 0, 0, 0)),
        out_shape=jax.ShapeDtypeStruct((2, A_HEADS, TQ, TQ), F32),
        name="bias_tables",
    )(rel_bias, jnp.asarray(buckets))


def _in_proj_kernel(x_ref, g_ref, w_ref, wv_ref, a_ref, vt_ref, i_ref, b_ref):
    h = _rms(x_ref[...], g_ref[...]).astype(BF16)
    c0 = 0
    a_ref[:, 0:A_WIDTH] = (_dot(h, w_ref[:, c0:c0 + A_WIDTH]) * (A_HEAD_DIM ** -0.5 * LOG2E)).astype(BF16)
    c0 += A_WIDTH
    a_ref[:, A_WIDTH:2 * A_WIDTH] = _dot(h, w_ref[:, c0:c0 + A_WIDTH]).astype(BF16)
    c0 += A_WIDTH
    vt = _dot_nt(wv_ref[...], h).astype(BF16)
    ones = jnp.ones((VT_HEAD - A_HEAD_DIM, vt.shape[1]), BF16)
    for hd in range(A_HEADS):
        vt_ref[hd * VT_HEAD:hd * VT_HEAD + A_HEAD_DIM, :] = vt[hd * A_HEAD_DIM:(hd + 1) * A_HEAD_DIM, :]
        vt_ref[hd * VT_HEAD + A_HEAD_DIM:(hd + 1) * VT_HEAD, :] = ones
    i_ref[...] = _dot(h, w_ref[:, c0:c0 + IDX_COLS])
    c0 += IDX_COLS
    b_ref[...] = _dot(h, w_ref[:, c0:c0 + B_COLS])


def _pack_w_in(w_in):
    sizes = (A_WIDTH, A_WIDTH, A_WIDTH, IDX_WIDTH, IDX_DIM, IDX_HEADS,
             B_KEY_WIDTH, B_KEY_WIDTH, B_WIDTH, GATE_RANK, B_WIDTH)
    offs = np.cumsum((0,) + sizes)
    qa, ka, va, qi, ki, wi, qb, kb, vb, gd, og = [w_in[:, offs[j]:offs[j + 1]] for j in range(len(sizes))]
    d = w_in.shape[0]
    cols = [qa, ka,
            qi, jnp.pad(jnp.concatenate([ki, wi], axis=1), ((0, 0), (0, LANES - IDX_DIM - IDX_HEADS))),
            qb, kb, vb, og, jnp.pad(gd, ((0, 0), (0, LANES - GATE_RANK)))]
    return jnp.concatenate(cols, axis=1).astype(BF16), va.T.astype(BF16)


def _in_proj(x2d, g, w_packed, wv_t):
    t = x2d.shape[0]
    return pl.pallas_call(
        _in_proj_kernel,
        grid=(t // TM_PROJ,),
        in_specs=[pl.BlockSpec((TM_PROJ, D_MODEL), lambda r: (r, 0)),
                  _resident((1, D_MODEL)),
                  _resident((D_MODEL, W_COLS)),
                  _resident((A_WIDTH, D_MODEL))],
        out_specs=[pl.BlockSpec((TM_PROJ, A_COLS), lambda r: (r, 0)),
                   pl.BlockSpec((VT_ROWS, TM_PROJ), lambda r: (0, r)),
                   pl.BlockSpec((TM_PROJ, IDX_COLS), lambda r: (r, 0)),
                   pl.BlockSpec((TM_PROJ, B_COLS), lambda r: (r, 0))],
        out_shape=[jax.ShapeDtypeStruct((t, A_COLS), BF16),
                   jax.ShapeDtypeStruct((VT_ROWS, t), BF16),
                   jax.ShapeDtypeStruct((t, IDX_COLS), F32),
                   jax.ShapeDtypeStruct((t, B_COLS), F32)],
        compiler_params=pltpu.CompilerParams(dimension_semantics=("parallel",),
                                             vmem_limit_bytes=VMEM_LIMIT),
        name="in_proj",
    )(x2d, g, w_packed, wv_t)


def _rows8(x, op):
    r, c = x.shape
    if r % 32 == 0 and r > 32:
        return op(op(x.reshape(r // 32, 32, c), axis=0).reshape(4, 8, c), axis=0)
    return op(x.reshape(r // 8, 8, c), axis=0)


def _tree_add(parts):
    while len(parts) > 1:
        parts = [a + b for a, b in zip(parts[0::2], parts[1::2])] + ([parts[-1]] if len(parts) % 2 else [])
    return parts[0]


def _colmax(x):
    return jnp.max(_rows8(x, jnp.max), axis=0, keepdims=True)


N_COARSE = 8
N_BISECT = 12


def _dsa_kernel(q_ref, k_ref, vt_ref, iq_ref, iw_ref, ik_ref, bias_ref, o_ref,
                qi_ref, qa_ref, kt_ref, s_ref, sb_ref, mb_ref, m_ref, acc_ref, lg_ref, mo_ref, mn_ref):
    i = pl.program_id(1)
    nk = i + 1
    npair = (nk + 1) // 2
    big = jnp.float32(3e38)
    kk = jnp.float32(TOPK_MAX)

    def blk(kc):
        start = kc * TQ
        return pl.ds(start if isinstance(kc, int) else pl.multiple_of(start, TQ), TQ)

    def blk2(j):
        return pl.ds(pl.multiple_of(j * 2 * TQ, 2 * TQ), 2 * TQ)

    row = lax.broadcasted_iota(jnp.int32, (TQ, TQ), 0)
    col = lax.broadcasted_iota(jnp.int32, (TQ, TQ), 1)
    adm_diag = row < (col // CHUNK + 1) * CHUNK

    lane_i = lax.broadcasted_iota(jnp.int32, (1, IDX_WIDTH), 1)
    iq = iq_ref[...]
    for h in range(IDX_HEADS):
        qi_ref[h] = jnp.where(lane_i // IDX_DIM == h, iq, 0.0).astype(BF16)
    lane_p = lax.broadcasted_iota(jnp.int32, (1, LANES), 1)
    for h in range(A_HEADS):
        q_pair = q_ref[:, (h // 2) * LANES:(h // 2 + 1) * LANES]
        qa_ref[h] = jnp.where((lane_p < A_HEAD_DIM) == (h % 2 == 0), q_pair, jnp.zeros_like(q_pair))
    w_t = iw_ref[...].T * (IDX_DIM ** -0.5 * IDX_HEADS ** -0.5)

    @pl.when(i == 0)
    def _():
        src = lax.broadcasted_iota(jnp.int32, (LANES, IDX_WIDTH), 0)
        dst = lax.broadcasted_iota(jnp.int32, (LANES, IDX_WIDTH), 1)
        spread = jnp.logical_and(src < IDX_DIM, dst % IDX_DIM == src).astype(BF16)
        for kb in range(kt_ref.shape[0] // TQ):
            kt_ref[kb * TQ:(kb + 1) * TQ, :] = _dot(ik_ref[kb * TQ:(kb + 1) * TQ, :].astype(BF16),
                                                    spread).astype(BF16)

    def score_blk(kc, c):
        kic = kt_ref[blk(kc), :]
        acc = jnp.zeros((TQ, TQ), F32)
        for h in range(IDX_HEADS):
            w_h = w_t[IDX_W_LANE + h:IDX_W_LANE + h + 1, :]
            acc = acc + jnp.maximum(_dot_nt(kic, qi_ref[h]), 0.0) * w_h
        adm = jnp.logical_or(kc < i, adm_diag)
        s = jnp.where(adm, acc, NEG)
        s_ref[blk(kc), :] = s
        sb_ref[blk(kc), :] = s.astype(BF16)
        return (jnp.minimum(c[0], _rows8(jnp.where(adm, acc, big), jnp.min)),
                jnp.maximum(c[1], _rows8(s, jnp.max)))

    def score_two(j, c):
        return score_blk(2 * j + 1, score_blk(2 * j, c))

    c0 = lax.fori_loop(0, nk // 2, score_two, (jnp.full((8, TQ), big), jnp.full((8, TQ), -big)))
    lo8, hi8 = lax.cond(nk % 2 == 1, lambda c: score_blk(nk - 1, c), lambda c: c, c0)

    @pl.when(i == 0)
    def _():
        mb_ref[0:TQ, :] = jnp.where(adm_diag, 0.0, NEG)

    @pl.when(i > 0)
    def _():
        @pl.when(nk % 2 == 1)
        def _():
            s_ref[blk(nk), :] = jnp.full((TQ, TQ), NEG, F32)
            sb_ref[blk(nk), :] = jnp.full((TQ, TQ), NEG, BF16)

        lo = jnp.min(lo8, axis=0, keepdims=True)
        hi = jnp.max(hi8, axis=0, keepdims=True)
        q_col = lax.broadcasted_iota(jnp.int32, (1, TQ), 1)
        c_lo = (i * TQ + (q_col // CHUNK + 1) * CHUNK).astype(F32)

        def count_ge(t):
            def body(j, acc):
                return acc + _rows8(jnp.where(s_ref[blk2(j), :] >= t, 1.0, 0.0), jnp.sum)
            return jnp.sum(lax.fori_loop(0, npair, body, jnp.zeros((8, TQ), F32)), axis=0, keepdims=True)

        def count_ge16(t16):
            def body(j, acc):
                x = sb_ref[blk2(j), :].reshape(2 * TQ // 64, 64, TQ)
                hit = jnp.where(x >= t16, jnp.ones_like(x), jnp.zeros_like(x))
                part = _tree_add([hit[k] for k in range(hit.shape[0])])
                return acc + _tree_add([part[16 * k:16 * (k + 1)] for k in range(4)])
            acc = lax.fori_loop(0, npair, body, jnp.zeros((16, TQ), BF16))
            return jnp.sum(acc.astype(F32), axis=0, keepdims=True)

        def coarse(_, c):
            lo, hi, moved = c
            mid = (0.5 * lo + 0.5 * hi).astype(BF16).astype(F32)
            t16 = jnp.tile(jnp.broadcast_to(mid, (16, TQ)).astype(BF16), (4, 1))[None]
            ge = count_ge16(t16) >= kk
            return jnp.where(ge, mid, lo), jnp.where(ge, hi, mid), jnp.where(ge, 1.0, moved)
        lo_c, hi, moved = lax.fori_loop(0, N_COARSE, coarse, (lo, hi, jnp.zeros((1, TQ), F32)))
        lo = jnp.where(moved > 0.5, lo_c - jnp.abs(lo_c) * 2.0 ** -7 - 1e-37, lo)
        c_lo = jnp.where(moved > 0.5, big, c_lo)

        def bisect(_, c):
            lo, hi, c_lo = c
            mid = 0.5 * lo + 0.5 * hi
            cnt = count_ge(mid)
            ge = cnt >= kk
            return jnp.where(ge, mid, lo), jnp.where(ge, hi, mid), jnp.where(ge, cnt, c_lo)
        lo, hi, c_lo = lax.fori_loop(0, N_BISECT, bisect, (lo, hi, c_lo))
        unresolved = jnp.max(c_lo)

        @pl.when(unresolved <= kk)
        def _():
            def select(kc, carry):
                mb_ref[blk(kc), :] = jnp.where(s_ref[blk(kc), :] >= lo, 0.0, NEG)
                return carry
            lax.fori_loop(0, nk, select, 0)

        @pl.when(unresolved > kk)
        def _():
            def min_ge(j, acc):
                s = s_ref[blk2(j), :]
                return jnp.minimum(acc, _rows8(jnp.where(s >= lo, s, big), jnp.min))
            v0 = jnp.min(lax.fori_loop(0, npair, min_ge, jnp.full((8, TQ), big)), axis=0, keepdims=True)

            def above(v):
                def body(j, c):
                    s = s_ref[blk2(j), :]
                    gt = s > v
                    return (c[0] + _rows8(jnp.where(gt, 1.0, 0.0), jnp.sum),
                            jnp.minimum(c[1], _rows8(jnp.where(gt, s, big), jnp.min)))
                c8, n8 = lax.fori_loop(0, npair, body, (jnp.zeros((8, TQ), F32), jnp.full((8, TQ), big)))
                return jnp.sum(c8, axis=0, keepdims=True), jnp.min(n8, axis=0, keepdims=True)

            def peel_cond(c):
                return c[2] > 0.5

            def peel(c):
                v, _, _ = c
                c_gt, nxt = above(v)
                done = c_gt < kk
                v_new = jnp.where(done, v, nxt)
                pending = jnp.max(jnp.where(done, 0.0, 1.0))
                return v_new, c_gt, pending
            thr, c_gt, _ = lax.while_loop(peel_cond, peel, (v0, jnp.zeros((1, TQ), F32), jnp.float32(1.0)))
            need = kk - c_gt

            ltri = (row >= col).astype(BF16)

            def select(kc, carry):
                s = s_ref[blk(kc), :]
                tie = s == thr
                pref = _dot(ltri, jnp.where(tie, 1.0, 0.0).astype(BF16))
                sel = jnp.logical_or(s > thr, jnp.logical_and(tie, carry + pref <= need))
                mb_ref[blk(kc), :] = jnp.where(sel, 0.0, NEG)
                return carry + pref[TQ - 1:TQ, :]
            lax.fori_loop(0, nk, select, jnp.zeros((1, TQ), F32))

    m_ref[...] = jnp.full(m_ref.shape, -big, F32)
    acc_ref[...] = jnp.zeros(acc_ref.shape, F32)

    def qk_stage(kc, slot, near):
        mb = mb_ref[blk(kc), :]
        for h in range(A_HEADS):
            cs = slice((h // 2) * LANES, (h // 2 + 1) * LANES)
            l = _dot_nt(k_ref[blk(kc), cs], qa_ref[h]) + mb
            if near:
                l = l + bias_ref[kc - i + 1, h]
            lg_ref[slot, h] = l
            m_old = m_ref[h, 0:1, :]
            m_new = jnp.maximum(m_old, _colmax(l))
            mo_ref[slot, h, 0:1, :] = m_old
            mn_ref[slot, h, 0:1, :] = m_new
            m_ref[h, 0:1, :] = m_new

    def pv_stage(kc, slot):
        for h in range(A_HEADS):
            m_new = mn_ref[slot, h, 0:1, :]
            p = jnp.exp2(lg_ref[slot, h] - m_new).astype(BF16)
            acc_ref[h] = (acc_ref[h] * jnp.exp2(mo_ref[slot, h, 0:1, :] - m_new)
                          + _dot(vt_ref[h * VT_HEAD:(h + 1) * VT_HEAD, blk(kc)], p))

    n_far = jnp.maximum(i - 1, 0)

    @pl.when(n_far > 0)
    def _():
        qk_stage(0, 0, False)
        n_body = (n_far - 1) // 2

        def body(j, carry):
            qk_stage(2 * j + 1, 1, False)
            pv_stage(2 * j, 0)
            qk_stage(2 * j + 2, 0, False)
            pv_stage(2 * j + 1, 1)
            return carry
        lax.fori_loop(0, n_body, body, 0)
        last = 2 * n_body

        @pl.when(n_far - last == 2)
        def _():
            qk_stage(last + 1, 1, False)
        pv_stage(last, 0)

        @pl.when(n_far - last == 2)
        def _():
            pv_stage(last + 1, 1)

    @pl.when(i > 0)
    def _():
        qk_stage(i - 1, 0, True)
        qk_stage(i, 1, True)
        pv_stage(i - 1, 0)
        pv_stage(i, 1)

    @pl.when(i == 0)
    def _():
        qk_stage(0, 0, True)
        pv_stage(0, 0)

    outs = [acc_ref[h, 0:A_HEAD_DIM, :] / acc_ref[h, A_HEAD_DIM:A_HEAD_DIM + 1, :] for h in range(A_HEADS)]
    o_ref[...] = jnp.concatenate(outs, axis=0).T.astype(o_ref.dtype)


def _dsa(a_out, vt_out, idx_out, bias_tab, bsz, seq):
    nq = seq // TQ
    return pl.pallas_call(
        _dsa_kernel,
        grid=(bsz, nq),
        in_specs=[pl.BlockSpec((TQ, A_WIDTH), lambda b, i: (b * nq + i, 0)),
                  pl.BlockSpec((seq, A_WIDTH), lambda b, i: (b, 1)),
                  pl.BlockSpec((VT_ROWS, seq), lambda b, i: (0, b)),
                  pl.BlockSpec((TQ, IDX_WIDTH), lambda b, i: (b * nq + i, 0)),
                  pl.BlockSpec((TQ, LANES), lambda b, i: (b * nq + i, IDX_WIDTH // LANES)),
                  pl.BlockSpec((seq, LANES), lambda b, i: (b, IDX_WIDTH // LANES)),
                  _resident((2, A_HEADS, TQ, TQ))],
        out_specs=pl.BlockSpec((TQ, A_WIDTH), lambda b, i: (b * nq + i, 0)),
        out_shape=jax.ShapeDtypeStruct((bsz * seq, A_WIDTH), BF16),
        scratch_shapes=[pltpu.VMEM((IDX_HEADS, TQ, IDX_WIDTH), BF16),
                        pltpu.VMEM((A_HEADS, TQ, LANES), BF16),
                        pltpu.VMEM((seq, IDX_WIDTH), BF16),
                        pltpu.VMEM((seq, TQ), F32),
                        pltpu.VMEM((seq, TQ), BF16),
                        pltpu.VMEM((seq, TQ), F32),
                        pltpu.VMEM((A_HEADS, 8, TQ), F32),
                        pltpu.VMEM((A_HEADS, VT_HEAD, TQ), F32),
                        pltpu.VMEM((2, A_HEADS, TQ, TQ), F32),
                        pltpu.VMEM((2, A_HEADS, 8, TQ), F32),
                        pltpu.VMEM((2, A_HEADS, 8, TQ), F32)],
        compiler_params=pltpu.CompilerParams(dimension_semantics=("parallel", "arbitrary"),
                                             vmem_limit_bytes=VMEM_LIMIT),
        name="dsa",
    )(a_out, a_out, vt_out, idx_out, idx_out, idx_out, bias_tab)


def _gla_kernel(qk_ref, v_ref, og_ref, gd_ref, wgu_ref, bg_ref, gn_ref, mcum_ref, mtot_ref,
                o_ref, st_ref, ob_ref):
    @pl.when(pl.program_id(1) == 0)
    def _():
        st_ref[...] = jnp.zeros_like(st_ref)

    z = _dot(gd_ref[...].astype(BF16), wgu_ref[...]) + bg_ref[...]
    g = -(jnp.maximum(-z, 0.0) + jnp.log(1.0 + jnp.exp(-jnp.abs(z)))) * (1.0 / GATE_NORMALIZER)
    g_hi = g.astype(BF16)
    g_lo = (g - g_hi.astype(F32)).astype(BF16)
    mcum = mcum_ref[...]
    mtot = mtot_ref[...]
    bcum = _dot(mcum, g_hi) + _dot(mcum, g_lo)
    btot = _dot(mtot, g_hi) + _dot(mtot, g_lo)

    q = qk_ref[:, 0:B_KEY_WIDTH]
    k = qk_ref[:, B_KEY_WIDTH:2 * B_KEY_WIDTH]
    q_in = q * B_KEY_DIM ** -0.5 * jnp.exp(bcum)
    k_in = (k * jnp.exp(-bcum)).astype(BF16)
    k_dec = k * jnp.exp(btot - bcum)
    e_tot = jnp.exp(btot)

    lane = lax.broadcasted_iota(jnp.int32, (1, LANES), 1)
    halves = (lane < B_KEY_DIM, lane >= B_KEY_DIM)
    r_i = lax.broadcasted_iota(jnp.int32, (CHUNK, CHUNK), 0)
    c_i = lax.broadcasted_iota(jnp.int32, (CHUNK, CHUNK), 1)
    causal = r_i >= c_i

    n_chunks = TS_GLA // CHUNK
    rows = [slice(c * CHUNK, (c + 1) * CHUNK) for c in range(n_chunks)]
    qm, kdm, vh, att, upd = {}, {}, {}, {}, {}
    for pair in range(B_HEADS // 2):
        cs = slice(pair * LANES, (pair + 1) * LANES)
        for sub in range(2):
            h = 2 * pair + sub
            qm[h] = jnp.where(halves[sub], q_in[:, cs], 0.0).astype(BF16)
            kdm[h] = jnp.where(halves[sub], k_dec[:, cs], 0.0).astype(BF16)
            vh[h] = v_ref[:, h * B_VAL_DIM:(h + 1) * B_VAL_DIM].astype(BF16)
    for h in range(B_HEADS):
        ki = k_in[:, (h // 2) * LANES:(h // 2 + 1) * LANES]
        for c in range(n_chunks):
            att[h, c] = jnp.where(causal, _dot_nt(qm[h][rows[c]], ki[rows[c]]), 0.0).astype(BF16)
    for pair in range(B_HEADS // 2):
        for c in range(n_chunks):
            upd[pair, c] = (_dot_tn(vh[2 * pair][rows[c]], kdm[2 * pair][rows[c]])
                            + _dot_tn(vh[2 * pair + 1][rows[c]], kdm[2 * pair + 1][rows[c]]))
    for h in range(B_HEADS):
        for c in range(n_chunks):
            ob_ref[rows[c], h * B_VAL_DIM:(h + 1) * B_VAL_DIM] = _dot(att[h, c], vh[h][rows[c]])
    for pair in range(B_HEADS // 2):
        cs = slice(pair * LANES, (pair + 1) * LANES)
        state = st_ref[pair]
        for c in range(n_chunks):
            sb = state.astype(BF16)
            for h in (2 * pair, 2 * pair + 1):
                ob_ref[rows[c], h * B_VAL_DIM:(h + 1) * B_VAL_DIM] += _dot_nt(qm[h][rows[c]], sb)
            state = state * e_tot[c * CHUNK:c * CHUNK + 1, cs] + upd[pair, c]
        st_ref[pair] = state

    og = og_ref[...]
    gate = og * jax.nn.sigmoid(og)
    gn = gn_ref[...]
    for h in range(B_HEADS):
        cs = slice(h * B_VAL_DIM, (h + 1) * B_VAL_DIM)
        o_ref[:, cs] = (_rms(ob_ref[:, cs], gn[:, cs]) * gate[:, cs]).astype(o_ref.dtype)


def _gla(b_out, w_gate_up, b_gate, gla_norm, bsz, seq):
    ns = seq // TS_GLA
    idx = np.arange(TS_GLA)
    same = (idx[:, None] // CHUNK) == (idx[None, :] // CHUNK)
    mcum = jnp.asarray(same & (idx[None, :] <= idx[:, None]), BF16)
    mtot = jnp.asarray(same, BF16)
    wgu = jnp.pad(w_gate_up, ((0, LANES - GATE_RANK), (0, 0))).astype(BF16)
    return pl.pallas_call(
        _gla_kernel,
        grid=(bsz, ns),
        in_specs=[pl.BlockSpec((TS_GLA, 2 * B_KEY_WIDTH), lambda b, j: (b * ns + j, 0)),
                  pl.BlockSpec((TS_GLA, B_WIDTH), lambda b, j: (b * ns + j, 1)),
                  pl.BlockSpec((TS_GLA, B_WIDTH), lambda b, j: (b * ns + j, 2)),
                  pl.BlockSpec((TS_GLA, LANES), lambda b, j: (b * ns + j, (2 * B_KEY_WIDTH + 2 * B_WIDTH) // LANES)),
                  _resident((LANES, B_KEY_WIDTH)),
                  _resident((1, B_KEY_WIDTH)),
                  _resident((1, B_WIDTH)),
                  _resident((TS_GLA, TS_GLA)),
                  _resident((TS_GLA, TS_GLA))],
        out_specs=pl.BlockSpec((TS_GLA, B_WIDTH), lambda b, j: (b * ns + j, 0)),
        out_shape=jax.ShapeDtypeStruct((bsz * seq, B_WIDTH), BF16),
        scratch_shapes=[pltpu.VMEM((B_HEADS // 2, B_VAL_DIM, LANES), F32),
                        pltpu.VMEM((TS_GLA, B_WIDTH), F32)],
        compiler_params=pltpu.CompilerParams(dimension_semantics=("parallel", "arbitrary"),
                                             vmem_limit_bytes=VMEM_LIMIT),
        name="gla",
    )(b_out, b_out, b_out, b_out, wgu, b_gate.reshape(1, -1), gla_norm.reshape(1, -1), mcum, mtot)


def _out_ffn_kernel(x_ref, oa_ref, ob_ref, wo_ref, nf_ref, wg_ref, wu_ref, wd_ref, nl_ref, o_ref):
    mix = jnp.concatenate([oa_ref[...], ob_ref[...]], axis=1)
    x1 = x_ref[...] + _dot(mix, wo_ref[...])
    h = _rms(x1, nf_ref[...]).astype(BF16)
    g = _dot(h, wg_ref[...])
    u = _dot(h, wu_ref[...])
    a = (g * jax.nn.sigmoid(g) * u).astype(BF16)
    o_ref[...] = _rms(x1 + _dot(a, wd_ref[...]), nl_ref[...])


def _out_ffn(x2d, o_a, o_b, w_out, norm_ffn, w_gate, w_up, w_down, norm_final):
    t = x2d.shape[0]
    row = lambda r: (r, 0)
    return pl.pallas_call(
        _out_ffn_kernel,
        grid=(t // TM_FFN,),
        in_specs=[pl.BlockSpec((TM_FFN, D_MODEL), row),
                  pl.BlockSpec((TM_FFN, A_WIDTH), row),
                  pl.BlockSpec((TM_FFN, B_WIDTH), row),
                  _resident((A_WIDTH + B_WIDTH, D_MODEL)),
                  _resident((1, D_MODEL)),
                  _resident((D_MODEL, D_FF)),
                  _resident((D_MODEL, D_FF)),
                  _resident((D_FF, D_MODEL)),
                  _resident((1, D_MODEL))],
        out_specs=pl.BlockSpec((TM_FFN, D_MODEL), row),
        out_shape=jax.ShapeDtypeStruct((t, D_MODEL), F32),
        compiler_params=pltpu.CompilerParams(dimension_semantics=("parallel",),
                                             vmem_limit_bytes=VMEM_LIMIT),
        name="out_ffn",
    )(x2d, o_a, o_b, w_out, norm_ffn, w_gate, w_up, w_down, norm_final)


def kernel(x, norm_mix, w_in, w_gate_up, b_gate, gla_norm, w_out, rel_bias,
           norm_ffn, w_ffn_gate, w_ffn_up, w_ffn_down, norm_final):
    bsz, seq, d = x.shape
    assert w_in.shape[0] == 1, "fused final norm assumes a single layer"
    assert d == D_MODEL and seq % TS_GLA == 0 and seq % TQ == 0 and seq // 4 >= TOPK_MAX
    bias_tab = _bias_tables(rel_bias)
    x2d = x.reshape(bsz * seq, d)
    a_out, vt_out, idx_out, b_out = _in_proj(x2d, norm_mix[0].reshape(1, d), *_pack_w_in(w_in[0]))
    o_a = _dsa(a_out, vt_out, idx_out, bias_tab, bsz, seq)
    o_b = _gla(b_out, w_gate_up[0], b_gate[0], gla_norm[0], bsz, seq)
    out = _out_ffn(x2d, o_a, o_b, w_out[0].astype(BF16), norm_ffn[0].reshape(1, d),
                   w_ffn_gate[0].astype(BF16), w_ffn_up[0].astype(BF16),
                   w_ffn_down[0].astype(BF16), norm_final.reshape(1, d))
    return out.reshape(bsz, seq, d)
```

```python
import math

import numpy as np
import jax
import jax.numpy as jnp
from jax import lax
from jax.experimental import pallas as pl
from jax.experimental.pallas import tpu as pltpu

F32 = jnp.float32
BF16 = jnp.bfloat16

D_MODEL = 1024
CHUNK = 64
A_HEADS = 8
A_HEAD_DIM = 64
A_WIDTH = A_HEADS * A_HEAD_DIM
IDX_HEADS = 8
IDX_DIM = 32
IDX_WIDTH = IDX_HEADS * IDX_DIM
TOPK_MAX = 256
B_HEADS = 4
B_KEY_DIM = 64
B_VAL_DIM = 128
B_WIDTH = B_HEADS * B_VAL_DIM
B_KEY_WIDTH = B_HEADS * B_KEY_DIM
GATE_RANK = 16
GATE_NORMALIZER = 16.0
N_BUCKETS = 32
MAX_DISTANCE = 128
D_FF = 2816
EPS = 1e-6
NEG = -1e30
LOG2E = math.log2(math.e)

LANES = 128
VMEM_LIMIT = 56 * 1024 * 1024

TM_PROJ = 512
TQ = 256
TS_GLA = 512
TM_FFN = 512

A_COLS = 2 * A_WIDTH
VT_HEAD = A_HEAD_DIM + 16
VT_ROWS = A_HEADS * VT_HEAD
IDX_COLS = IDX_WIDTH + LANES
IDX_W_LANE = IDX_DIM
B_COLS = 2 * B_KEY_WIDTH + B_WIDTH + B_WIDTH + LANES
W_COLS = A_COLS + IDX_COLS + B_COLS

NT_DIMS = (((1,), (1,)), ((), ()))
TN_DIMS = (((0,), (0,)), ((), ()))


def _dot(a, b):
    return jnp.dot(a, b, preferred_element_type=F32)


def _dot_nt(a, b):
    return lax.dot_general(a, b, NT_DIMS, preferred_element_type=F32)


def _dot_tn(a, b):
    return lax.dot_general(a, b, TN_DIMS, preferred_element_type=F32)


def _rms(x, g):
    return x * lax.rsqrt(jnp.mean(x * x, axis=-1, keepdims=True) + EPS) * g


def _resident(shape):
    return pl.BlockSpec(shape, lambda *_: (0,) * len(shape), pipeline_mode=pl.Buffered(1))


def _t5_bucket_np(rel):
    half = N_BUCKETS // 2
    max_exact = half // 2
    ret = np.where(rel > 0, half, 0)
    n = np.abs(rel)
    nf = np.maximum(n, 1).astype(np.float32)
    large = max_exact + (np.log(nf / np.float32(max_exact))
                         / np.float32(math.log(MAX_DISTANCE / max_exact))
                         * np.float32(half - max_exact)).astype(np.int32)
    large = np.minimum(large, half - 1)
    return (ret + np.where(n < max_exact, n, large)).astype(np.int32)


FAR_BUCKET = int(_t5_bucket_np(np.array([-2 * TQ]))[0])


def _bias_kernel(rb_ref, bkt_ref, o_ref):
    bk = bkt_ref[0]
    for h in range(A_HEADS):
        acc = jnp.zeros(bk.shape, F32)
        for b in range(N_BUCKETS):
            acc = jnp.where(bk == b, rb_ref[b, h], acc)
        o_ref[0, h] = (acc - rb_ref[FAR_BUCKET, h]) * LOG2E


def _bias_tables(rel_bias):
    kl = np.arange(TQ)[:, None]
    ql = np.arange(TQ)[None, :]
    buckets = np.stack([_t5_bucket_np(kl - ql - TQ), _t5_bucket_np(kl - ql)])
    return pl.pallas_call(
        _bias_kernel,
        grid=(2,),
        in_specs=[pl.BlockSpec(memory_space=pltpu.SMEM),
                  pl.BlockSpec((1, TQ, TQ), lambda t: (t, 0, 0))],
        out_specs=pl.BlockSpec((1, A_HEADS, TQ, TQ), lambda t: (t, 0, 0, 0)),
        out_shape=jax.ShapeDtypeStruct((2, A_HEADS, TQ, TQ), F32),
        name="bias_tables",
    )(rel_bias, jnp.asarray(buckets))


def _in_proj_kernel(x_ref, g_ref, w_ref, wv_ref, a_ref, vt_ref, i_ref, b_ref):
    h = _rms(x_ref[...], g_ref[...]).astype(BF16)
    c0 = 0
    a_ref[:, 0:A_WIDTH] = (_dot(h, w_ref[:, c0:c0 + A_WIDTH]) * (A_HEAD_DIM ** -0.5 * LOG2E)).astype(BF16)
    c0 += A_WIDTH
    a_ref[:, A_WIDTH:2 * A_WIDTH] = _dot(h, w_ref[:, c0:c0 + A_WIDTH]).astype(BF16)
    c0 += A_WIDTH
    vt = _dot_nt(wv_ref[...], h).astype(BF16)
    ones = jnp.ones((VT_HEAD - A_HEAD_DIM, vt.shape[1]), BF16)
    for hd in range(A_HEADS):
        vt_ref[hd * VT_HEAD:hd * VT_HEAD + A_HEAD_DIM, :] = vt[hd * A_HEAD_DIM:(hd + 1) * A_HEAD_DIM, :]
        vt_ref[hd * VT_HEAD + A_HEAD_DIM:(hd + 1) * VT_HEAD, :] = ones
    i_ref[...] = _dot(h, w_ref[:, c0:c0 + IDX_COLS])
    c0 += IDX_COLS
    b_ref[...] = _dot(h, w_ref[:, c0:c0 + B_COLS])


def _pack_w_in(w_in):
    sizes = (A_WIDTH, A_WIDTH, A_WIDTH, IDX_WIDTH, IDX_DIM, IDX_HEADS,
             B_KEY_WIDTH, B_KEY_WIDTH, B_WIDTH, GATE_RANK, B_WIDTH)
    offs = np.cumsum((0,) + sizes)
    qa, ka, va, qi, ki, wi, qb, kb, vb, gd, og = [w_in[:, offs[j]:offs[j + 1]] for j in range(len(sizes))]
    d = w_in.shape[0]
    cols = [qa, ka,
            qi, jnp.pad(jnp.concatenate([ki, wi], axis=1), ((0, 0), (0, LANES - IDX_DIM - IDX_HEADS))),
            qb, kb, vb, og, jnp.pad(gd, ((0, 0), (0, LANES - GATE_RANK)))]
    return jnp.concatenate(cols, axis=1).astype(BF16), va.T.astype(BF16)


def _in_proj(x2d, g, w_packed, wv_t):
    t = x2d.shape[0]
    return pl.pallas_call(
        _in_proj_kernel,
        grid=(t // TM_PROJ,),
        in_specs=[pl.BlockSpec((TM_PROJ, D_MODEL), lambda r: (r, 0)),
                  _resident((1, D_MODEL)),
                  _resident((D_MODEL, W_COLS)),
                  _resident((A_WIDTH, D_MODEL))],
        out_specs=[pl.BlockSpec((TM_PROJ, A_COLS), lambda r: (r, 0)),
                   pl.BlockSpec((VT_ROWS, TM_PROJ), lambda r: (0, r)),
                   pl.BlockSpec((TM_PROJ, IDX_COLS), lambda r: (r, 0)),
                   pl.BlockSpec((TM_PROJ, B_COLS), lambda r: (r, 0))],
        out_shape=[jax.ShapeDtypeStruct((t, A_COLS), BF16),
                   jax.ShapeDtypeStruct((VT_ROWS, t), BF16),
                   jax.ShapeDtypeStruct((t, IDX_COLS), F32),
                   jax.ShapeDtypeStruct((t, B_COLS), F32)],
        compiler_params=pltpu.CompilerParams(dimension_semantics=("parallel",),
                                             vmem_limit_bytes=VMEM_LIMIT),
        name="in_proj",
    )(x2d, g, w_packed, wv_t)


def _rows8(x, op):
    r, c = x.shape
    if r % 32 == 0 and r > 32:
        return op(op(x.reshape(r // 32, 32, c), axis=0).reshape(4, 8, c), axis=0)
    return op(x.reshape(r // 8, 8, c), axis=0)


def _tree_add(parts):
    while len(parts) > 1:
        parts = [a + b for a, b in zip(parts[0::2], parts[1::2])] + ([parts[-1]] if len(parts) % 2 else [])
    return parts[0]


def _colmax(x):
    return jnp.max(_rows8(x, jnp.max), axis=0, keepdims=True)


N_COARSE = 8
N_BISECT = 12


def _dsa_kernel(q_ref, k_ref, vt_ref, iq_ref, iw_ref, ik_ref, bias_ref, o_ref,
                qi_ref, qa_ref, kt_ref, s_ref, sb_ref, mb_ref, m_ref, acc_ref, lg_ref, mo_ref, mn_ref):
    i = pl.program_id(1)
    nk = i + 1
    npair = (nk + 1) // 2
    big = jnp.float32(3e38)
    kk = jnp.float32(TOPK_MAX)

    def blk(kc):
        start = kc * TQ
        return pl.ds(start if isinstance(kc, int) else pl.multiple_of(start, TQ), TQ)

    def blk2(j):
        return pl.ds(pl.multiple_of(j * 2 * TQ, 2 * TQ), 2 * TQ)

    row = lax.broadcasted_iota(jnp.int32, (TQ, TQ), 0)
    col = lax.broadcasted_iota(jnp.int32, (TQ, TQ), 1)
    adm_diag = row < (col // CHUNK + 1) * CHUNK

    lane_i = lax.broadcasted_iota(jnp.int32, (1, IDX_WIDTH), 1)
    iq = iq_ref[...]
    for h in range(IDX_HEADS):
        qi_ref[h] = jnp.where(lane_i // IDX_DIM == h, iq, 0.0).astype(BF16)
    lane_p = lax.broadcasted_iota(jnp.int32, (1, LANES), 1)
    for h in range(A_HEADS):
        q_pair = q_ref[:, (h // 2) * LANES:(h // 2 + 1) * LANES]
        qa_ref[h] = jnp.where((lane_p < A_HEAD_DIM) == (h % 2 == 0), q_pair, jnp.zeros_like(q_pair))
    w_t = iw_ref[...].T * (IDX_DIM ** -0.5 * IDX_HEADS ** -0.5)

    @pl.when(i == 0)
    def _():
        src = lax.broadcasted_iota(jnp.int32, (LANES, IDX_WIDTH), 0)
        dst = lax.broadcasted_iota(jnp.int32, (LANES, IDX_WIDTH), 1)
        spread = jnp.logical_and(src < IDX_DIM, dst % IDX_DIM == src).astype(BF16)
        for kb in range(kt_ref.shape[0] // TQ):
            kt_ref[kb * TQ:(kb + 1) * TQ, :] = _dot(ik_ref[kb * TQ:(kb + 1) * TQ, :].astype(BF16),
                                                    spread).astype(BF16)

    def score_blk(kc, c):
        kic = kt_ref[blk(kc), :]
        acc = jnp.zeros((TQ, TQ), F32)
        for h in range(IDX_HEADS):
            w_h = w_t[IDX_W_LANE + h:IDX_W_LANE + h + 1, :]
            acc = acc + jnp.maximum(_dot_nt(kic, qi_ref[h]), 0.0) * w_h
        adm = jnp.logical_or(kc < i, adm_diag)
        s = jnp.where(adm, acc, NEG)
        s_ref[blk(kc), :] = s
        sb_ref[blk(kc), :] = s.astype(BF16)
        return (jnp.minimum(c[0], _rows8(jnp.where(adm, acc, big), jnp.min)),
                jnp.maximum(c[1], _rows8(s, jnp.max)))

    def score_two(j, c):
        return score_blk(2 * j + 1, score_blk(2 * j, c))

    c0 = lax.fori_loop(0, nk // 2, score_two, (jnp.full((8, TQ), big), jnp.full((8, TQ), -big)))
    lo8, hi8 = lax.cond(nk % 2 == 1, lambda c: score_blk(nk - 1, c), lambda c: c, c0)

    @pl.when(i == 0)
    def _():
        mb_ref[0:TQ, :] = jnp.where(adm_diag, 0.0, NEG)

    @pl.when(i > 0)
    def _():
        @pl.when(nk % 2 == 1)
        def _():
            s_ref[blk(nk), :] = jnp.full((TQ, TQ), NEG, F32)
            sb_ref[blk(nk), :] = jnp.full((TQ, TQ), NEG, BF16)

        lo = jnp.min(lo8, axis=0, keepdims=True)
        hi = jnp.max(hi8, axis=0, keepdims=True)
        q_col = lax.broadcasted_iota(jnp.int32, (1, TQ), 1)
        c_lo = (i * TQ + (q_col // CHUNK + 1) * CHUNK).astype(F32)

        def count_ge(t):
            def body(j, acc):
                return acc + _rows8(jnp.where(s_ref[blk2(j), :] >= t, 1.0, 0.0), jnp.sum)
            return jnp.sum(lax.fori_loop(0, npair, body, jnp.zeros((8, TQ), F32)), axis=0, keepdims=True)

        def count_ge16(t16):
            def body(j, acc):
                x = sb_ref[blk2(j), :].reshape(2 * TQ // 64, 64, TQ)
                hit = jnp.where(x >= t16, jnp.ones_like(x), jnp.zeros_like(x))
                part = _tree_add([hit[k] for k in range(hit.shape[0])])
                return acc + _tree_add([part[16 * k:16 * (k + 1)] for k in range(4)])
            acc = lax.fori_loop(0, npair, body, jnp.zeros((16, TQ), BF16))
            return jnp.sum(acc.astype(F32), axis=0, keepdims=True)

        def coarse(_, c):
            lo, hi, moved = c
            mid = (0.5 * lo + 0.5 * hi).astype(BF16).astype(F32)
            t16 = jnp.tile(jnp.broadcast_to(mid, (16, TQ)).astype(BF16), (4, 1))[None]
            ge = count_ge16(t16) >= kk
            return jnp.where(ge, mid, lo), jnp.where(ge, hi, mid), jnp.where(ge, 1.0, moved)
        lo_c, hi, moved = lax.fori_loop(0, N_COARSE, coarse, (lo, hi, jnp.zeros((1, TQ), F32)))
        lo = jnp.where(moved > 0.5, lo_c - jnp.abs(lo_c) * 2.0 ** -7 - 1e-37, lo)
        c_lo = jnp.where(moved > 0.5, big, c_lo)

        def bisect(_, c):
            lo, hi, c_lo = c
            mid = 0.5 * lo + 0.5 * hi
            cnt = count_ge(mid)
            ge = cnt >= kk
            return jnp.where(ge, mid, lo), jnp.where(ge, hi, mid), jnp.where(ge, cnt, c_lo)
        lo, hi, c_lo = lax.fori_loop(0, N_BISECT, bisect, (lo, hi, c_lo))
        unresolved = jnp.max(c_lo)

        def select_ge(t):
            def select(kc, carry):
                mb_ref[blk(kc), :] = jnp.where(s_ref[blk(kc), :] >= t, 0.0, NEG)
                return carry
            lax.fori_loop(0, nk, select, 0)

        @pl.when(unresolved <= kk)
        def _():
            select_ge(lo)

        @pl.when(unresolved > kk)
        def _():
            def min_ge(j, acc):
                s = s_ref[blk2(j), :]
                return jnp.minimum(acc, _rows8(jnp.where(s >= lo, s, big), jnp.min))
            v0 = jnp.min(lax.fori_loop(0, npair, min_ge, jnp.full((8, TQ), big)), axis=0, keepdims=True)

            def above(v):
                def body(j, c):
                    s = s_ref[blk2(j), :]
                    gt = s > v
                    return (c[0] + _rows8(jnp.where(gt, 1.0, 0.0), jnp.sum),
                            jnp.minimum(c[1], _rows8(jnp.where(gt, s, big), jnp.min)))
                c8, n8 = lax.fori_loop(0, npair, body, (jnp.zeros((8, TQ), F32), jnp.full((8, TQ), big)))
                return jnp.sum(c8, axis=0, keepdims=True), jnp.min(n8, axis=0, keepdims=True)

            def peel_cond(c):
                return c[2] > 0.5

            def peel(c):
                v, _, _ = c
                c_gt, nxt = above(v)
                done = c_gt < kk
                v_new = jnp.where(done, v, nxt)
                pending = jnp.max(jnp.where(done, 0.0, 1.0))
                return v_new, c_gt, pending
            thr, c_gt, _ = lax.while_loop(peel_cond, peel, (v0, jnp.zeros((1, TQ), F32), jnp.float32(1.0)))
            surplus = jnp.max(count_ge(thr))

            @pl.when(surplus <= kk)
            def _():
                select_ge(thr)

            @pl.when(surplus > kk)
            def _():
                need = kk - c_gt
                ltri = (row >= col).astype(BF16)

                def select(kc, carry):
                    s = s_ref[blk(kc), :]
                    tie = s == thr
                    pref = _dot(ltri, jnp.where(tie, 1.0, 0.0).astype(BF16))
                    sel = jnp.logical_or(s > thr, jnp.logical_and(tie, carry + pref <= need))
                    mb_ref[blk(kc), :] = jnp.where(sel, 0.0, NEG)
                    return carry + pref[TQ - 1:TQ, :]
                lax.fori_loop(0, nk, select, jnp.zeros((1, TQ), F32))

    m_ref[...] = jnp.full(m_ref.shape, -big, F32)
    acc_ref[...] = jnp.zeros(acc_ref.shape, F32)

    def qk_stage(kc, slot, near):
        mb = mb_ref[blk(kc), :]
        for h in range(A_HEADS):
            cs = slice((h // 2) * LANES, (h // 2 + 1) * LANES)
            l = _dot_nt(k_ref[blk(kc), cs], qa_ref[h]) + mb
            if near:
                l = l + bias_ref[kc - i + 1, h]
            lg_ref[slot, h] = l
            m_old = m_ref[h, 0:1, :]
            m_new = jnp.maximum(m_old, _colmax(l))
            mo_ref[slot, h, 0:1, :] = m_old
            mn_ref[slot, h, 0:1, :] = m_new
            m_ref[h, 0:1, :] = m_new

    def pv_stage(kc, slot):
        for h in range(A_HEADS):
            m_new = mn_ref[slot, h, 0:1, :]
            p = jnp.exp2(lg_ref[slot, h] - m_new).astype(BF16)
            acc_ref[h] = (acc_ref[h] * jnp.exp2(mo_ref[slot, h, 0:1, :] - m_new)
                          + _dot(vt_ref[h * VT_HEAD:(h + 1) * VT_HEAD, blk(kc)], p))

    n_far = jnp.maximum(i - 1, 0)

    @pl.when(n_far > 0)
    def _():
        qk_stage(0, 0, False)
        n_body = (n_far - 1) // 2

        def body(j, carry):
            qk_stage(2 * j + 1, 1, False)
            pv_stage(2 * j, 0)
            qk_stage(2 * j + 2, 0, False)
            pv_stage(2 * j + 1, 1)
            return carry
        lax.fori_loop(0, n_body, body, 0)
        last = 2 * n_body

        @pl.when(n_far - last == 2)
        def _():
            qk_stage(last + 1, 1, False)
        pv_stage(last, 0)

        @pl.when(n_far - last == 2)
        def _():
            pv_stage(last + 1, 1)

    @pl.when(i > 0)
    def _():
        qk_stage(i - 1, 0, True)
        qk_stage(i, 1, True)
        pv_stage(i - 1, 0)
        pv_stage(i, 1)

    @pl.when(i == 0)
    def _():
        qk_stage(0, 0, True)
        pv_stage(0, 0)

    outs = [acc_ref[h, 0:A_HEAD_DIM, :] / acc_ref[h, A_HEAD_DIM:A_HEAD_DIM + 1, :] for h in range(A_HEADS)]
    o_ref[...] = jnp.concatenate(outs, axis=0).T.astype(o_ref.dtype)


def _dsa(a_out, vt_out, idx_out, bias_tab, bsz, seq):
    nq = seq // TQ
    return pl.pallas_call(
        _dsa_kernel,
        grid=(bsz, nq),
        in_specs=[pl.BlockSpec((TQ, A_WIDTH), lambda b, i: (b * nq + i, 0)),
                  pl.BlockSpec((seq, A_WIDTH), lambda b, i: (b, 1)),
                  pl.BlockSpec((VT_ROWS, seq), lambda b, i: (0, b)),
                  pl.BlockSpec((TQ, IDX_WIDTH), lambda b, i: (b * nq + i, 0)),
                  pl.BlockSpec((TQ, LANES), lambda b, i: (b * nq + i, IDX_WIDTH // LANES)),
                  pl.BlockSpec((seq, LANES), lambda b, i: (b, IDX_WIDTH // LANES)),
                  _resident((2, A_HEADS, TQ, TQ))],
        out_specs=pl.BlockSpec((TQ, A_WIDTH), lambda b, i: (b * nq + i, 0)),
        out_shape=jax.ShapeDtypeStruct((bsz * seq, A_WIDTH), BF16),
        scratch_shapes=[pltpu.VMEM((IDX_HEADS, TQ, IDX_WIDTH), BF16),
                        pltpu.VMEM((A_HEADS, TQ, LANES), BF16),
                        pltpu.VMEM((seq, IDX_WIDTH), BF16),
                        pltpu.VMEM((seq, TQ), F32),
                        pltpu.VMEM((seq, TQ), BF16),
                        pltpu.VMEM((seq, TQ), F32),
                        pltpu.VMEM((A_HEADS, 8, TQ), F32),
                        pltpu.VMEM((A_HEADS, VT_HEAD, TQ), F32),
                        pltpu.VMEM((2, A_HEADS, TQ, TQ), F32),
                        pltpu.VMEM((2, A_HEADS, 8, TQ), F32),
                        pltpu.VMEM((2, A_HEADS, 8, TQ), F32)],
        compiler_params=pltpu.CompilerParams(dimension_semantics=("parallel", "arbitrary"),
                                             vmem_limit_bytes=VMEM_LIMIT),
        name="dsa",
    )(a_out, a_out, vt_out, idx_out, idx_out, idx_out, bias_tab)


def _gla_kernel(qk_ref, v_ref, og_ref, gd_ref, wgu_ref, bg_ref, gn_ref, mcum_ref, mtot_ref,
                o_ref, st_ref, ob_ref):
    @pl.when(pl.program_id(1) == 0)
    def _():
        st_ref[...] = jnp.zeros_like(st_ref)

    z = _dot(gd_ref[...].astype(BF16), wgu_ref[...]) + bg_ref[...]
    g = -(jnp.maximum(-z, 0.0) + jnp.log(1.0 + jnp.exp(-jnp.abs(z)))) * (1.0 / GATE_NORMALIZER)
    g_hi = g.astype(BF16)
    g_lo = (g - g_hi.astype(F32)).astype(BF16)
    mcum = mcum_ref[...]
    mtot = mtot_ref[...]
    bcum = _dot(mcum, g_hi) + _dot(mcum, g_lo)
    btot = _dot(mtot, g_hi) + _dot(mtot, g_lo)

    q = qk_ref[:, 0:B_KEY_WIDTH]
    k = qk_ref[:, B_KEY_WIDTH:2 * B_KEY_WIDTH]
    q_in = q * B_KEY_DIM ** -0.5 * jnp.exp(bcum)
    k_in = (k * jnp.exp(-bcum)).astype(BF16)
    k_dec = k * jnp.exp(btot - bcum)
    e_tot = jnp.exp(btot)

    lane = lax.broadcasted_iota(jnp.int32, (1, LANES), 1)
    halves = (lane < B_KEY_DIM, lane >= B_KEY_DIM)
    r_i = lax.broadcasted_iota(jnp.int32, (CHUNK, CHUNK), 0)
    c_i = lax.broadcasted_iota(jnp.int32, (CHUNK, CHUNK), 1)
    causal = r_i >= c_i

    n_chunks = TS_GLA // CHUNK
    rows = [slice(c * CHUNK, (c + 1) * CHUNK) for c in range(n_chunks)]
    qm, kdm, vh, att, upd = {}, {}, {}, {}, {}
    for pair in range(B_HEADS // 2):
        cs = slice(pair * LANES, (pair + 1) * LANES)
        for sub in range(2):
            h = 2 * pair + sub
            qm[h] = jnp.where(halves[sub], q_in[:, cs], 0.0).astype(BF16)
            kdm[h] = jnp.where(halves[sub], k_dec[:, cs], 0.0).astype(BF16)
            vh[h] = v_ref[:, h * B_VAL_DIM:(h + 1) * B_VAL_DIM].astype(BF16)
    for h in range(B_HEADS):
        ki = k_in[:, (h // 2) * LANES:(h // 2 + 1) * LANES]
        for c in range(n_chunks):
            att[h, c] = jnp.where(causal, _dot_nt(qm[h][rows[c]], ki[rows[c]]), 0.0).astype(BF16)
    for pair in range(B_HEADS // 2):
        for c in range(n_chunks):
            upd[pair, c] = (_dot_tn(vh[2 * pair][rows[c]], kdm[2 * pair][rows[c]])
                            + _dot_tn(vh[2 * pair + 1][rows[c]], kdm[2 * pair + 1][rows[c]]))
    for h in range(B_HEADS):
        for c in range(n_chunks):
            ob_ref[rows[c], h * B_VAL_DIM:(h + 1) * B_VAL_DIM] = _dot(att[h, c], vh[h][rows[c]])
    for pair in range(B_HEADS // 2):
        cs = slice(pair * LANES, (pair + 1) * LANES)
        state = st_ref[pair]
        for c in range(n_chunks):
            sb = state.astype(BF16)
            for h in (2 * pair, 2 * pair + 1):
                ob_ref[rows[c], h * B_VAL_DIM:(h + 1) * B_VAL_DIM] += _dot_nt(qm[h][rows[c]], sb)
            state = state * e_tot[c * CHUNK:c * CHUNK + 1, cs] + upd[pair, c]
        st_ref[pair] = state

    og = og_ref[...]
    gate = og * jax.nn.sigmoid(og)
    gn = gn_ref[...]
    for h in range(B_HEADS):
        cs = slice(h * B_VAL_DIM, (h + 1) * B_VAL_DIM)
        o_ref[:, cs] = (_rms(ob_ref[:, cs], gn[:, cs]) * gate[:, cs]).astype(o_ref.dtype)


def _gla(b_out, w_gate_up, b_gate, gla_norm, bsz, seq):
    ns = seq // TS_GLA
    idx = np.arange(TS_GLA)
    same = (idx[:, None] // CHUNK) == (idx[None, :] // CHUNK)
    mcum = jnp.asarray(same & (idx[None, :] <= idx[:, None]), BF16)
    mtot = jnp.asarray(same, BF16)
    wgu = jnp.pad(w_gate_up, ((0, LANES - GATE_RANK), (0, 0))).astype(BF16)
    return pl.pallas_call(
        _gla_kernel,
        grid=(bsz, ns),
        in_specs=[pl.BlockSpec((TS_GLA, 2 * B_KEY_WIDTH), lambda b, j: (b * ns + j, 0)),
                  pl.BlockSpec((TS_GLA, B_WIDTH), lambda b, j: (b * ns + j, 1)),
                  pl.BlockSpec((TS_GLA, B_WIDTH), lambda b, j: (b * ns + j, 2)),
                  pl.BlockSpec((TS_GLA, LANES), lambda b, j: (b * ns + j, (2 * B_KEY_WIDTH + 2 * B_WIDTH) // LANES)),
                  _resident((LANES, B_KEY_WIDTH)),
                  _resident((1, B_KEY_WIDTH)),
                  _resident((1, B_WIDTH)),
                  _resident((TS_GLA, TS_GLA)),
                  _resident((TS_GLA, TS_GLA))],
        out_specs=pl.BlockSpec((TS_GLA, B_WIDTH), lambda b, j: (b * ns + j, 0)),
        out_shape=jax.ShapeDtypeStruct((bsz * seq, B_WIDTH), BF16),
        scratch_shapes=[pltpu.VMEM((B_HEADS // 2, B_VAL_DIM, LANES), F32),
                        pltpu.VMEM((TS_GLA, B_WIDTH), F32)],
        compiler_params=pltpu.CompilerParams(dimension_semantics=("parallel", "arbitrary"),
                                             vmem_limit_bytes=VMEM_LIMIT),
        name="gla",
    )(b_out, b_out, b_out, b_out, wgu, b_gate.reshape(1, -1), gla_norm.reshape(1, -1), mcum, mtot)


def _out_ffn_kernel(x_ref, oa_ref, ob_ref, wo_ref, nf_ref, wg_ref, wu_ref, wd_ref, nl_ref, o_ref):
    mix = jnp.concatenate([oa_ref[...], ob_ref[...]], axis=1)
    x1 = x_ref[...] + _dot(mix, wo_ref[...])
    h = _rms(x1, nf_ref[...]).astype(BF16)
    g = _dot(h, wg_ref[...])
    u = _dot(h, wu_ref[...])
    a = (g * jax.nn.sigmoid(g) * u).astype(BF16)
    o_ref[...] = _rms(x1 + _dot(a, wd_ref[...]), nl_ref[...])


def _out_ffn(x2d, o_a, o_b, w_out, norm_ffn, w_gate, w_up, w_down, norm_final):
    t = x2d.shape[0]
    row = lambda r: (r, 0)
    return pl.pallas_call(
        _out_ffn_kernel,
        grid=(t // TM_FFN,),
        in_specs=[pl.BlockSpec((TM_FFN, D_MODEL), row),
                  pl.BlockSpec((TM_FFN, A_WIDTH), row),
                  pl.BlockSpec((TM_FFN, B_WIDTH), row),
                  _resident((A_WIDTH + B_WIDTH, D_MODEL)),
                  _resident((1, D_MODEL)),
                  _resident((D_MODEL, D_FF)),
                  _resident((D_MODEL, D_FF)),
                  _resident((D_FF, D_MODEL)),
                  _resident((1, D_MODEL))],
        out_specs=pl.BlockSpec((TM_FFN, D_MODEL), row),
        out_shape=jax.ShapeDtypeStruct((t, D_MODEL), F32),
        compiler_params=pltpu.CompilerParams(dimension_semantics=("parallel",),
                                             vmem_limit_bytes=VMEM_LIMIT),
        name="out_ffn",
    )(x2d, o_a, o_b, w_out, norm_ffn, w_gate, w_up, w_down, norm_final)


def kernel(x, norm_mix, w_in, w_gate_up, b_gate, gla_norm, w_out, rel_bias,
           norm_ffn, w_ffn_gate, w_ffn_up, w_ffn_down, norm_final):
    bsz, seq, d = x.shape
    assert w_in.shape[0] == 1, "fused final norm assumes a single layer"
    assert d == D_MODEL and seq % TS_GLA == 0 and seq % TQ == 0 and seq // 4 >= TOPK_MAX
    bias_tab = _bias_tables(rel_bias)
    x2d = x.reshape(bsz * seq, d)
    a_out, vt_out, idx_out, b_out = _in_proj(x2d, norm_mix[0].reshape(1, d), *_pack_w_in(w_in[0]))
    o_a = _dsa(a_out, vt_out, idx_out, bias_tab, bsz, seq)
    o_b = _gla(b_out, w_gate_up[0], b_gate[0], gla_norm[0], bsz, seq)
    out = _out_ffn(x2d, o_a, o_b, w_out[0].astype(BF16), norm_ffn[0].reshape(1, d),
                   w_ffn_gate[0].astype(BF16), w_ffn_up[0].astype(BF16),
                   w_ffn_down[0].astype(BF16), norm_final.reshape(1, d))
    return out.reshape(bsz, seq, d)
```

```python
import math

import numpy as np
import jax
import jax.numpy as jnp
from jax import lax
from jax.experimental import pallas as pl
from jax.experimental.pallas import tpu as pltpu

F32 = jnp.float32
BF16 = jnp.bfloat16

D_MODEL = 1024
CHUNK = 64
A_HEADS = 8
A_HEAD_DIM = 64
A_WIDTH = A_HEADS * A_HEAD_DIM
IDX_HEADS = 8
IDX_DIM = 32
IDX_WIDTH = IDX_HEADS * IDX_DIM
TOPK_MAX = 256
B_HEADS = 4
B_KEY_DIM = 64
B_VAL_DIM = 128
B_WIDTH = B_HEADS * B_VAL_DIM
B_KEY_WIDTH = B_HEADS * B_KEY_DIM
GATE_RANK = 16
GATE_NORMALIZER = 16.0
N_BUCKETS = 32
MAX_DISTANCE = 128
D_FF = 2816
EPS = 1e-6
NEG = -1e30
LOG2E = math.log2(math.e)

LANES = 128
VMEM_LIMIT = 56 * 1024 * 1024

TM_PROJ = 512
TQ = 256
TS_GLA = 512
TM_FFN = 512

A_COLS = 2 * A_WIDTH
VT_HEAD = A_HEAD_DIM + 16
VT_ROWS = A_HEADS * VT_HEAD
IDX_COLS = IDX_WIDTH + LANES
IDX_W_LANE = IDX_DIM
B_COLS = 2 * B_KEY_WIDTH + B_WIDTH + B_WIDTH + LANES
W_COLS = A_COLS + IDX_COLS + B_COLS

NT_DIMS = (((1,), (1,)), ((), ()))
TN_DIMS = (((0,), (0,)), ((), ()))


def _dot(a, b):
    return jnp.dot(a, b, preferred_element_type=F32)


def _dot_nt(a, b):
    return lax.dot_general(a, b, NT_DIMS, preferred_element_type=F32)


def _dot_tn(a, b):
    return lax.dot_general(a, b, TN_DIMS, preferred_element_type=F32)


def _rms(x, g):
    return x * lax.rsqrt(jnp.mean(x * x, axis=-1, keepdims=True) + EPS) * g


def _resident(shape):
    return pl.BlockSpec(shape, lambda *_: (0,) * len(shape), pipeline_mode=pl.Buffered(1))


def _t5_bucket_np(rel):
    half = N_BUCKETS // 2
    max_exact = half // 2
    ret = np.where(rel > 0, half, 0)
    n = np.abs(rel)
    nf = np.maximum(n, 1).astype(np.float32)
    large = max_exact + (np.log(nf / np.float32(max_exact))
                         / np.float32(math.log(MAX_DISTANCE / max_exact))
                         * np.float32(half - max_exact)).astype(np.int32)
    large = np.minimum(large, half - 1)
    return (ret + np.where(n < max_exact, n, large)).astype(np.int32)


FAR_BUCKET = int(_t5_bucket_np(np.array([-2 * TQ]))[0])


def _bias_kernel(rb_ref, bkt_ref, o_ref):
    bk = bkt_ref[0]
    for h in range(A_HEADS):
        acc = jnp.zeros(bk.shape, F32)
        for b in range(N_BUCKETS):
            acc = jnp.where(bk == b, rb_ref[b, h], acc)
        o_ref[0, h] = (acc - rb_ref[FAR_BUCKET, h]) * LOG2E


def _bias_tables(rel_bias):
    kl = np.arange(TQ)[:, None]
    ql = np.arange(TQ)[None, :]
    buckets = np.stack([_t5_bucket_np(kl - ql - TQ), _t5_bucket_np(kl - ql)])
    return pl.pallas_call(
        _bias_kernel,
        grid=(2,),
        in_specs=[pl.BlockSpec(memory_space=pltpu.SMEM),
                  pl.BlockSpec((1, TQ, TQ), lambda t: (t, 0, 0))],
        out_specs=pl.BlockSpec((1, A_HEADS, TQ, TQ), lambda t: (t, 0, 0, 0)),
        out_shape=jax.ShapeDtypeStruct((2, A_HEADS, TQ, TQ), F32),
        name="bias_tables",
    )(rel_bias, jnp.asarray(buckets))


def _in_proj_kernel(x_ref, g_ref, w_ref, wv_ref, a_ref, vt_ref, i_ref, b_ref):
    h = _rms(x_ref[...], g_ref[...]).astype(BF16)
    c0 = 0
    a_ref[:, 0:A_WIDTH] = (_dot(h, w_ref[:, c0:c0 + A_WIDTH]) * (A_HEAD_DIM ** -0.5 * LOG2E)).astype(BF16)
    c0 += A_WIDTH
    a_ref[:, A_WIDTH:2 * A_WIDTH] = _dot(h, w_ref[:, c0:c0 + A_WIDTH]).astype(BF16)
    c0 += A_WIDTH
    vt = _dot_nt(wv_ref[...], h).astype(BF16)
    ones = jnp.ones((VT_HEAD - A_HEAD_DIM, vt.shape[1]), BF16)
    for hd in range(A_HEADS):
        vt_ref[hd * VT_HEAD:hd * VT_HEAD + A_HEAD_DIM, :] = vt[hd * A_HEAD_DIM:(hd + 1) * A_HEAD_DIM, :]
        vt_ref[hd * VT_HEAD + A_HEAD_DIM:(hd + 1) * VT_HEAD, :] = ones
    i_ref[...] = _dot(h, w_ref[:, c0:c0 + IDX_COLS])
    c0 += IDX_COLS
    b_ref[...] = _dot(h, w_ref[:, c0:c0 + B_COLS])


def _pack_w_in(w_in):
    sizes = (A_WIDTH, A_WIDTH, A_WIDTH, IDX_WIDTH, IDX_DIM, IDX_HEADS,
             B_KEY_WIDTH, B_KEY_WIDTH, B_WIDTH, GATE_RANK, B_WIDTH)
    offs = np.cumsum((0,) + sizes)
    qa, ka, va, qi, ki, wi, qb, kb, vb, gd, og = [w_in[:, offs[j]:offs[j + 1]] for j in range(len(sizes))]
    d = w_in.shape[0]
    cols = [qa, ka,
            qi, jnp.pad(jnp.concatenate([ki, wi], axis=1), ((0, 0), (0, LANES - IDX_DIM - IDX_HEADS))),
            qb, kb, vb, og, jnp.pad(gd, ((0, 0), (0, LANES - GATE_RANK)))]
    return jnp.concatenate(cols, axis=1).astype(BF16), va.T.astype(BF16)


def _in_proj(x2d, g, w_packed, wv_t):
    t = x2d.shape[0]
    return pl.pallas_call(
        _in_proj_kernel,
        grid=(t // TM_PROJ,),
        in_specs=[pl.BlockSpec((TM_PROJ, D_MODEL), lambda r: (r, 0)),
                  _resident((1, D_MODEL)),
                  _resident((D_MODEL, W_COLS)),
                  _resident((A_WIDTH, D_MODEL))],
        out_specs=[pl.BlockSpec((TM_PROJ, A_COLS), lambda r: (r, 0)),
                   pl.BlockSpec((VT_ROWS, TM_PROJ), lambda r: (0, r)),
                   pl.BlockSpec((TM_PROJ, IDX_COLS), lambda r: (r, 0)),
                   pl.BlockSpec((TM_PROJ, B_COLS), lambda r: (r, 0))],
        out_shape=[jax.ShapeDtypeStruct((t, A_COLS), BF16),
                   jax.ShapeDtypeStruct((VT_ROWS, t), BF16),
                   jax.ShapeDtypeStruct((t, IDX_COLS), F32),
                   jax.ShapeDtypeStruct((t, B_COLS), F32)],
        compiler_params=pltpu.CompilerParams(dimension_semantics=("parallel",),
                                             vmem_limit_bytes=VMEM_LIMIT),
        name="in_proj",
    )(x2d, g, w_packed, wv_t)


def _rows8(x, op):
    r, c = x.shape
    if r % 32 == 0 and r > 32:
        return op(op(x.reshape(r // 32, 32, c), axis=0).reshape(4, 8, c), axis=0)
    return op(x.reshape(r // 8, 8, c), axis=0)


def _tree_add(parts):
    while len(parts) > 1:
        parts = [a + b for a, b in zip(parts[0::2], parts[1::2])] + ([parts[-1]] if len(parts) % 2 else [])
    return parts[0]


def _colmax(x):
    return jnp.max(_rows8(x, jnp.max), axis=0, keepdims=True)


N_COARSE = 8
N_BISECT = 12


def _dsa_kernel(q_ref, k_ref, vt_ref, iq_ref, iw_ref, ik_ref, bias_ref, o_ref,
                qi_ref, qa_ref, kt_ref, s_ref, sb_ref, mb_ref, m_ref, acc_ref, lg_ref, mo_ref, mn_ref):
    i = pl.program_id(1)
    nk = i + 1
    npair = (nk + 1) // 2
    big = jnp.float32(3e38)
    kk = jnp.float32(TOPK_MAX)

    def blk(kc):
        start = kc * TQ
        return pl.ds(start if isinstance(kc, int) else pl.multiple_of(start, TQ), TQ)

    def blk2(j):
        return pl.ds(pl.multiple_of(j * 2 * TQ, 2 * TQ), 2 * TQ)

    row = lax.broadcasted_iota(jnp.int32, (TQ, TQ), 0)
    col = lax.broadcasted_iota(jnp.int32, (TQ, TQ), 1)
    adm_diag = row < (col // CHUNK + 1) * CHUNK

    lane_i = lax.broadcasted_iota(jnp.int32, (1, IDX_WIDTH), 1)
    iq = iq_ref[...]
    for h in range(IDX_HEADS):
        qi_ref[h] = jnp.where(lane_i // IDX_DIM == h, iq, 0.0).astype(BF16)
    lane_p = lax.broadcasted_iota(jnp.int32, (1, LANES), 1)
    for h in range(A_HEADS):
        q_pair = q_ref[:, (h // 2) * LANES:(h // 2 + 1) * LANES]
        qa_ref[h] = jnp.where((lane_p < A_HEAD_DIM) == (h % 2 == 0), q_pair, jnp.zeros_like(q_pair))
    w_t = iw_ref[...].T * (IDX_DIM ** -0.5 * IDX_HEADS ** -0.5)

    @pl.when(i == 0)
    def _():
        src = lax.broadcasted_iota(jnp.int32, (LANES, IDX_WIDTH), 0)
        dst = lax.broadcasted_iota(jnp.int32, (LANES, IDX_WIDTH), 1)
        spread = jnp.logical_and(src < IDX_DIM, dst % IDX_DIM == src).astype(BF16)
        for kb in range(kt_ref.shape[0] // TQ):
            kt_ref[kb * TQ:(kb + 1) * TQ, :] = _dot(ik_ref[kb * TQ:(kb + 1) * TQ, :].astype(BF16),
                                                    spread).astype(BF16)

    def score_blk(kc, c):
        kic = kt_ref[blk(kc), :]
        acc = jnp.zeros((TQ, TQ), F32)
        for h in range(IDX_HEADS):
            w_h = w_t[IDX_W_LANE + h:IDX_W_LANE + h + 1, :]
            acc = acc + jnp.maximum(_dot_nt(kic, qi_ref[h]), 0.0) * w_h
        adm = jnp.logical_or(kc < i, adm_diag)
        s = jnp.where(adm, acc, NEG)
        s_ref[blk(kc), :] = s
        sb_ref[blk(kc), :] = s.astype(BF16)
        return (jnp.minimum(c[0], _rows8(jnp.where(adm, acc, big), jnp.min)),
                jnp.maximum(c[1], _rows8(s, jnp.max)))

    def score_two(j, c):
        return score_blk(2 * j + 1, score_blk(2 * j, c))

    c0 = lax.fori_loop(0, nk // 2, score_two, (jnp.full((8, TQ), big), jnp.full((8, TQ), -big)))
    lo8, hi8 = lax.cond(nk % 2 == 1, lambda c: score_blk(nk - 1, c), lambda c: c, c0)

    @pl.when(i == 0)
    def _():
        mb_ref[0:TQ, :] = jnp.where(adm_diag, 0.0, NEG)

    @pl.when(i > 0)
    def _():
        @pl.when(nk % 2 == 1)
        def _():
            s_ref[blk(nk), :] = jnp.full((TQ, TQ), NEG, F32)
            sb_ref[blk(nk), :] = jnp.full((TQ, TQ), NEG, BF16)

        lo = jnp.min(lo8, axis=0, keepdims=True)
        hi = jnp.max(hi8, axis=0, keepdims=True)
        q_col = lax.broadcasted_iota(jnp.int32, (1, TQ), 1)
        c_lo = (i * TQ + (q_col // CHUNK + 1) * CHUNK).astype(F32)

        def count_ge(t):
            def body(j, acc):
                return acc + _rows8(jnp.where(s_ref[blk2(j), :] >= t, 1.0, 0.0), jnp.sum)
            return jnp.sum(lax.fori_loop(0, npair, body, jnp.zeros((8, TQ), F32)), axis=0, keepdims=True)

        def count_ge16(t16):
            def body(j, acc):
                x = sb_ref[blk2(j), :].reshape(2 * TQ // 64, 64, TQ)
                hit = jnp.where(x >= t16, jnp.ones_like(x), jnp.zeros_like(x))
                part = _tree_add([hit[k] for k in range(hit.shape[0])])
                return acc + _tree_add([part[16 * k:16 * (k + 1)] for k in range(4)])
            acc = lax.fori_loop(0, npair, body, jnp.zeros((16, TQ), BF16))
            return jnp.sum(acc.astype(F32), axis=0, keepdims=True)

        def coarse(_, c):
            lo, hi, moved = c
            mid = (0.5 * lo + 0.5 * hi).astype(BF16).astype(F32)
            t16 = jnp.tile(jnp.broadcast_to(mid, (16, TQ)).astype(BF16), (4, 1))[None]
            ge = count_ge16(t16) >= kk
            return jnp.where(ge, mid, lo), jnp.where(ge, hi, mid), jnp.where(ge, 1.0, moved)
        lo_c, hi, moved = lax.fori_loop(0, N_COARSE, coarse, (lo, hi, jnp.zeros((1, TQ), F32)))
        lo = jnp.where(moved > 0.5, lo_c - jnp.abs(lo_c) * 2.0 ** -7 - 1e-37, lo)
        c_lo = jnp.where(moved > 0.5, big, c_lo)

        def bisect(_, c):
            lo, hi, c_lo = c
            mid = 0.5 * lo + 0.5 * hi
            cnt = count_ge(mid)
            ge = cnt >= kk
            return jnp.where(ge, mid, lo), jnp.where(ge, hi, mid), jnp.where(ge, cnt, c_lo)
        lo, hi, c_lo = lax.fori_loop(0, N_BISECT, bisect, (lo, hi, c_lo))
        unresolved = jnp.max(c_lo)

        @pl.when(unresolved <= kk)
        def _():
            def select(kc, carry):
                mb_ref[blk(kc), :] = jnp.where(s_ref[blk(kc), :] >= lo, 0.0, NEG)
                return carry
            lax.fori_loop(0, nk, select, 0)

        @pl.when(unresolved > kk)
        def _():
            def min_ge(j, acc):
                s = s_ref[blk2(j), :]
                return jnp.minimum(acc, _rows8(jnp.where(s >= lo, s, big), jnp.min))
            v0 = jnp.min(lax.fori_loop(0, npair, min_ge, jnp.full((8, TQ), big)), axis=0, keepdims=True)

            def above(v):
                def body(j, c):
                    s = s_ref[blk2(j), :]
                    gt = s > v
                    return (c[0] + _rows8(jnp.where(gt, 1.0, 0.0), jnp.sum),
                            jnp.minimum(c[1], _rows8(jnp.where(gt, s, big), jnp.min)))
                c8, n8 = lax.fori_loop(0, npair, body, (jnp.zeros((8, TQ), F32), jnp.full((8, TQ), big)))
                return jnp.sum(c8, axis=0, keepdims=True), jnp.min(n8, axis=0, keepdims=True)

            def peel_cond(c):
                return c[2] > 0.5

            def peel(c):
                v, _, _ = c
                c_gt, nxt = above(v)
                done = c_gt < kk
                v_new = jnp.where(done, v, nxt)
                pending = jnp.max(jnp.where(done, 0.0, 1.0))
                return v_new, c_gt, pending
            thr, c_gt, _ = lax.while_loop(peel_cond, peel, (v0, jnp.zeros((1, TQ), F32), jnp.float32(1.0)))
            need = kk - c_gt

            ltri = (row >= col).astype(BF16)

            def select_blk(kc, carry):
                s = s_ref[blk(kc), :]
                tie = s == thr
                pref = _dot(ltri, jnp.where(tie, 1.0, 0.0).astype(BF16))
                rank = jnp.where(tie, carry + pref, big)
                mb_ref[blk(kc), :] = jnp.where(s > thr, 0.0, jnp.where(rank <= need, 0.0, NEG))
                return carry + pref[TQ - 1:TQ, :]

            def select_two(j, carry):
                return select_blk(2 * j + 1, select_blk(2 * j, carry))
            lax.fori_loop(0, npair, select_two, jnp.zeros((1, TQ), F32))

    m_ref[...] = jnp.full(m_ref.shape, -big, F32)
    acc_ref[...] = jnp.zeros(acc_ref.shape, F32)

    def qk_stage(kc, slot, near):
        mb = mb_ref[blk(kc), :]
        for h in range(A_HEADS):
            cs = slice((h // 2) * LANES, (h // 2 + 1) * LANES)
            l = _dot_nt(k_ref[blk(kc), cs], qa_ref[h]) + mb
            if near:
                l = l + bias_ref[kc - i + 1, h]
            lg_ref[slot, h] = l
            m_old = m_ref[h, 0:1, :]
            m_new = jnp.maximum(m_old, _colmax(l))
            mo_ref[slot, h, 0:1, :] = m_old
            mn_ref[slot, h, 0:1, :] = m_new
            m_ref[h, 0:1, :] = m_new

    def pv_stage(kc, slot):
        for h in range(A_HEADS):
            m_new = mn_ref[slot, h, 0:1, :]
            p = jnp.exp2(lg_ref[slot, h] - m_new).astype(BF16)
            acc_ref[h] = (acc_ref[h] * jnp.exp2(mo_ref[slot, h, 0:1, :] - m_new)
                          + _dot(vt_ref[h * VT_HEAD:(h + 1) * VT_HEAD, blk(kc)], p))

    n_far = jnp.maximum(i - 1, 0)

    @pl.when(n_far > 0)
    def _():
        qk_stage(0, 0, False)
        n_body = (n_far - 1) // 2

        def body(j, carry):
            qk_stage(2 * j + 1, 1, False)
            pv_stage(2 * j, 0)
            qk_stage(2 * j + 2, 0, False)
            pv_stage(2 * j + 1, 1)
            return carry
        lax.fori_loop(0, n_body, body, 0)
        last = 2 * n_body

        @pl.when(n_far - last == 2)
        def _():
            qk_stage(last + 1, 1, False)
        pv_stage(last, 0)

        @pl.when(n_far - last == 2)
        def _():
            pv_stage(last + 1, 1)

    @pl.when(i > 0)
    def _():
        qk_stage(i - 1, 0, True)
        qk_stage(i, 1, True)
        pv_stage(i - 1, 0)
        pv_stage(i, 1)

    @pl.when(i == 0)
    def _():
        qk_stage(0, 0, True)
        pv_stage(0, 0)

    outs = [acc_ref[h, 0:A_HEAD_DIM, :] / acc_ref[h, A_HEAD_DIM:A_HEAD_DIM + 1, :] for h in range(A_HEADS)]
    o_ref[...] = jnp.concatenate(outs, axis=0).T.astype(o_ref.dtype)


def _dsa(a_out, vt_out, idx_out, bias_tab, bsz, seq):
    nq = seq // TQ
    return pl.pallas_call(
        _dsa_kernel,
        grid=(bsz, nq),
        in_specs=[pl.BlockSpec((TQ, A_WIDTH), lambda b, i: (b * nq + i, 0)),
                  pl.BlockSpec((seq, A_WIDTH), lambda b, i: (b, 1)),
                  pl.BlockSpec((VT_ROWS, seq), lambda b, i: (0, b)),
                  pl.BlockSpec((TQ, IDX_WIDTH), lambda b, i: (b * nq + i, 0)),
                  pl.BlockSpec((TQ, LANES), lambda b, i: (b * nq + i, IDX_WIDTH // LANES)),
                  pl.BlockSpec((seq, LANES), lambda b, i: (b, IDX_WIDTH // LANES)),
                  _resident((2, A_HEADS, TQ, TQ))],
        out_specs=pl.BlockSpec((TQ, A_WIDTH), lambda b, i: (b * nq + i, 0)),
        out_shape=jax.ShapeDtypeStruct((bsz * seq, A_WIDTH), BF16),
        scratch_shapes=[pltpu.VMEM((IDX_HEADS, TQ, IDX_WIDTH), BF16),
                        pltpu.VMEM((A_HEADS, TQ, LANES), BF16),
                        pltpu.VMEM((seq, IDX_WIDTH), BF16),
                        pltpu.VMEM((seq, TQ), F32),
                        pltpu.VMEM((seq, TQ), BF16),
                        pltpu.VMEM((seq, TQ), F32),
                        pltpu.VMEM((A_HEADS, 8, TQ), F32),
                        pltpu.VMEM((A_HEADS, VT_HEAD, TQ), F32),
                        pltpu.VMEM((2, A_HEADS, TQ, TQ), F32),
                        pltpu.VMEM((2, A_HEADS, 8, TQ), F32),
                        pltpu.VMEM((2, A_HEADS, 8, TQ), F32)],
        compiler_params=pltpu.CompilerParams(dimension_semantics=("parallel", "arbitrary"),
                                             vmem_limit_bytes=VMEM_LIMIT),
        name="dsa",
    )(a_out, a_out, vt_out, idx_out, idx_out, idx_out, bias_tab)


def _gla_kernel(qk_ref, v_ref, og_ref, gd_ref, wgu_ref, bg_ref, gn_ref, mcum_ref, mtot_ref,
                o_ref, st_ref, ob_ref):
    @pl.when(pl.program_id(1) == 0)
    def _():
        st_ref[...] = jnp.zeros_like(st_ref)

    z = _dot(gd_ref[...].astype(BF16), wgu_ref[...]) + bg_ref[...]
    g = -(jnp.maximum(-z, 0.0) + jnp.log(1.0 + jnp.exp(-jnp.abs(z)))) * (1.0 / GATE_NORMALIZER)
    g_hi = g.astype(BF16)
    g_lo = (g - g_hi.astype(F32)).astype(BF16)
    mcum = mcum_ref[...]
    mtot = mtot_ref[...]
    bcum = _dot(mcum, g_hi) + _dot(mcum, g_lo)
    btot = _dot(mtot, g_hi) + _dot(mtot, g_lo)

    q = qk_ref[:, 0:B_KEY_WIDTH]
    k = qk_ref[:, B_KEY_WIDTH:2 * B_KEY_WIDTH]
    q_in = q * B_KEY_DIM ** -0.5 * jnp.exp(bcum)
    k_in = (k * jnp.exp(-bcum)).astype(BF16)
    k_dec = k * jnp.exp(btot - bcum)
    e_tot = jnp.exp(btot)

    lane = lax.broadcasted_iota(jnp.int32, (1, LANES), 1)
    halves = (lane < B_KEY_DIM, lane >= B_KEY_DIM)
    r_i = lax.broadcasted_iota(jnp.int32, (CHUNK, CHUNK), 0)
    c_i = lax.broadcasted_iota(jnp.int32, (CHUNK, CHUNK), 1)
    causal = r_i >= c_i

    n_chunks = TS_GLA // CHUNK
    rows = [slice(c * CHUNK, (c + 1) * CHUNK) for c in range(n_chunks)]
    qm, kdm, vh, att, upd = {}, {}, {}, {}, {}
    for pair in range(B_HEADS // 2):
        cs = slice(pair * LANES, (pair + 1) * LANES)
        for sub in range(2):
            h = 2 * pair + sub
            qm[h] = jnp.where(halves[sub], q_in[:, cs], 0.0).astype(BF16)
            kdm[h] = jnp.where(halves[sub], k_dec[:, cs], 0.0).astype(BF16)
            vh[h] = v_ref[:, h * B_VAL_DIM:(h + 1) * B_VAL_DIM].astype(BF16)
    for h in range(B_HEADS):
        ki = k_in[:, (h // 2) * LANES:(h // 2 + 1) * LANES]
        for c in range(n_chunks):
            att[h, c] = jnp.where(causal, _dot_nt(qm[h][rows[c]], ki[rows[c]]), 0.0).astype(BF16)
    for pair in range(B_HEADS // 2):
        for c in range(n_chunks):
            upd[pair, c] = (_dot_tn(vh[2 * pair][rows[c]], kdm[2 * pair][rows[c]])
                            + _dot_tn(vh[2 * pair + 1][rows[c]], kdm[2 * pair + 1][rows[c]]))
    for h in range(B_HEADS):
        for c in range(n_chunks):
            ob_ref[rows[c], h * B_VAL_DIM:(h + 1) * B_VAL_DIM] = _dot(att[h, c], vh[h][rows[c]])
    for pair in range(B_HEADS // 2):
        cs = slice(pair * LANES, (pair + 1) * LANES)
        state = st_ref[pair]
        for c in range(n_chunks):
            sb = state.astype(BF16)
            for h in (2 * pair, 2 * pair + 1):
                ob_ref[rows[c], h * B_VAL_DIM:(h + 1) * B_VAL_DIM] += _dot_nt(qm[h][rows[c]], sb)
            state = state * e_tot[c * CHUNK:c * CHUNK + 1, cs] + upd[pair, c]
        st_ref[pair] = state

    og = og_ref[...]
    gate = og * jax.nn.sigmoid(og)
    gn = gn_ref[...]
    for h in range(B_HEADS):
        cs = slice(h * B_VAL_DIM, (h + 1) * B_VAL_DIM)
        o_ref[:, cs] = (_rms(ob_ref[:, cs], gn[:, cs]) * gate[:, cs]).astype(o_ref.dtype)


def _gla(b_out, w_gate_up, b_gate, gla_norm, bsz, seq):
    ns = seq // TS_GLA
    idx = np.arange(TS_GLA)
    same = (idx[:, None] // CHUNK) == (idx[None, :] // CHUNK)
    mcum = jnp.asarray(same & (idx[None, :] <= idx[:, None]), BF16)
    mtot = jnp.asarray(same, BF16)
    wgu = jnp.pad(w_gate_up, ((0, LANES - GATE_RANK), (0, 0))).astype(BF16)
    return pl.pallas_call(
        _gla_kernel,
        grid=(bsz, ns),
        in_specs=[pl.BlockSpec((TS_GLA, 2 * B_KEY_WIDTH), lambda b, j: (b * ns + j, 0)),
                  pl.BlockSpec((TS_GLA, B_WIDTH), lambda b, j: (b * ns + j, 1)),
                  pl.BlockSpec((TS_GLA, B_WIDTH), lambda b, j: (b * ns + j, 2)),
                  pl.BlockSpec((TS_GLA, LANES), lambda b, j: (b * ns + j, (2 * B_KEY_WIDTH + 2 * B_WIDTH) // LANES)),
                  _resident((LANES, B_KEY_WIDTH)),
                  _resident((1, B_KEY_WIDTH)),
                  _resident((1, B_WIDTH)),
                  _resident((TS_GLA, TS_GLA)),
                  _resident((TS_GLA, TS_GLA))],
        out_specs=pl.BlockSpec((TS_GLA, B_WIDTH), lambda b, j: (b * ns + j, 0)),
        out_shape=jax.ShapeDtypeStruct((bsz * seq, B_WIDTH), BF16),
        scratch_shapes=[pltpu.VMEM((B_HEADS // 2, B_VAL_DIM, LANES), F32),
                        pltpu.VMEM((TS_GLA, B_WIDTH), F32)],
        compiler_params=pltpu.CompilerParams(dimension_semantics=("parallel", "arbitrary"),
                                             vmem_limit_bytes=VMEM_LIMIT),
        name="gla",
    )(b_out, b_out, b_out, b_out, wgu, b_gate.reshape(1, -1), gla_norm.reshape(1, -1), mcum, mtot)


def _out_ffn_kernel(x_ref, oa_ref, ob_ref, wo_ref, nf_ref, wg_ref, wu_ref, wd_ref, nl_ref, o_ref):
    mix = jnp.concatenate([oa_ref[...], ob_ref[...]], axis=1)
    x1 = x_ref[...] + _dot(mix, wo_ref[...])
    h = _rms(x1, nf_ref[...]).astype(BF16)
    g = _dot(h, wg_ref[...])
    u = _dot(h, wu_ref[...])
    a = (g * jax.nn.sigmoid(g) * u).astype(BF16)
    o_ref[...] = _rms(x1 + _dot(a, wd_ref[...]), nl_ref[...])


def _out_ffn(x2d, o_a, o_b, w_out, norm_ffn, w_gate, w_up, w_down, norm_final):
    t = x2d.shape[0]
    row = lambda r: (r, 0)
    return pl.pallas_call(
        _out_ffn_kernel,
        grid=(t // TM_FFN,),
        in_specs=[pl.BlockSpec((TM_FFN, D_MODEL), row),
                  pl.BlockSpec((TM_FFN, A_WIDTH), row),
                  pl.BlockSpec((TM_FFN, B_WIDTH), row),
                  _resident((A_WIDTH + B_WIDTH, D_MODEL)),
                  _resident((1, D_MODEL)),
                  _resident((D_MODEL, D_FF)),
                  _resident((D_MODEL, D_FF)),
                  _resident((D_FF, D_MODEL)),
                  _resident((1, D_MODEL))],
        out_specs=pl.BlockSpec((TM_FFN, D_MODEL), row),
        out_shape=jax.ShapeDtypeStruct((t, D_MODEL), F32),
        compiler_params=pltpu.CompilerParams(dimension_semantics=("parallel",),
                                             vmem_limit_bytes=VMEM_LIMIT),
        name="out_ffn",
    )(x2d, o_a, o_b, w_out, norm_ffn, w_gate, w_up, w_down, norm_final)


def kernel(x, norm_mix, w_in, w_gate_up, b_gate, gla_norm, w_out, rel_bias,
           norm_ffn, w_ffn_gate, w_ffn_up, w_ffn_down, norm_final):
    bsz, seq, d = x.shape
    assert w_in.shape[0] == 1, "fused final norm assumes a single layer"
    assert d == D_MODEL and seq % TS_GLA == 0 and seq % TQ == 0 and seq // 4 >= TOPK_MAX
    bias_tab = _bias_tables(rel_bias)
    x2d = x.reshape(bsz * seq, d)
    a_out, vt_out, idx_out, b_out = _in_proj(x2d, norm_mix[0].reshape(1, d), *_pack_w_in(w_in[0]))
    o_a = _dsa(a_out, vt_out, idx_out, bias_tab, bsz, seq)
    o_b = _gla(b_out, w_gate_up[0], b_gate[0], gla_norm[0], bsz, seq)
    out = _out_ffn(x2d, o_a, o_b, w_out[0].astype(BF16), norm_ffn[0].reshape(1, d),
                   w_ffn_gate[0].astype(BF16), w_ffn_up[0].astype(BF16),
                   w_ffn_down[0].astype(BF16), norm_final.reshape(1, d))
    return out.reshape(bsz, seq, d)
```

```python
import math

import numpy as np
import jax
import jax.numpy as jnp
from jax import lax
from jax.experimental import pallas as pl
from jax.experimental.pallas import tpu as pltpu

F32 = jnp.float32
BF16 = jnp.bfloat16

D_MODEL = 1024
CHUNK = 64
A_HEADS = 8
A_HEAD_DIM = 64
A_WIDTH = A_HEADS * A_HEAD_DIM
IDX_HEADS = 8
IDX_DIM = 32
IDX_WIDTH = IDX_HEADS * IDX_DIM
TOPK_MAX = 256
B_HEADS = 4
B_KEY_DIM = 64
B_VAL_DIM = 128
B_WIDTH = B_HEADS * B_VAL_DIM
B_KEY_WIDTH = B_HEADS * B_KEY_DIM
GATE_RANK = 16
GATE_NORMALIZER = 16.0
N_BUCKETS = 32
MAX_DISTANCE = 128
D_FF = 2816
EPS = 1e-6
NEG = -1e30
LOG2E = math.log2(math.e)

LANES = 128
VMEM_LIMIT = 56 * 1024 * 1024

TM_PROJ = 512
TQ = 256
TS_GLA = 512
TM_FFN = 512

A_COLS = 2 * A_WIDTH
VT_HEAD = A_HEAD_DIM + 16
VT_ROWS = A_HEADS * VT_HEAD
IDX_COLS = IDX_WIDTH + LANES
IDX_W_LANE = IDX_DIM
B_COLS = 2 * B_KEY_WIDTH + B_WIDTH + B_WIDTH + LANES
W_COLS = A_COLS + IDX_COLS + B_COLS

NT_DIMS = (((1,), (1,)), ((), ()))
TN_DIMS = (((0,), (0,)), ((), ()))


def _dot(a, b):
    return jnp.dot(a, b, preferred_element_type=F32)


def _dot_nt(a, b):
    return lax.dot_general(a, b, NT_DIMS, preferred_element_type=F32)


def _dot_tn(a, b):
    return lax.dot_general(a, b, TN_DIMS, preferred_element_type=F32)


def _rms(x, g):
    return x * lax.rsqrt(jnp.mean(x * x, axis=-1, keepdims=True) + EPS) * g


def _resident(shape):
    return pl.BlockSpec(shape, lambda *_: (0,) * len(shape), pipeline_mode=pl.Buffered(1))


def _t5_bucket_np(rel):
    half = N_BUCKETS // 2
    max_exact = half // 2
    ret = np.where(rel > 0, half, 0)
    n = np.abs(rel)
    nf = np.maximum(n, 1).astype(np.float32)
    large = max_exact + (np.log(nf / np.float32(max_exact))
                         / np.float32(math.log(MAX_DISTANCE / max_exact))
                         * np.float32(half - max_exact)).astype(np.int32)
    large = np.minimum(large, half - 1)
    return (ret + np.where(n < max_exact, n, large)).astype(np.int32)


FAR_BUCKET = int(_t5_bucket_np(np.array([-2 * TQ]))[0])


def _bias_kernel(rb_ref, bkt_ref, o_ref):
    bk = bkt_ref[0]
    for h in range(A_HEADS):
        acc = jnp.zeros(bk.shape, F32)
        for b in range(N_BUCKETS):
            acc = jnp.where(bk == b, rb_ref[b, h], acc)
        o_ref[0, h] = (acc - rb_ref[FAR_BUCKET, h]) * LOG2E


def _bias_tables(rel_bias):
    kl = np.arange(TQ)[:, None]
    ql = np.arange(TQ)[None, :]
    buckets = np.stack([_t5_bucket_np(kl - ql - TQ), _t5_bucket_np(kl - ql)])
    return pl.pallas_call(
        _bias_kernel,
        grid=(2,),
        in_specs=[pl.BlockSpec(memory_space=pltpu.SMEM),
                  pl.BlockSpec((1, TQ, TQ), lambda t: (t, 0, 0))],
        out_specs=pl.BlockSpec((1, A_HEADS, TQ, TQ), lambda t: (t, 0, 0, 0)),
        out_shape=jax.ShapeDtypeStruct((2, A_HEADS, TQ, TQ), F32),
        name="bias_tables",
    )(rel_bias, jnp.asarray(buckets))


def _in_proj_kernel(x_ref, g_ref, w_ref, wv_ref, a_ref, vt_ref, i_ref, b_ref):
    h = _rms(x_ref[...], g_ref[...]).astype(BF16)
    c0 = 0
    a_ref[:, 0:A_WIDTH] = (_dot(h, w_ref[:, c0:c0 + A_WIDTH]) * (A_HEAD_DIM ** -0.5 * LOG2E)).astype(BF16)
    c0 += A_WIDTH
    a_ref[:, A_WIDTH:2 * A_WIDTH] = _dot(h, w_ref[:, c0:c0 + A_WIDTH]).astype(BF16)
    c0 += A_WIDTH
    vt = _dot_nt(wv_ref[...], h).astype(BF16)
    ones = jnp.ones((VT_HEAD - A_HEAD_DIM, vt.shape[1]), BF16)
    for hd in range(A_HEADS):
        vt_ref[hd * VT_HEAD:hd * VT_HEAD + A_HEAD_DIM, :] = vt[hd * A_HEAD_DIM:(hd + 1) * A_HEAD_DIM, :]
        vt_ref[hd * VT_HEAD + A_HEAD_DIM:(hd + 1) * VT_HEAD, :] = ones
    i_ref[...] = _dot(h, w_ref[:, c0:c0 + IDX_COLS])
    c0 += IDX_COLS
    b_ref[...] = _dot(h, w_ref[:, c0:c0 + B_COLS])


def _pack_w_in(w_in):
    sizes = (A_WIDTH, A_WIDTH, A_WIDTH, IDX_WIDTH, IDX_DIM, IDX_HEADS,
             B_KEY_WIDTH, B_KEY_WIDTH, B_WIDTH, GATE_RANK, B_WIDTH)
    offs = np.cumsum((0,) + sizes)
    qa, ka, va, qi, ki, wi, qb, kb, vb, gd, og = [w_in[:, offs[j]:offs[j + 1]] for j in range(len(sizes))]
    d = w_in.shape[0]
    cols = [qa, ka,
            qi, jnp.pad(jnp.concatenate([ki, wi], axis=1), ((0, 0), (0, LANES - IDX_DIM - IDX_HEADS))),
            qb, kb, vb, og, jnp.pad(gd, ((0, 0), (0, LANES - GATE_RANK)))]
    return jnp.concatenate(cols, axis=1).astype(BF16), va.T.astype(BF16)


def _in_proj(x2d, g, w_packed, wv_t):
    t = x2d.shape[0]
    return pl.pallas_call(
        _in_proj_kernel,
        grid=(t // TM_PROJ,),
        in_specs=[pl.BlockSpec((TM_PROJ, D_MODEL), lambda r: (r, 0)),
                  _resident((1, D_MODEL)),
                  _resident((D_MODEL, W_COLS)),
                  _resident((A_WIDTH, D_MODEL))],
        out_specs=[pl.BlockSpec((TM_PROJ, A_COLS), lambda r: (r, 0)),
                   pl.BlockSpec((VT_ROWS, TM_PROJ), lambda r: (0, r)),
                   pl.BlockSpec((TM_PROJ, IDX_COLS), lambda r: (r, 0)),
                   pl.BlockSpec((TM_PROJ, B_COLS), lambda r: (r, 0))],
        out_shape=[jax.ShapeDtypeStruct((t, A_COLS), BF16),
                   jax.ShapeDtypeStruct((VT_ROWS, t), BF16),
                   jax.ShapeDtypeStruct((t, IDX_COLS), F32),
                   jax.ShapeDtypeStruct((t, B_COLS), F32)],
        compiler_params=pltpu.CompilerParams(dimension_semantics=("parallel",),
                                             vmem_limit_bytes=VMEM_LIMIT),
        name="in_proj",
    )(x2d, g, w_packed, wv_t)


def _rows8(x, op):
    r, c = x.shape
    if r % 32 == 0 and r > 32:
        return op(op(x.reshape(r // 32, 32, c), axis=0).reshape(4, 8, c), axis=0)
    return op(x.reshape(r // 8, 8, c), axis=0)


def _tree_add(parts):
    while len(parts) > 1:
        parts = [a + b for a, b in zip(parts[0::2], parts[1::2])] + ([parts[-1]] if len(parts) % 2 else [])
    return parts[0]


def _colmax(x):
    return jnp.max(_rows8(x, jnp.max), axis=0, keepdims=True)


N_COARSE = 10
N_BISECT = 10


def _dsa_kernel(q_ref, k_ref, vt_ref, iq_ref, iw_ref, ik_ref, bias_ref, o_ref,
                qi_ref, qa_ref, kt_ref, s_ref, sb_ref, mb_ref, m_ref, acc_ref, lg_ref, mo_ref, mn_ref):
    i = pl.program_id(1)
    nk = i + 1
    npair = (nk + 1) // 2
    big = jnp.float32(3e38)
    kk = jnp.float32(TOPK_MAX)

    def blk(kc):
        start = kc * TQ
        return pl.ds(start if isinstance(kc, int) else pl.multiple_of(start, TQ), TQ)

    def blk2(j):
        return pl.ds(pl.multiple_of(j * 2 * TQ, 2 * TQ), 2 * TQ)

    row = lax.broadcasted_iota(jnp.int32, (TQ, TQ), 0)
    col = lax.broadcasted_iota(jnp.int32, (TQ, TQ), 1)
    adm_diag = row < (col // CHUNK + 1) * CHUNK

    lane_i = lax.broadcasted_iota(jnp.int32, (1, IDX_WIDTH), 1)
    iq = iq_ref[...]
    for h in range(IDX_HEADS):
        qi_ref[h] = jnp.where(lane_i // IDX_DIM == h, iq, 0.0).astype(BF16)
    lane_p = lax.broadcasted_iota(jnp.int32, (1, LANES), 1)
    for h in range(A_HEADS):
        q_pair = q_ref[:, (h // 2) * LANES:(h // 2 + 1) * LANES]
        qa_ref[h] = jnp.where((lane_p < A_HEAD_DIM) == (h % 2 == 0), q_pair, jnp.zeros_like(q_pair))
    w_t = iw_ref[...].T * (IDX_DIM ** -0.5 * IDX_HEADS ** -0.5)

    @pl.when(i == 0)
    def _():
        src = lax.broadcasted_iota(jnp.int32, (LANES, IDX_WIDTH), 0)
        dst = lax.broadcasted_iota(jnp.int32, (LANES, IDX_WIDTH), 1)
        spread = jnp.logical_and(src < IDX_DIM, dst % IDX_DIM == src).astype(BF16)
        for kb in range(kt_ref.shape[0] // TQ):
            kt_ref[kb * TQ:(kb + 1) * TQ, :] = _dot(ik_ref[kb * TQ:(kb + 1) * TQ, :].astype(BF16),
                                                    spread).astype(BF16)

    def score_blk(kc, c):
        kic = kt_ref[blk(kc), :]
        acc = jnp.zeros((TQ, TQ), F32)
        for h in range(IDX_HEADS):
            w_h = w_t[IDX_W_LANE + h:IDX_W_LANE + h + 1, :]
            acc = acc + jnp.maximum(_dot_nt(kic, qi_ref[h]), 0.0) * w_h
        adm = jnp.logical_or(kc < i, adm_diag)
        s = jnp.where(adm, acc, NEG)
        s_ref[blk(kc), :] = s
        sb_ref[blk(kc), :] = s.astype(BF16)
        return (jnp.minimum(c[0], _rows8(jnp.where(adm, acc, big), jnp.min)),
                jnp.maximum(c[1], _rows8(s, jnp.max)))

    def score_two(j, c):
        return score_blk(2 * j + 1, score_blk(2 * j, c))

    c0 = lax.fori_loop(0, nk // 2, score_two, (jnp.full((8, TQ), big), jnp.full((8, TQ), -big)))
    lo8, hi8 = lax.cond(nk % 2 == 1, lambda c: score_blk(nk - 1, c), lambda c: c, c0)

    @pl.when(i == 0)
    def _():
        mb_ref[0:TQ, :] = jnp.where(adm_diag, 0.0, NEG)

    @pl.when(i > 0)
    def _():
        @pl.when(nk % 2 == 1)
        def _():
            s_ref[blk(nk), :] = jnp.full((TQ, TQ), NEG, F32)
            sb_ref[blk(nk), :] = jnp.full((TQ, TQ), NEG, BF16)

        lo = jnp.min(lo8, axis=0, keepdims=True)
        hi = jnp.max(hi8, axis=0, keepdims=True)
        q_col = lax.broadcasted_iota(jnp.int32, (1, TQ), 1)
        c_lo = (i * TQ + (q_col // CHUNK + 1) * CHUNK).astype(F32)

        def count_ge(t):
            def body(j, acc):
                return acc + _rows8(jnp.where(s_ref[blk2(j), :] >= t, 1.0, 0.0), jnp.sum)
            return jnp.sum(lax.fori_loop(0, npair, body, jnp.zeros((8, TQ), F32)), axis=0, keepdims=True)

        def count_ge16(t16):
            def body(j, acc):
                x = sb_ref[blk2(j), :].reshape(2 * TQ // 64, 64, TQ)
                hit = jnp.where(x >= t16, jnp.ones_like(x), jnp.zeros_like(x))
                part = _tree_add([hit[k] for k in range(hit.shape[0])])
                return acc + _tree_add([part[16 * k:16 * (k + 1)] for k in range(4)])
            acc = lax.fori_loop(0, npair, body, jnp.zeros((16, TQ), BF16))
            return jnp.sum(acc.astype(F32), axis=0, keepdims=True)

        def coarse(_, c):
            lo, hi, moved = c
            mid = (0.5 * lo + 0.5 * hi).astype(BF16).astype(F32)
            t16 = jnp.tile(jnp.broadcast_to(mid, (16, TQ)).astype(BF16), (4, 1))[None]
            ge = count_ge16(t16) >= kk
            return jnp.where(ge, mid, lo), jnp.where(ge, hi, mid), jnp.where(ge, 1.0, moved)
        lo_c, hi, moved = lax.fori_loop(0, N_COARSE, coarse, (lo, hi, jnp.zeros((1, TQ), F32)))
        lo = jnp.where(moved > 0.5, lo_c - jnp.abs(lo_c) * 2.0 ** -7 - 1e-37, lo)
        c_lo = jnp.where(moved > 0.5, big, c_lo)

        def bisect(_, c):
            lo, hi, c_lo = c
            mid = 0.5 * lo + 0.5 * hi
            cnt = count_ge(mid)
            ge = cnt >= kk
            return jnp.where(ge, mid, lo), jnp.where(ge, hi, mid), jnp.where(ge, cnt, c_lo)
        lo, hi, c_lo = lax.fori_loop(0, N_BISECT, bisect, (lo, hi, c_lo))
        def finish(ls, lo, c_lo):
            width = ls.stop - ls.start
            unresolved = jnp.max(c_lo)

            @pl.when(unresolved <= kk)
            def _():
                def select(kc, carry):
                    mb_ref[blk(kc), ls] = jnp.where(s_ref[blk(kc), ls] >= lo, 0.0, NEG)
                    return carry
                lax.fori_loop(0, nk, select, 0)

            @pl.when(unresolved > kk)
            def _():
                def min_ge(j, acc):
                    s = s_ref[blk2(j), ls]
                    return jnp.minimum(acc, _rows8(jnp.where(s >= lo, s, big), jnp.min))
                v0 = jnp.min(lax.fori_loop(0, npair, min_ge, jnp.full((8, width), big)), axis=0, keepdims=True)

                def above(v):
                    def body(j, c):
                        s = s_ref[blk2(j), ls]
                        gt = s > v
                        return (c[0] + _rows8(jnp.where(gt, 1.0, 0.0), jnp.sum),
                                jnp.minimum(c[1], _rows8(jnp.where(gt, s, big), jnp.min)))
                    c8, n8 = lax.fori_loop(0, npair, body,
                                           (jnp.zeros((8, width), F32), jnp.full((8, width), big)))
                    return jnp.sum(c8, axis=0, keepdims=True), jnp.min(n8, axis=0, keepdims=True)

                def peel_cond(c):
                    return c[2] > 0.5

                def peel(c):
                    v, _, _ = c
                    c_gt, nxt = above(v)
                    done = c_gt < kk
                    v_new = jnp.where(done, v, nxt)
                    pending = jnp.max(jnp.where(done, 0.0, 1.0))
                    return v_new, c_gt, pending
                thr, c_gt, _ = lax.while_loop(peel_cond, peel,
                                              (v0, jnp.zeros((1, width), F32), jnp.float32(1.0)))
                need = kk - c_gt

                ltri = (row >= col).astype(BF16)

                def select_blk(kc, carry):
                    s = s_ref[blk(kc), ls]
                    tie = s == thr
                    pref = _dot(ltri, jnp.where(tie, 1.0, 0.0).astype(BF16))
                    rank = jnp.where(tie, carry + pref, big)
                    mb_ref[blk(kc), ls] = jnp.where(s > thr, 0.0, jnp.where(rank <= need, 0.0, NEG))
                    return carry + pref[TQ - 1:TQ, :]

                def select_two(j, carry):
                    return select_blk(2 * j + 1, select_blk(2 * j, carry))
                lax.fori_loop(0, npair, select_two, jnp.zeros((1, width), F32))

        for half in range(TQ // LANES):
            ls = slice(half * LANES, (half + 1) * LANES)
            finish(ls, lo[:, ls], c_lo[:, ls])

    m_ref[...] = jnp.full(m_ref.shape, -big, F32)
    acc_ref[...] = jnp.zeros(acc_ref.shape, F32)

    def qk_stage(kc, slot, near):
        mb = mb_ref[blk(kc), :]
        for h in range(A_HEADS):
            cs = slice((h // 2) * LANES, (h // 2 + 1) * LANES)
            l = _dot_nt(k_ref[blk(kc), cs], qa_ref[h]) + mb
            if near:
                l = l + bias_ref[kc - i + 1, h]
            lg_ref[slot, h] = l
            m_old = m_ref[h, 0:1, :]
            m_new = jnp.maximum(m_old, _colmax(l))
            mo_ref[slot, h, 0:1, :] = m_old
            mn_ref[slot, h, 0:1, :] = m_new
            m_ref[h, 0:1, :] = m_new

    def pv_stage(kc, slot):
        for h in range(A_HEADS):
            m_new = mn_ref[slot, h, 0:1, :]
            p = jnp.exp2(lg_ref[slot, h] - m_new).astype(BF16)
            acc_ref[h] = (acc_ref[h] * jnp.exp2(mo_ref[slot, h, 0:1, :] - m_new)
                          + _dot(vt_ref[h * VT_HEAD:(h + 1) * VT_HEAD, blk(kc)], p))

    n_far = jnp.maximum(i - 1, 0)

    @pl.when(n_far > 0)
    def _():
        qk_stage(0, 0, False)
        n_body = (n_far - 1) // 2

        def body(j, carry):
            qk_stage(2 * j + 1, 1, False)
            pv_stage(2 * j, 0)
            qk_stage(2 * j + 2, 0, False)
            pv_stage(2 * j + 1, 1)
            return carry
        lax.fori_loop(0, n_body, body, 0)
        last = 2 * n_body

        @pl.when(n_far - last == 2)
        def _():
            qk_stage(last + 1, 1, False)
        pv_stage(last, 0)

        @pl.when(n_far - last == 2)
        def _():
            pv_stage(last + 1, 1)

    @pl.when(i > 0)
    def _():
        qk_stage(i - 1, 0, True)
        qk_stage(i, 1, True)
        pv_stage(i - 1, 0)
        pv_stage(i, 1)

    @pl.when(i == 0)
    def _():
        qk_stage(0, 0, True)
        pv_stage(0, 0)

    outs = [acc_ref[h, 0:A_HEAD_DIM, :] / acc_ref[h, A_HEAD_DIM:A_HEAD_DIM + 1, :] for h in range(A_HEADS)]
    o_ref[...] = jnp.concatenate(outs, axis=0).T.astype(o_ref.dtype)


def _dsa(a_out, vt_out, idx_out, bias_tab, bsz, seq):
    nq = seq // TQ
    return pl.pallas_call(
        _dsa_kernel,
        grid=(bsz, nq),
        in_specs=[pl.BlockSpec((TQ, A_WIDTH), lambda b, i: (b * nq + i, 0)),
                  pl.BlockSpec((seq, A_WIDTH), lambda b, i: (b, 1)),
                  pl.BlockSpec((VT_ROWS, seq), lambda b, i: (0, b)),
                  pl.BlockSpec((TQ, IDX_WIDTH), lambda b, i: (b * nq + i, 0)),
                  pl.BlockSpec((TQ, LANES), lambda b, i: (b * nq + i, IDX_WIDTH // LANES)),
                  pl.BlockSpec((seq, LANES), lambda b, i: (b, IDX_WIDTH // LANES)),
                  _resident((2, A_HEADS, TQ, TQ))],
        out_specs=pl.BlockSpec((TQ, A_WIDTH), lambda b, i: (b * nq + i, 0)),
        out_shape=jax.ShapeDtypeStruct((bsz * seq, A_WIDTH), BF16),
        scratch_shapes=[pltpu.VMEM((IDX_HEADS, TQ, IDX_WIDTH), BF16),
                        pltpu.VMEM((A_HEADS, TQ, LANES), BF16),
                        pltpu.VMEM((seq, IDX_WIDTH), BF16),
                        pltpu.VMEM((seq, TQ), F32),
                        pltpu.VMEM((seq, TQ), BF16),
                        pltpu.VMEM((seq, TQ), F32),
                        pltpu.VMEM((A_HEADS, 8, TQ), F32),
                        pltpu.VMEM((A_HEADS, VT_HEAD, TQ), F32),
                        pltpu.VMEM((2, A_HEADS, TQ, TQ), F32),
                        pltpu.VMEM((2, A_HEADS, 8, TQ), F32),
                        pltpu.VMEM((2, A_HEADS, 8, TQ), F32)],
        compiler_params=pltpu.CompilerParams(dimension_semantics=("parallel", "arbitrary"),
                                             vmem_limit_bytes=VMEM_LIMIT),
        name="dsa",
    )(a_out, a_out, vt_out, idx_out, idx_out, idx_out, bias_tab)


def _gla_kernel(qk_ref, v_ref, og_ref, gd_ref, wgu_ref, bg_ref, gn_ref, mcum_ref, mtot_ref,
                o_ref, st_ref, ob_ref):
    @pl.when(pl.program_id(1) == 0)
    def _():
        st_ref[...] = jnp.zeros_like(st_ref)

    z = _dot(gd_ref[...].astype(BF16), wgu_ref[...]) + bg_ref[...]
    g = -(jnp.maximum(-z, 0.0) + jnp.log(1.0 + jnp.exp(-jnp.abs(z)))) * (1.0 / GATE_NORMALIZER)
    g_hi = g.astype(BF16)
    g_lo = (g - g_hi.astype(F32)).astype(BF16)
    mcum = mcum_ref[...]
    mtot = mtot_ref[...]
    bcum = _dot(mcum, g_hi) + _dot(mcum, g_lo)
    btot = _dot(mtot, g_hi) + _dot(mtot, g_lo)

    q = qk_ref[:, 0:B_KEY_WIDTH]
    k = qk_ref[:, B_KEY_WIDTH:2 * B_KEY_WIDTH]
    q_in = q * B_KEY_DIM ** -0.5 * jnp.exp(bcum)
    k_in = (k * jnp.exp(-bcum)).astype(BF16)
    k_dec = k * jnp.exp(btot - bcum)
    e_tot = jnp.exp(btot)

    lane = lax.broadcasted_iota(jnp.int32, (1, LANES), 1)
    halves = (lane < B_KEY_DIM, lane >= B_KEY_DIM)
    r_i = lax.broadcasted_iota(jnp.int32, (CHUNK, CHUNK), 0)
    c_i = lax.broadcasted_iota(jnp.int32, (CHUNK, CHUNK), 1)
    causal = r_i >= c_i

    n_chunks = TS_GLA // CHUNK
    rows = [slice(c * CHUNK, (c + 1) * CHUNK) for c in range(n_chunks)]
    qm, kdm, vh, att, upd = {}, {}, {}, {}, {}
    for pair in range(B_HEADS // 2):
        cs = slice(pair * LANES, (pair + 1) * LANES)
        for sub in range(2):
            h = 2 * pair + sub
            qm[h] = jnp.where(halves[sub], q_in[:, cs], 0.0).astype(BF16)
            kdm[h] = jnp.where(halves[sub], k_dec[:, cs], 0.0).astype(BF16)
            vh[h] = v_ref[:, h * B_VAL_DIM:(h + 1) * B_VAL_DIM].astype(BF16)
    for h in range(B_HEADS):
        ki = k_in[:, (h // 2) * LANES:(h // 2 + 1) * LANES]
        for c in range(n_chunks):
            att[h, c] = jnp.where(causal, _dot_nt(qm[h][rows[c]], ki[rows[c]]), 0.0).astype(BF16)
    for pair in range(B_HEADS // 2):
        for c in range(n_chunks):
            upd[pair, c] = (_dot_tn(vh[2 * pair][rows[c]], kdm[2 * pair][rows[c]])
                            + _dot_tn(vh[2 * pair + 1][rows[c]], kdm[2 * pair + 1][rows[c]]))
    for h in range(B_HEADS):
        for c in range(n_chunks):
            ob_ref[rows[c], h * B_VAL_DIM:(h + 1) * B_VAL_DIM] = _dot(att[h, c], vh[h][rows[c]])
    for pair in range(B_HEADS // 2):
        cs = slice(pair * LANES, (pair + 1) * LANES)
        state = st_ref[pair]
        for c in range(n_chunks):
            sb = state.astype(BF16)
            for h in (2 * pair, 2 * pair + 1):
                ob_ref[rows[c], h * B_VAL_DIM:(h + 1) * B_VAL_DIM] += _dot_nt(qm[h][rows[c]], sb)
            state = state * e_tot[c * CHUNK:c * CHUNK + 1, cs] + upd[pair, c]
        st_ref[pair] = state

    og = og_ref[...]
    gate = og * jax.nn.sigmoid(og)
    gn = gn_ref[...]
    for h in range(B_HEADS):
        cs = slice(h * B_VAL_DIM, (h + 1) * B_VAL_DIM)
        o_ref[:, cs] = (_rms(ob_ref[:, cs], gn[:, cs]) * gate[:, cs]).astype(o_ref.dtype)


def _gla(b_out, w_gate_up, b_gate, gla_norm, bsz, seq):
    ns = seq // TS_GLA
    idx = np.arange(TS_GLA)
    same = (idx[:, None] // CHUNK) == (idx[None, :] // CHUNK)
    mcum = jnp.asarray(same & (idx[None, :] <= idx[:, None]), BF16)
    mtot = jnp.asarray(same, BF16)
    wgu = jnp.pad(w_gate_up, ((0, LANES - GATE_RANK), (0, 0))).astype(BF16)
    return pl.pallas_call(
        _gla_kernel,
        grid=(bsz, ns),
        in_specs=[pl.BlockSpec((TS_GLA, 2 * B_KEY_WIDTH), lambda b, j: (b * ns + j, 0)),
                  pl.BlockSpec((TS_GLA, B_WIDTH), lambda b, j: (b * ns + j, 1)),
                  pl.BlockSpec((TS_GLA, B_WIDTH), lambda b, j: (b * ns + j, 2)),
                  pl.BlockSpec((TS_GLA, LANES), lambda b, j: (b * ns + j, (2 * B_KEY_WIDTH + 2 * B_WIDTH) // LANES)),
                  _resident((LANES, B_KEY_WIDTH)),
                  _resident((1, B_KEY_WIDTH)),
                  _resident((1, B_WIDTH)),
                  _resident((TS_GLA, TS_GLA)),
                  _resident((TS_GLA, TS_GLA))],
        out_specs=pl.BlockSpec((TS_GLA, B_WIDTH), lambda b, j: (b * ns + j, 0)),
        out_shape=jax.ShapeDtypeStruct((bsz * seq, B_WIDTH), BF16),
        scratch_shapes=[pltpu.VMEM((B_HEADS // 2, B_VAL_DIM, LANES), F32),
                        pltpu.VMEM((TS_GLA, B_WIDTH), F32)],
        compiler_params=pltpu.CompilerParams(dimension_semantics=("parallel", "arbitrary"),
                                             vmem_limit_bytes=VMEM_LIMIT),
        name="gla",
    )(b_out, b_out, b_out, b_out, wgu, b_gate.reshape(1, -1), gla_norm.reshape(1, -1), mcum, mtot)


def _out_ffn_kernel(x_ref, oa_ref, ob_ref, wo_ref, nf_ref, wg_ref, wu_ref, wd_ref, nl_ref, o_ref):
    mix = jnp.concatenate([oa_ref[...], ob_ref[...]], axis=1)
    x1 = x_ref[...] + _dot(mix, wo_ref[...])
    h = _rms(x1, nf_ref[...]).astype(BF16)
    g = _dot(h, wg_ref[...])
    u = _dot(h, wu_ref[...])
    a = (g * jax.nn.sigmoid(g) * u).astype(BF16)
    o_ref[...] = _rms(x1 + _dot(a, wd_ref[...]), nl_ref[...])


def _out_ffn(x2d, o_a, o_b, w_out, norm_ffn, w_gate, w_up, w_down, norm_final):
    t = x2d.shape[0]
    row = lambda r: (r, 0)
    return pl.pallas_call(
        _out_ffn_kernel,
        grid=(t // TM_FFN,),
        in_specs=[pl.BlockSpec((TM_FFN, D_MODEL), row),
                  pl.BlockSpec((TM_FFN, A_WIDTH), row),
                  pl.BlockSpec((TM_FFN, B_WIDTH), row),
                  _resident((A_WIDTH + B_WIDTH, D_MODEL)),
                  _resident((1, D_MODEL)),
                  _resident((D_MODEL, D_FF)),
                  _resident((D_MODEL, D_FF)),
                  _resident((D_FF, D_MODEL)),
                  _resident((1, D_MODEL))],
        out_specs=pl.BlockSpec((TM_FFN, D_MODEL), row),
        out_shape=jax.ShapeDtypeStruct((t, D_MODEL), F32),
        compiler_params=pltpu.CompilerParams(dimension_semantics=("parallel",),
                                             vmem_limit_bytes=VMEM_LIMIT),
        name="out_ffn",
    )(x2d, o_a, o_b, w_out, norm_ffn, w_gate, w_up, w_down, norm_final)


def kernel(x, norm_mix, w_in, w_gate_up, b_gate, gla_norm, w_out, rel_bias,
           norm_ffn, w_ffn_gate, w_ffn_up, w_ffn_down, norm_final):
    bsz, seq, d = x.shape
    assert w_in.shape[0] == 1, "fused final norm assumes a single layer"
    assert d == D_MODEL and seq % TS_GLA == 0 and seq % TQ == 0 and seq // 4 >= TOPK_MAX
    bias_tab = _bias_tables(rel_bias)
    x2d = x.reshape(bsz * seq, d)
    a_out, vt_out, idx_out, b_out = _in_proj(x2d, norm_mix[0].reshape(1, d), *_pack_w_in(w_in[0]))
    o_a = _dsa(a_out, vt_out, idx_out, bias_tab, bsz, seq)
    o_b = _gla(b_out, w_gate_up[0], b_gate[0], gla_norm[0], bsz, seq)
    out = _out_ffn(x2d, o_a, o_b, w_out[0].astype(BF16), norm_ffn[0].reshape(1, d),
                   w_ffn_gate[0].astype(BF16), w_ffn_up[0].astype(BF16),
                   w_ffn_down[0].astype(BF16), norm_final.reshape(1, d))
    return out.reshape(bsz, seq, d)
```

```python
import math

import numpy as np
import jax
import jax.numpy as jnp
from jax import lax
from jax.experimental import pallas as pl
from jax.experimental.pallas import tpu as pltpu

F32 = jnp.float32
BF16 = jnp.bfloat16

D_MODEL = 1024
CHUNK = 64
A_HEADS = 8
A_HEAD_DIM = 64
A_WIDTH = A_HEADS * A_HEAD_DIM
IDX_HEADS = 8
IDX_DIM = 32
IDX_WIDTH = IDX_HEADS * IDX_DIM
TOPK_MAX = 256
B_HEADS = 4
B_KEY_DIM = 64
B_VAL_DIM = 128
B_WIDTH = B_HEADS * B_VAL_DIM
B_KEY_WIDTH = B_HEADS * B_KEY_DIM
GATE_RANK = 16
GATE_NORMALIZER = 16.0
N_BUCKETS = 32
MAX_DISTANCE = 128
D_FF = 2816
EPS = 1e-6
NEG = -1e30
LOG2E = math.log2(math.e)

LANES = 128
VMEM_LIMIT = 56 * 1024 * 1024

TM_PROJ = 512
TQ = 256
TS_GLA = 512
TM_FFN = 512

A_COLS = 2 * A_WIDTH
VT_HEAD = A_HEAD_DIM + 16
VT_ROWS = A_HEADS * VT_HEAD
IDX_COLS = IDX_WIDTH + LANES
IDX_W_LANE = IDX_DIM
B_COLS = 2 * B_KEY_WIDTH + B_WIDTH + B_WIDTH + LANES
W_COLS = A_COLS + IDX_COLS + B_COLS

NT_DIMS = (((1,), (1,)), ((), ()))
TN_DIMS = (((0,), (0,)), ((), ()))


def _dot(a, b):
    return jnp.dot(a, b, preferred_element_type=F32)


def _dot_nt(a, b):
    return lax.dot_general(a, b, NT_DIMS, preferred_element_type=F32)


def _dot_tn(a, b):
    return lax.dot_general(a, b, TN_DIMS, preferred_element_type=F32)


def _rms(x, g):
    return x * lax.rsqrt(jnp.mean(x * x, axis=-1, keepdims=True) + EPS) * g


def _resident(shape):
    return pl.BlockSpec(shape, lambda *_: (0,) * len(shape), pipeline_mode=pl.Buffered(1))


def _t5_bucket_np(rel):
    half = N_BUCKETS // 2
    max_exact = half // 2
    ret = np.where(rel > 0, half, 0)
    n = np.abs(rel)
    nf = np.maximum(n, 1).astype(np.float32)
    large = max_exact + (np.log(nf / np.float32(max_exact))
                         / np.float32(math.log(MAX_DISTANCE / max_exact))
                         * np.float32(half - max_exact)).astype(np.int32)
    large = np.minimum(large, half - 1)
    return (ret + np.where(n < max_exact, n, large)).astype(np.int32)


FAR_BUCKET = int(_t5_bucket_np(np.array([-2 * TQ]))[0])


def _bias_kernel(rb_ref, bkt_ref, o_ref):
    bk = bkt_ref[0]
    for h in range(A_HEADS):
        acc = jnp.zeros(bk.shape, F32)
        for b in range(N_BUCKETS):
            acc = jnp.where(bk == b, rb_ref[b, h], acc)
        o_ref[0, h] = (acc - rb_ref[FAR_BUCKET, h]) * LOG2E


def _bias_tables(rel_bias):
    kl = np.arange(TQ)[:, None]
    ql = np.arange(TQ)[None, :]
    buckets = np.stack([_t5_bucket_np(kl - ql - TQ), _t5_bucket_np(kl - ql)])
    return pl.pallas_call(
        _bias_kernel,
        grid=(2,),
        in_specs=[pl.BlockSpec(memory_space=pltpu.SMEM),
                  pl.BlockSpec((1, TQ, TQ), lambda t: (t, 0, 0))],
        out_specs=pl.BlockSpec((1, A_HEADS, TQ, TQ), lambda t: (t, 0, 0, 0)),
        out_shape=jax.ShapeDtypeStruct((2, A_HEADS, TQ, TQ), F32),
        name="bias_tables",
    )(rel_bias, jnp.asarray(buckets))


def _in_proj_kernel(x_ref, g_ref, w_ref, wv_ref, a_ref, vt_ref, i_ref, b_ref):
    h = _rms(x_ref[...], g_ref[...]).astype(BF16)
    c0 = 0
    a_ref[:, 0:A_WIDTH] = (_dot(h, w_ref[:, c0:c0 + A_WIDTH]) * (A_HEAD_DIM ** -0.5 * LOG2E)).astype(BF16)
    c0 += A_WIDTH
    a_ref[:, A_WIDTH:2 * A_WIDTH] = _dot(h, w_ref[:, c0:c0 + A_WIDTH]).astype(BF16)
    c0 += A_WIDTH
    vt = _dot_nt(wv_ref[...], h).astype(BF16)
    ones = jnp.ones((VT_HEAD - A_HEAD_DIM, vt.shape[1]), BF16)
    for hd in range(A_HEADS):
        vt_ref[hd * VT_HEAD:hd * VT_HEAD + A_HEAD_DIM, :] = vt[hd * A_HEAD_DIM:(hd + 1) * A_HEAD_DIM, :]
        vt_ref[hd * VT_HEAD + A_HEAD_DIM:(hd + 1) * VT_HEAD, :] = ones
    i_ref[...] = _dot(h, w_ref[:, c0:c0 + IDX_COLS])
    c0 += IDX_COLS
    b_ref[...] = _dot(h, w_ref[:, c0:c0 + B_COLS])


def _pack_w_in(w_in):
    sizes = (A_WIDTH, A_WIDTH, A_WIDTH, IDX_WIDTH, IDX_DIM, IDX_HEADS,
             B_KEY_WIDTH, B_KEY_WIDTH, B_WIDTH, GATE_RANK, B_WIDTH)
    offs = np.cumsum((0,) + sizes)
    qa, ka, va, qi, ki, wi, qb, kb, vb, gd, og = [w_in[:, offs[j]:offs[j + 1]] for j in range(len(sizes))]
    d = w_in.shape[0]
    cols = [qa, ka,
            qi, jnp.pad(jnp.concatenate([ki, wi], axis=1), ((0, 0), (0, LANES - IDX_DIM - IDX_HEADS))),
            qb, kb, vb, og, jnp.pad(gd, ((0, 0), (0, LANES - GATE_RANK)))]
    return jnp.concatenate(cols, axis=1).astype(BF16), va.T.astype(BF16)


def _in_proj(x2d, g, w_packed, wv_t):
    t = x2d.shape[0]
    return pl.pallas_call(
        _in_proj_kernel,
        grid=(t // TM_PROJ,),
        in_specs=[pl.BlockSpec((TM_PROJ, D_MODEL), lambda r: (r, 0)),
                  _resident((1, D_MODEL)),
                  _resident((D_MODEL, W_COLS)),
                  _resident((A_WIDTH, D_MODEL))],
        out_specs=[pl.BlockSpec((TM_PROJ, A_COLS), lambda r: (r, 0)),
                   pl.BlockSpec((VT_ROWS, TM_PROJ), lambda r: (0, r)),
                   pl.BlockSpec((TM_PROJ, IDX_COLS), lambda r: (r, 0)),
                   pl.BlockSpec((TM_PROJ, B_COLS), lambda r: (r, 0))],
        out_shape=[jax.ShapeDtypeStruct((t, A_COLS), BF16),
                   jax.ShapeDtypeStruct((VT_ROWS, t), BF16),
                   jax.ShapeDtypeStruct((t, IDX_COLS), F32),
                   jax.ShapeDtypeStruct((t, B_COLS), F32)],
        compiler_params=pltpu.CompilerParams(dimension_semantics=("parallel",),
                                             vmem_limit_bytes=VMEM_LIMIT),
        name="in_proj",
    )(x2d, g, w_packed, wv_t)


def _rows8(x, op):
    r, c = x.shape
    if r % 32 == 0 and r > 32:
        return op(op(x.reshape(r // 32, 32, c), axis=0).reshape(4, 8, c), axis=0)
    return op(x.reshape(r // 8, 8, c), axis=0)


def _tree_add(parts):
    while len(parts) > 1:
        parts = [a + b for a, b in zip(parts[0::2], parts[1::2])] + ([parts[-1]] if len(parts) % 2 else [])
    return parts[0]


def _colmax(x):
    return jnp.max(_rows8(x, jnp.max), axis=0, keepdims=True)


N_COARSE = 10
N_BISECT = 10


def _dsa_kernel(q_ref, k_ref, vt_ref, iq_ref, iw_ref, ik_ref, bias_ref, o_ref,
                qi_ref, qa_ref, kt_ref, s_ref, sb_ref, mb_ref, m_ref, acc_ref, lg_ref, mo_ref, mn_ref):
    i = pl.program_id(1)
    nk = i + 1
    npair = (nk + 1) // 2
    big = jnp.float32(3e38)
    kk = jnp.float32(TOPK_MAX)

    def blk(kc):
        start = kc * TQ
        return pl.ds(start if isinstance(kc, int) else pl.multiple_of(start, TQ), TQ)

    def blk2(j):
        return pl.ds(pl.multiple_of(j * 2 * TQ, 2 * TQ), 2 * TQ)

    row = lax.broadcasted_iota(jnp.int32, (TQ, TQ), 0)
    col = lax.broadcasted_iota(jnp.int32, (TQ, TQ), 1)
    adm_diag = row < (col // CHUNK + 1) * CHUNK

    lane_i = lax.broadcasted_iota(jnp.int32, (1, IDX_WIDTH), 1)
    iq = iq_ref[...]
    for h in range(IDX_HEADS):
        qi_ref[h] = jnp.where(lane_i // IDX_DIM == h, iq, 0.0).astype(BF16)
    lane_p = lax.broadcasted_iota(jnp.int32, (1, LANES), 1)
    for h in range(A_HEADS):
        q_pair = q_ref[:, (h // 2) * LANES:(h // 2 + 1) * LANES]
        qa_ref[h] = jnp.where((lane_p < A_HEAD_DIM) == (h % 2 == 0), q_pair, jnp.zeros_like(q_pair))
    w_t = iw_ref[...].T * (IDX_DIM ** -0.5 * IDX_HEADS ** -0.5)

    @pl.when(i == 0)
    def _():
        src = lax.broadcasted_iota(jnp.int32, (LANES, IDX_WIDTH), 0)
        dst = lax.broadcasted_iota(jnp.int32, (LANES, IDX_WIDTH), 1)
        spread = jnp.logical_and(src < IDX_DIM, dst % IDX_DIM == src).astype(BF16)
        for kb in range(kt_ref.shape[0] // TQ):
            kt_ref[kb * TQ:(kb + 1) * TQ, :] = _dot(ik_ref[kb * TQ:(kb + 1) * TQ, :].astype(BF16),
                                                    spread).astype(BF16)

    def score_blk(kc, c):
        kic = kt_ref[blk(kc), :]
        acc = jnp.zeros((TQ, TQ), F32)
        for h in range(IDX_HEADS):
            w_h = w_t[IDX_W_LANE + h:IDX_W_LANE + h + 1, :]
            acc = acc + jnp.maximum(_dot_nt(kic, qi_ref[h]), 0.0) * w_h
        adm = jnp.logical_or(kc < i, adm_diag)
        s = jnp.where(adm, acc, NEG)
        s_ref[blk(kc), :] = s
        sb_ref[blk(kc), :] = s.astype(BF16)
        return (jnp.minimum(c[0], _rows8(jnp.where(adm, acc, big), jnp.min)),
                jnp.maximum(c[1], _rows8(s, jnp.max)))

    def score_four(j, c):
        for u in range(4):
            c = score_blk(4 * j + u, c)
        return c

    c0 = lax.fori_loop(0, nk // 4, score_four, (jnp.full((8, TQ), big), jnp.full((8, TQ), -big)))
    done4 = 4 * (nk // 4)
    c0 = lax.cond(nk - done4 >= 2, lambda c: score_blk(done4 + 1, score_blk(done4, c)), lambda c: c, c0)
    lo8, hi8 = lax.cond(nk % 2 == 1, lambda c: score_blk(nk - 1, c), lambda c: c, c0)

    @pl.when(i == 0)
    def _():
        mb_ref[0:TQ, :] = jnp.where(adm_diag, 0.0, NEG)

    @pl.when(i > 0)
    def _():
        @pl.when(nk % 2 == 1)
        def _():
            s_ref[blk(nk), :] = jnp.full((TQ, TQ), NEG, F32)
            sb_ref[blk(nk), :] = jnp.full((TQ, TQ), NEG, BF16)

        lo = jnp.min(lo8, axis=0, keepdims=True)
        hi = jnp.max(hi8, axis=0, keepdims=True)
        q_col = lax.broadcasted_iota(jnp.int32, (1, TQ), 1)
        c_lo = (i * TQ + (q_col // CHUNK + 1) * CHUNK).astype(F32)

        def count_ge(t):
            def body(j, acc):
                return acc + _rows8(jnp.where(s_ref[blk2(j), :] >= t, 1.0, 0.0), jnp.sum)
            return jnp.sum(lax.fori_loop(0, npair, body, jnp.zeros((8, TQ), F32)), axis=0, keepdims=True)

        def count_ge16(t16):
            def body(j, acc):
                x = sb_ref[blk2(j), :].reshape(2 * TQ // 64, 64, TQ)
                hit = jnp.where(x >= t16, jnp.ones_like(x), jnp.zeros_like(x))
                part = _tree_add([hit[k] for k in range(hit.shape[0])])
                return acc + _tree_add([part[16 * k:16 * (k + 1)] for k in range(4)])
            acc = lax.fori_loop(0, npair, body, jnp.zeros((16, TQ), BF16))
            return jnp.sum(acc.astype(F32), axis=0, keepdims=True)

        def coarse(_, c):
            lo, hi, moved = c
            mid = (0.5 * lo + 0.5 * hi).astype(BF16).astype(F32)
            t16 = jnp.tile(jnp.broadcast_to(mid, (16, TQ)).astype(BF16), (4, 1))[None]
            ge = count_ge16(t16) >= kk
            return jnp.where(ge, mid, lo), jnp.where(ge, hi, mid), jnp.where(ge, 1.0, moved)
        lo_c, hi, moved = lax.fori_loop(0, N_COARSE, coarse, (lo, hi, jnp.zeros((1, TQ), F32)))
        lo = jnp.where(moved > 0.5, lo_c - jnp.abs(lo_c) * 2.0 ** -7 - 1e-37, lo)
        c_lo = jnp.where(moved > 0.5, big, c_lo)

        def bisect(_, c):
            lo, hi, c_lo = c
            mid = 0.5 * lo + 0.5 * hi
            cnt = count_ge(mid)
            ge = cnt >= kk
            return jnp.where(ge, mid, lo), jnp.where(ge, hi, mid), jnp.where(ge, cnt, c_lo)
        lo, hi, c_lo = lax.fori_loop(0, N_BISECT, bisect, (lo, hi, c_lo))
        def finish(ls, lo, c_lo):
            width = ls.stop - ls.start
            unresolved = jnp.max(c_lo)

            @pl.when(unresolved <= kk)
            def _():
                def select(kc, carry):
                    mb_ref[blk(kc), ls] = jnp.where(s_ref[blk(kc), ls] >= lo, 0.0, NEG)
                    return carry
                lax.fori_loop(0, nk, select, 0)

            @pl.when(unresolved > kk)
            def _():
                def min_ge(j, acc):
                    s = s_ref[blk2(j), ls]
                    return jnp.minimum(acc, _rows8(jnp.where(s >= lo, s, big), jnp.min))
                v0 = jnp.min(lax.fori_loop(0, npair, min_ge, jnp.full((8, width), big)), axis=0, keepdims=True)

                def above(v):
                    def body(j, c):
                        s = s_ref[blk2(j), ls]
                        gt = s > v
                        return (c[0] + _rows8(jnp.where(gt, 1.0, 0.0), jnp.sum),
                                jnp.minimum(c[1], _rows8(jnp.where(gt, s, big), jnp.min)))
                    c8, n8 = lax.fori_loop(0, npair, body,
                                           (jnp.zeros((8, width), F32), jnp.full((8, width), big)))
                    return jnp.sum(c8, axis=0, keepdims=True), jnp.min(n8, axis=0, keepdims=True)

                def peel_cond(c):
                    return c[2] > 0.5

                def peel(c):
                    v, _, _ = c
                    c_gt, nxt = above(v)
                    done = c_gt < kk
                    v_new = jnp.where(done, v, nxt)
                    pending = jnp.max(jnp.where(done, 0.0, 1.0))
                    return v_new, c_gt, pending
                thr, c_gt, _ = lax.while_loop(peel_cond, peel,
                                              (v0, jnp.zeros((1, width), F32), jnp.float32(1.0)))
                need = kk - c_gt

                ltri = (row >= col).astype(BF16)

                def select_blk(kc, carry):
                    s = s_ref[blk(kc), ls]
                    tie = s == thr
                    pref = _dot(ltri, jnp.where(tie, 1.0, 0.0).astype(BF16))
                    rank = jnp.where(tie, carry + pref, big)
                    mb_ref[blk(kc), ls] = jnp.where(s > thr, 0.0, jnp.where(rank <= need, 0.0, NEG))
                    return carry + pref[TQ - 1:TQ, :]

                def select_two(j, carry):
                    return select_blk(2 * j + 1, select_blk(2 * j, carry))
                lax.fori_loop(0, npair, select_two, jnp.zeros((1, width), F32))

        for half in range(TQ // LANES):
            ls = slice(half * LANES, (half + 1) * LANES)
            finish(ls, lo[:, ls], c_lo[:, ls])

    m_ref[...] = jnp.full(m_ref.shape, -big, F32)
    acc_ref[...] = jnp.zeros(acc_ref.shape, F32)

    def qk_stage(kc, slot, near):
        mb = mb_ref[blk(kc), :]
        for h in range(A_HEADS):
            cs = slice((h // 2) * LANES, (h // 2 + 1) * LANES)
            l = _dot_nt(k_ref[blk(kc), cs], qa_ref[h]) + mb
            if near:
                l = l + bias_ref[kc - i + 1, h]
            lg_ref[slot, h] = l
            m_old = m_ref[h, 0:1, :]
            m_new = jnp.maximum(m_old, _colmax(l))
            mo_ref[slot, h, 0:1, :] = m_old
            mn_ref[slot, h, 0:1, :] = m_new
            m_ref[h, 0:1, :] = m_new

    def pv_stage(kc, slot):
        for h in range(A_HEADS):
            m_new = mn_ref[slot, h, 0:1, :]
            p = jnp.exp2(lg_ref[slot, h] - m_new).astype(BF16)
            acc_ref[h] = (acc_ref[h] * jnp.exp2(mo_ref[slot, h, 0:1, :] - m_new)
                          + _dot(vt_ref[h * VT_HEAD:(h + 1) * VT_HEAD, blk(kc)], p))

    n_far = jnp.maximum(i - 1, 0)

    @pl.when(n_far > 0)
    def _():
        qk_stage(0, 0, False)
        n_body = (n_far - 1) // 2

        def body(j, carry):
            qk_stage(2 * j + 1, 1, False)
            pv_stage(2 * j, 0)
            qk_stage(2 * j + 2, 0, False)
            pv_stage(2 * j + 1, 1)
            return carry
        lax.fori_loop(0, n_body, body, 0)
        last = 2 * n_body

        @pl.when(n_far - last == 2)
        def _():
            qk_stage(last + 1, 1, False)
        pv_stage(last, 0)

        @pl.when(n_far - last == 2)
        def _():
            pv_stage(last + 1, 1)

    @pl.when(i > 0)
    def _():
        qk_stage(i - 1, 0, True)
        qk_stage(i, 1, True)
        pv_stage(i - 1, 0)
        pv_stage(i, 1)

    @pl.when(i == 0)
    def _():
        qk_stage(0, 0, True)
        pv_stage(0, 0)

    outs = [acc_ref[h, 0:A_HEAD_DIM, :] / acc_ref[h, A_HEAD_DIM:A_HEAD_DIM + 1, :] for h in range(A_HEADS)]
    o_ref[...] = jnp.concatenate(outs, axis=0).T.astype(o_ref.dtype)


def _dsa(a_out, vt_out, idx_out, bias_tab, bsz, seq):
    nq = seq // TQ
    return pl.pallas_call(
        _dsa_kernel,
        grid=(bsz, nq),
        in_specs=[pl.BlockSpec((TQ, A_WIDTH), lambda b, i: (b * nq + i, 0)),
                  pl.BlockSpec((seq, A_WIDTH), lambda b, i: (b, 1)),
                  pl.BlockSpec((VT_ROWS, seq), lambda b, i: (0, b)),
                  pl.BlockSpec((TQ, IDX_WIDTH), lambda b, i: (b * nq + i, 0)),
                  pl.BlockSpec((TQ, LANES), lambda b, i: (b * nq + i, IDX_WIDTH // LANES)),
                  pl.BlockSpec((seq, LANES), lambda b, i: (b, IDX_WIDTH // LANES)),
                  _resident((2, A_HEADS, TQ, TQ))],
        out_specs=pl.BlockSpec((TQ, A_WIDTH), lambda b, i: (b * nq + i, 0)),
        out_shape=jax.ShapeDtypeStruct((bsz * seq, A_WIDTH), BF16),
        scratch_shapes=[pltpu.VMEM((IDX_HEADS, TQ, IDX_WIDTH), BF16),
                        pltpu.VMEM((A_HEADS, TQ, LANES), BF16),
                        pltpu.VMEM((seq, IDX_WIDTH), BF16),
                        pltpu.VMEM((seq, TQ), F32),
                        pltpu.VMEM((seq, TQ), BF16),
                        pltpu.VMEM((seq, TQ), F32),
                        pltpu.VMEM((A_HEADS, 8, TQ), F32),
                        pltpu.VMEM((A_HEADS, VT_HEAD, TQ), F32),
                        pltpu.VMEM((2, A_HEADS, TQ, TQ), F32),
                        pltpu.VMEM((2, A_HEADS, 8, TQ), F32),
                        pltpu.VMEM((2, A_HEADS, 8, TQ), F32)],
        compiler_params=pltpu.CompilerParams(dimension_semantics=("parallel", "arbitrary"),
                                             vmem_limit_bytes=VMEM_LIMIT),
        name="dsa",
    )(a_out, a_out, vt_out, idx_out, idx_out, idx_out, bias_tab)


def _gla_kernel(qk_ref, v_ref, og_ref, gd_ref, wgu_ref, bg_ref, gn_ref, mcum_ref, mtot_ref,
                o_ref, st_ref, ob_ref):
    @pl.when(pl.program_id(1) == 0)
    def _():
        st_ref[...] = jnp.zeros_like(st_ref)

    z = _dot(gd_ref[...].astype(BF16), wgu_ref[...]) + bg_ref[...]
    g = -(jnp.maximum(-z, 0.0) + jnp.log(1.0 + jnp.exp(-jnp.abs(z)))) * (1.0 / GATE_NORMALIZER)
    g_hi = g.astype(BF16)
    g_lo = (g - g_hi.astype(F32)).astype(BF16)
    mcum = mcum_ref[...]
    mtot = mtot_ref[...]
    bcum = _dot(mcum, g_hi) + _dot(mcum, g_lo)
    btot = _dot(mtot, g_hi) + _dot(mtot, g_lo)

    q = qk_ref[:, 0:B_KEY_WIDTH]
    k = qk_ref[:, B_KEY_WIDTH:2 * B_KEY_WIDTH]
    q_in = q * B_KEY_DIM ** -0.5 * jnp.exp(bcum)
    k_in = (k * jnp.exp(-bcum)).astype(BF16)
    k_dec = k * jnp.exp(btot - bcum)
    e_tot = jnp.exp(btot)

    lane = lax.broadcasted_iota(jnp.int32, (1, LANES), 1)
    halves = (lane < B_KEY_DIM, lane >= B_KEY_DIM)
    r_i = lax.broadcasted_iota(jnp.int32, (CHUNK, CHUNK), 0)
    c_i = lax.broadcasted_iota(jnp.int32, (CHUNK, CHUNK), 1)
    causal = r_i >= c_i

    n_chunks = TS_GLA // CHUNK
    rows = [slice(c * CHUNK, (c + 1) * CHUNK) for c in range(n_chunks)]
    qm, kdm, vh, att, upd = {}, {}, {}, {}, {}
    for pair in range(B_HEADS // 2):
        cs = slice(pair * LANES, (pair + 1) * LANES)
        for sub in range(2):
            h = 2 * pair + sub
            qm[h] = jnp.where(halves[sub], q_in[:, cs], 0.0).astype(BF16)
            kdm[h] = jnp.where(halves[sub], k_dec[:, cs], 0.0).astype(BF16)
            vh[h] = v_ref[:, h * B_VAL_DIM:(h + 1) * B_VAL_DIM].astype(BF16)
    for h in range(B_HEADS):
        ki = k_in[:, (h // 2) * LANES:(h // 2 + 1) * LANES]
        for c in range(n_chunks):
            att[h, c] = jnp.where(causal, _dot_nt(qm[h][rows[c]], ki[rows[c]]), 0.0).astype(BF16)
    for pair in range(B_HEADS // 2):
        for c in range(n_chunks):
            upd[pair, c] = (_dot_tn(vh[2 * pair][rows[c]], kdm[2 * pair][rows[c]])
                            + _dot_tn(vh[2 * pair + 1][rows[c]], kdm[2 * pair + 1][rows[c]]))
    for h in range(B_HEADS):
        for c in range(n_chunks):
            ob_ref[rows[c], h * B_VAL_DIM:(h + 1) * B_VAL_DIM] = _dot(att[h, c], vh[h][rows[c]])
    for pair in range(B_HEADS // 2):
        cs = slice(pair * LANES, (pair + 1) * LANES)
        state = st_ref[pair]
        for c in range(n_chunks):
            sb = state.astype(BF16)
            for h in (2 * pair, 2 * pair + 1):
                ob_ref[rows[c], h * B_VAL_DIM:(h + 1) * B_VAL_DIM] += _dot_nt(qm[h][rows[c]], sb)
            state = state * e_tot[c * CHUNK:c * CHUNK + 1, cs] + upd[pair, c]
        st_ref[pair] = state

    og = og_ref[...]
    gate = og * jax.nn.sigmoid(og)
    gn = gn_ref[...]
    for h in range(B_HEADS):
        cs = slice(h * B_VAL_DIM, (h + 1) * B_VAL_DIM)
        o_ref[:, cs] = (_rms(ob_ref[:, cs], gn[:, cs]) * gate[:, cs]).astype(o_ref.dtype)


def _gla(b_out, w_gate_up, b_gate, gla_norm, bsz, seq):
    ns = seq // TS_GLA
    idx = np.arange(TS_GLA)
    same = (idx[:, None] // CHUNK) == (idx[None, :] // CHUNK)
    mcum = jnp.asarray(same & (idx[None, :] <= idx[:, None]), BF16)
    mtot = jnp.asarray(same, BF16)
    wgu = jnp.pad(w_gate_up, ((0, LANES - GATE_RANK), (0, 0))).astype(BF16)
    return pl.pallas_call(
        _gla_kernel,
        grid=(bsz, ns),
        in_specs=[pl.BlockSpec((TS_GLA, 2 * B_KEY_WIDTH), lambda b, j: (b * ns + j, 0)),
                  pl.BlockSpec((TS_GLA, B_WIDTH), lambda b, j: (b * ns + j, 1)),
                  pl.BlockSpec((TS_GLA, B_WIDTH), lambda b, j: (b * ns + j, 2)),
                  pl.BlockSpec((TS_GLA, LANES), lambda b, j: (b * ns + j, (2 * B_KEY_WIDTH + 2 * B_WIDTH) // LANES)),
                  _resident((LANES, B_KEY_WIDTH)),
                  _resident((1, B_KEY_WIDTH)),
                  _resident((1, B_WIDTH)),
                  _resident((TS_GLA, TS_GLA)),
                  _resident((TS_GLA, TS_GLA))],
        out_specs=pl.BlockSpec((TS_GLA, B_WIDTH), lambda b, j: (b * ns + j, 0)),
        out_shape=jax.ShapeDtypeStruct((bsz * seq, B_WIDTH), BF16),
        scratch_shapes=[pltpu.VMEM((B_HEADS // 2, B_VAL_DIM, LANES), F32),
                        pltpu.VMEM((TS_GLA, B_WIDTH), F32)],
        compiler_params=pltpu.CompilerParams(dimension_semantics=("parallel", "arbitrary"),
                                             vmem_limit_bytes=VMEM_LIMIT),
        name="gla",
    )(b_out, b_out, b_out, b_out, wgu, b_gate.reshape(1, -1), gla_norm.reshape(1, -1), mcum, mtot)


def _out_ffn_kernel(x_ref, oa_ref, ob_ref, wo_ref, nf_ref, wg_ref, wu_ref, wd_ref, nl_ref, o_ref):
    mix = jnp.concatenate([oa_ref[...], ob_ref[...]], axis=1)
    x1 = x_ref[...] + _dot(mix, wo_ref[...])
    h = _rms(x1, nf_ref[...]).astype(BF16)
    g = _dot(h, wg_ref[...])
    u = _dot(h, wu_ref[...])
    a = (g * jax.nn.sigmoid(g) * u).astype(BF16)
    o_ref[...] = _rms(x1 + _dot(a, wd_ref[...]), nl_ref[...])


def _out_ffn(x2d, o_a, o_b, w_out, norm_ffn, w_gate, w_up, w_down, norm_final):
    t = x2d.shape[0]
    row = lambda r: (r, 0)
    return pl.pallas_call(
        _out_ffn_kernel,
        grid=(t // TM_FFN,),
        in_specs=[pl.BlockSpec((TM_FFN, D_MODEL), row),
                  pl.BlockSpec((TM_FFN, A_WIDTH), row),
                  pl.BlockSpec((TM_FFN, B_WIDTH), row),
                  _resident((A_WIDTH + B_WIDTH, D_MODEL)),
                  _resident((1, D_MODEL)),
                  _resident((D_MODEL, D_FF)),
                  _resident((D_MODEL, D_FF)),
                  _resident((D_FF, D_MODEL)),
                  _resident((1, D_MODEL))],
        out_specs=pl.BlockSpec((TM_FFN, D_MODEL), row),
        out_shape=jax.ShapeDtypeStruct((t, D_MODEL), F32),
        compiler_params=pltpu.CompilerParams(dimension_semantics=("parallel",),
                                             vmem_limit_bytes=VMEM_LIMIT),
        name="out_ffn",
    )(x2d, o_a, o_b, w_out, norm_ffn, w_gate, w_up, w_down, norm_final)


def kernel(x, norm_mix, w_in, w_gate_up, b_gate, gla_norm, w_out, rel_bias,
           norm_ffn, w_ffn_gate, w_ffn_up, w_ffn_down, norm_final):
    bsz, seq, d = x.shape
    assert w_in.shape[0] == 1, "fused final norm assumes a single layer"
    assert d == D_MODEL and seq % TS_GLA == 0 and seq % TQ == 0 and seq // 4 >= TOPK_MAX
    bias_tab = _bias_tables(rel_bias)
    x2d = x.reshape(bsz * seq, d)
    a_out, vt_out, idx_out, b_out = _in_proj(x2d, norm_mix[0].reshape(1, d), *_pack_w_in(w_in[0]))
    o_a = _dsa(a_out, vt_out, idx_out, bias_tab, bsz, seq)
    o_b = _gla(b_out, w_gate_up[0], b_gate[0], gla_norm[0], bsz, seq)
    out = _out_ffn(x2d, o_a, o_b, w_out[0].astype(BF16), norm_ffn[0].reshape(1, d),
                   w_ffn_gate[0].astype(BF16), w_ffn_up[0].astype(BF16),
                   w_ffn_down[0].astype(BF16), norm_final.reshape(1, d))
    return out.reshape(bsz, seq, d)
```

```python
import math

import numpy as np
import jax
import jax.numpy as jnp
from jax import lax
from jax.experimental import pallas as pl
from jax.experimental.pallas import tpu as pltpu

F32 = jnp.float32
BF16 = jnp.bfloat16

D_MODEL = 1024
CHUNK = 64
A_HEADS = 8
A_HEAD_DIM = 64
A_WIDTH = A_HEADS * A_HEAD_DIM
IDX_HEADS = 8
IDX_DIM = 32
IDX_WIDTH = IDX_HEADS * IDX_DIM
TOPK_MAX = 256
B_HEADS = 4
B_KEY_DIM = 64
B_VAL_DIM = 128
B_WIDTH = B_HEADS * B_VAL_DIM
B_KEY_WIDTH = B_HEADS * B_KEY_DIM
GATE_RANK = 16
GATE_NORMALIZER = 16.0
N_BUCKETS = 32
MAX_DISTANCE = 128
D_FF = 2816
EPS = 1e-6
NEG = -1e30
LOG2E = math.log2(math.e)

LANES = 128
VMEM_LIMIT = 56 * 1024 * 1024

TM_PROJ = 512
TQ = 256
TS_GLA = 512
TM_FFN = 512

A_COLS = 2 * A_WIDTH
VT_HEAD = A_HEAD_DIM + 16
VT_ROWS = A_HEADS * VT_HEAD
IDX_COLS = IDX_WIDTH + LANES
IDX_W_LANE = IDX_DIM
B_COLS = 2 * B_KEY_WIDTH + B_WIDTH + B_WIDTH + LANES
W_COLS = A_COLS + IDX_COLS + B_COLS

NT_DIMS = (((1,), (1,)), ((), ()))
TN_DIMS = (((0,), (0,)), ((), ()))


def _dot(a, b):
    return jnp.dot(a, b, preferred_element_type=F32)


def _dot_nt(a, b):
    return lax.dot_general(a, b, NT_DIMS, preferred_element_type=F32)


def _dot_tn(a, b):
    return lax.dot_general(a, b, TN_DIMS, preferred_element_type=F32)


def _rms(x, g):
    return x * lax.rsqrt(jnp.mean(x * x, axis=-1, keepdims=True) + EPS) * g


def _resident(shape):
    return pl.BlockSpec(shape, lambda *_: (0,) * len(shape), pipeline_mode=pl.Buffered(1))


def _t5_bucket_np(rel):
    half = N_BUCKETS // 2
    max_exact = half // 2
    ret = np.where(rel > 0, half, 0)
    n = np.abs(rel)
    nf = np.maximum(n, 1).astype(np.float32)
    large = max_exact + (np.log(nf / np.float32(max_exact))
                         / np.float32(math.log(MAX_DISTANCE / max_exact))
                         * np.float32(half - max_exact)).astype(np.int32)
    large = np.minimum(large, half - 1)
    return (ret + np.where(n < max_exact, n, large)).astype(np.int32)


FAR_BUCKET = int(_t5_bucket_np(np.array([-2 * TQ]))[0])


def _bias_kernel(rb_ref, bkt_ref, o_ref):
    bk = bkt_ref[0]
    for h in range(A_HEADS):
        acc = jnp.zeros(bk.shape, F32)
        for b in range(N_BUCKETS):
            acc = jnp.where(bk == b, rb_ref[b, h], acc)
        o_ref[0, h] = (acc - rb_ref[FAR_BUCKET, h]) * LOG2E


def _bias_tables(rel_bias):
    kl = np.arange(TQ)[:, None]
    ql = np.arange(TQ)[None, :]
    buckets = np.stack([_t5_bucket_np(kl - ql - TQ), _t5_bucket_np(kl - ql)])
    return pl.pallas_call(
        _bias_kernel,
        grid=(2,),
        in_specs=[pl.BlockSpec(memory_space=pltpu.SMEM),
                  pl.BlockSpec((1, TQ, TQ), lambda t: (t, 0, 0))],
        out_specs=pl.BlockSpec((1, A_HEADS, TQ, TQ), lambda t: (t, 0, 0, 0)),
        out_shape=jax.ShapeDtypeStruct((2, A_HEADS, TQ, TQ), F32),
        name="bias_tables",
    )(rel_bias, jnp.asarray(buckets))


def _in_proj_kernel(x_ref, g_ref, w_ref, wv_ref, a_ref, vt_ref, i_ref, b_ref):
    h = _rms(x_ref[...], g_ref[...]).astype(BF16)
    c0 = 0
    a_ref[:, 0:A_WIDTH] = (_dot(h, w_ref[:, c0:c0 + A_WIDTH]) * (A_HEAD_DIM ** -0.5 * LOG2E)).astype(BF16)
    c0 += A_WIDTH
    a_ref[:, A_WIDTH:2 * A_WIDTH] = _dot(h, w_ref[:, c0:c0 + A_WIDTH]).astype(BF16)
    c0 += A_WIDTH
    vt = _dot_nt(wv_ref[...], h).astype(BF16)
    ones = jnp.ones((VT_HEAD - A_HEAD_DIM, vt.shape[1]), BF16)
    for hd in range(A_HEADS):
        vt_ref[hd * VT_HEAD:hd * VT_HEAD + A_HEAD_DIM, :] = vt[hd * A_HEAD_DIM:(hd + 1) * A_HEAD_DIM, :]
        vt_ref[hd * VT_HEAD + A_HEAD_DIM:(hd + 1) * VT_HEAD, :] = ones
    i_ref[...] = _dot(h, w_ref[:, c0:c0 + IDX_COLS])
    c0 += IDX_COLS
    b_ref[...] = _dot(h, w_ref[:, c0:c0 + B_COLS])


def _pack_w_in(w_in):
    sizes = (A_WIDTH, A_WIDTH, A_WIDTH, IDX_WIDTH, IDX_DIM, IDX_HEADS,
             B_KEY_WIDTH, B_KEY_WIDTH, B_WIDTH, GATE_RANK, B_WIDTH)
    offs = np.cumsum((0,) + sizes)
    qa, ka, va, qi, ki, wi, qb, kb, vb, gd, og = [w_in[:, offs[j]:offs[j + 1]] for j in range(len(sizes))]
    d = w_in.shape[0]
    cols = [qa, ka,
            qi, jnp.pad(jnp.concatenate([ki, wi], axis=1), ((0, 0), (0, LANES - IDX_DIM - IDX_HEADS))),
            qb, kb, vb, og, jnp.pad(gd, ((0, 0), (0, LANES - GATE_RANK)))]
    return jnp.concatenate(cols, axis=1).astype(BF16), va.T.astype(BF16)


def _in_proj(x2d, g, w_packed, wv_t):
    t = x2d.shape[0]
    return pl.pallas_call(
        _in_proj_kernel,
        grid=(t // TM_PROJ,),
        in_specs=[pl.BlockSpec((TM_PROJ, D_MODEL), lambda r: (r, 0)),
                  _resident((1, D_MODEL)),
                  _resident((D_MODEL, W_COLS)),
                  _resident((A_WIDTH, D_MODEL))],
        out_specs=[pl.BlockSpec((TM_PROJ, A_COLS), lambda r: (r, 0)),
                   pl.BlockSpec((VT_ROWS, TM_PROJ), lambda r: (0, r)),
                   pl.BlockSpec((TM_PROJ, IDX_COLS), lambda r: (r, 0)),
                   pl.BlockSpec((TM_PROJ, B_COLS), lambda r: (r, 0))],
        out_shape=[jax.ShapeDtypeStruct((t, A_COLS), BF16),
                   jax.ShapeDtypeStruct((VT_ROWS, t), BF16),
                   jax.ShapeDtypeStruct((t, IDX_COLS), F32),
                   jax.ShapeDtypeStruct((t, B_COLS), F32)],
        compiler_params=pltpu.CompilerParams(dimension_semantics=("parallel",),
                                             vmem_limit_bytes=VMEM_LIMIT),
        name="in_proj",
    )(x2d, g, w_packed, wv_t)


def _rows8(x, op):
    r, c = x.shape
    if r % 32 == 0 and r > 32:
        return op(op(x.reshape(r // 32, 32, c), axis=0).reshape(4, 8, c), axis=0)
    return op(x.reshape(r // 8, 8, c), axis=0)


def _tree_add(parts):
    while len(parts) > 1:
        parts = [a + b for a, b in zip(parts[0::2], parts[1::2])] + ([parts[-1]] if len(parts) % 2 else [])
    return parts[0]


def _colmax(x):
    return jnp.max(_rows8(x, jnp.max), axis=0, keepdims=True)


N_COARSE = 10
N_BISECT = 10


def _dsa_kernel(q_ref, k_ref, vt_ref, iq_ref, iw_ref, ik_ref, bias_ref, o_ref,
                qi_ref, qa_ref, kt_ref, s_ref, sb_ref, mb_ref, m_ref, acc_ref, lg_ref, mo_ref, mn_ref):
    i = pl.program_id(1)
    nk = i + 1
    npair = (nk + 1) // 2
    big = jnp.float32(3e38)
    kk = jnp.float32(TOPK_MAX)

    def blk(kc):
        start = kc * TQ
        return pl.ds(start if isinstance(kc, int) else pl.multiple_of(start, TQ), TQ)

    def blk2(j):
        return pl.ds(pl.multiple_of(j * 2 * TQ, 2 * TQ), 2 * TQ)

    row = lax.broadcasted_iota(jnp.int32, (TQ, TQ), 0)
    col = lax.broadcasted_iota(jnp.int32, (TQ, TQ), 1)
    adm_diag = row < (col // CHUNK + 1) * CHUNK

    lane_i = lax.broadcasted_iota(jnp.int32, (1, IDX_WIDTH), 1)
    iq = iq_ref[...]
    for h in range(IDX_HEADS):
        qi_ref[h] = jnp.where(lane_i // IDX_DIM == h, iq, 0.0).astype(BF16)
    lane_p = lax.broadcasted_iota(jnp.int32, (1, LANES), 1)
    for h in range(A_HEADS):
        q_pair = q_ref[:, (h // 2) * LANES:(h // 2 + 1) * LANES]
        qa_ref[h] = jnp.where((lane_p < A_HEAD_DIM) == (h % 2 == 0), q_pair, jnp.zeros_like(q_pair))
    w_t = iw_ref[...].T * (IDX_DIM ** -0.5 * IDX_HEADS ** -0.5)

    @pl.when(i == 0)
    def _():
        src = lax.broadcasted_iota(jnp.int32, (LANES, IDX_WIDTH), 0)
        dst = lax.broadcasted_iota(jnp.int32, (LANES, IDX_WIDTH), 1)
        spread = jnp.logical_and(src < IDX_DIM, dst % IDX_DIM == src).astype(BF16)
        for kb in range(kt_ref.shape[0] // TQ):
            kt_ref[kb * TQ:(kb + 1) * TQ, :] = _dot(ik_ref[kb * TQ:(kb + 1) * TQ, :].astype(BF16),
                                                    spread).astype(BF16)

    def score_blk(kc, c):
        kic = kt_ref[blk(kc), :]
        acc = jnp.zeros((TQ, TQ), F32)
        for h in range(IDX_HEADS):
            w_h = w_t[IDX_W_LANE + h:IDX_W_LANE + h + 1, :]
            acc = acc + jnp.maximum(_dot_nt(kic, qi_ref[h]), 0.0) * w_h
        adm = jnp.logical_or(kc < i, adm_diag)
        s = jnp.where(adm, acc, NEG)
        s_ref[blk(kc), :] = s
        sb_ref[blk(kc), :] = s.astype(BF16)
        return (jnp.minimum(c[0], _rows8(jnp.where(adm, acc, big), jnp.min)),
                jnp.maximum(c[1], _rows8(s, jnp.max)))

    def score_four(j, c):
        for u in range(4):
            c = score_blk(4 * j + u, c)
        return c

    c0 = lax.fori_loop(0, nk // 4, score_four, (jnp.full((8, TQ), big), jnp.full((8, TQ), -big)))
    done4 = 4 * (nk // 4)
    c0 = lax.cond(nk - done4 >= 2, lambda c: score_blk(done4 + 1, score_blk(done4, c)), lambda c: c, c0)
    lo8, hi8 = lax.cond(nk % 2 == 1, lambda c: score_blk(nk - 1, c), lambda c: c, c0)

    @pl.when(i == 0)
    def _():
        mb_ref[0:TQ, :] = jnp.where(adm_diag, 0.0, NEG)

    @pl.when(i > 0)
    def _():
        @pl.when(nk % 2 == 1)
        def _():
            s_ref[blk(nk), :] = jnp.full((TQ, TQ), NEG, F32)
            sb_ref[blk(nk), :] = jnp.full((TQ, TQ), NEG, BF16)

        lo = jnp.min(lo8, axis=0, keepdims=True)
        hi = jnp.max(hi8, axis=0, keepdims=True)
        q_col = lax.broadcasted_iota(jnp.int32, (1, TQ), 1)
        c_lo = (i * TQ + (q_col // CHUNK + 1) * CHUNK).astype(F32)

        def count_ge(t):
            def body(j, acc):
                return acc + _rows8(jnp.where(s_ref[blk2(j), :] >= t, 1.0, 0.0), jnp.sum)
            return jnp.sum(lax.fori_loop(0, npair, body, jnp.zeros((8, TQ), F32)), axis=0, keepdims=True)

        def count_ge16(t16):
            def body(j, acc):
                x = sb_ref[blk2(j), :].reshape(2 * TQ // 64, 64, TQ)
                hit = jnp.where(x >= t16, jnp.ones_like(x), jnp.zeros_like(x))
                part = _tree_add([hit[k] for k in range(hit.shape[0])])
                return acc + _tree_add([part[16 * k:16 * (k + 1)] for k in range(4)])
            acc = lax.fori_loop(0, npair, body, jnp.zeros((16, TQ), BF16))
            return jnp.sum(acc.astype(F32), axis=0, keepdims=True)

        def coarse(_, c):
            lo, hi, moved = c
            mid = (0.5 * lo + 0.5 * hi).astype(BF16).astype(F32)
            t16 = jnp.tile(jnp.broadcast_to(mid, (16, TQ)).astype(BF16), (4, 1))[None]
            ge = count_ge16(t16) >= kk
            return jnp.where(ge, mid, lo), jnp.where(ge, hi, mid), jnp.where(ge, 1.0, moved)
        lo_c, hi, moved = lax.fori_loop(0, N_COARSE, coarse, (lo, hi, jnp.zeros((1, TQ), F32)))
        lo = jnp.where(moved > 0.5, lo_c - jnp.abs(lo_c) * 2.0 ** -7 - 1e-37, lo)
        c_lo = jnp.where(moved > 0.5, big, c_lo)

        def bisect(_, c):
            lo, hi, c_lo = c
            mid = 0.5 * lo + 0.5 * hi
            cnt = count_ge(mid)
            ge = cnt >= kk
            return jnp.where(ge, mid, lo), jnp.where(ge, hi, mid), jnp.where(ge, cnt, c_lo)
        lo, hi, c_lo = lax.fori_loop(0, N_BISECT, bisect, (lo, hi, c_lo))
        def finish(ls, lo, c_lo):
            width = ls.stop - ls.start
            unresolved = jnp.max(c_lo)

            @pl.when(unresolved <= kk)
            def _():
                def select(kc, carry):
                    mb_ref[blk(kc), ls] = jnp.where(s_ref[blk(kc), ls] >= lo, 0.0, NEG)
                    return carry
                lax.fori_loop(0, nk, select, 0)

            @pl.when(unresolved > kk)
            def _():
                def min_ge(j, acc):
                    s = s_ref[blk2(j), ls]
                    return jnp.minimum(acc, _rows8(jnp.where(s >= lo, s, big), jnp.min))
                v0 = jnp.min(lax.fori_loop(0, npair, min_ge, jnp.full((8, width), big)), axis=0, keepdims=True)

                def above(v):
                    def body(j, c):
                        s = s_ref[blk2(j), ls]
                        gt = s > v
                        return (c[0] + _rows8(jnp.where(gt, 1.0, 0.0), jnp.sum),
                                jnp.minimum(c[1], _rows8(jnp.where(gt, s, big), jnp.min)))
                    c8, n8 = lax.fori_loop(0, npair, body,
                                           (jnp.zeros((8, width), F32), jnp.full((8, width), big)))
                    return jnp.sum(c8, axis=0, keepdims=True), jnp.min(n8, axis=0, keepdims=True)

                def peel_cond(c):
                    return c[2] > 0.5

                def peel(c):
                    v, _, _ = c
                    c_gt, nxt = above(v)
                    done = c_gt < kk
                    v_new = jnp.where(done, v, nxt)
                    pending = jnp.max(jnp.where(done, 0.0, 1.0))
                    return v_new, c_gt, pending
                thr, c_gt, _ = lax.while_loop(peel_cond, peel,
                                              (v0, jnp.zeros((1, width), F32), jnp.float32(1.0)))
                need = kk - c_gt

                ltri = (row >= col).astype(BF16)

                def select_blk(kc, carry):
                    s = s_ref[blk(kc), ls]
                    tie = s == thr
                    pref = _dot(ltri, jnp.where(tie, 1.0, 0.0).astype(BF16))
                    rank = jnp.where(tie, carry + pref, big)
                    mb_ref[blk(kc), ls] = jnp.where(s > thr, 0.0, jnp.where(rank <= need, 0.0, NEG))
                    return carry + pref[TQ - 1:TQ, :]

                def select_two(j, carry):
                    return select_blk(2 * j + 1, select_blk(2 * j, carry))
                lax.fori_loop(0, npair, select_two, jnp.zeros((1, width), F32))

        for half in range(TQ // LANES):
            ls = slice(half * LANES, (half + 1) * LANES)
            finish(ls, lo[:, ls], c_lo[:, ls])

    m_ref[...] = jnp.full(m_ref.shape, -big, F32)
    acc_ref[...] = jnp.zeros(acc_ref.shape, F32)

    def qk_stage(kc, slot, near):
        mb = mb_ref[blk(kc), :]
        for h in range(A_HEADS):
            cs = slice((h // 2) * LANES, (h // 2 + 1) * LANES)
            l = _dot_nt(k_ref[blk(kc), cs], qa_ref[h]) + mb
            if near:
                l = l + bias_ref[kc - i + 1, h]
            lg_ref[slot, h] = l
            m_old = m_ref[h, 0:1, :]
            m_new = jnp.maximum(m_old, _colmax(l))
            mo_ref[slot, h, 0:1, :] = m_old
            mn_ref[slot, h, 0:1, :] = m_new
            m_ref[h, 0:1, :] = m_new

    def pv_stage(kc, slot):
        for h in range(A_HEADS):
            m_new = mn_ref[slot, h, 0:1, :]
            p = jnp.exp2(lg_ref[slot, h] - m_new).astype(BF16)
            acc_ref[h] = (acc_ref[h] * jnp.exp2(mo_ref[slot, h, 0:1, :] - m_new)
                          + _dot(vt_ref[h * VT_HEAD:(h + 1) * VT_HEAD, blk(kc)], p))

    n_far = jnp.maximum(i - 1, 0)

    @pl.when(n_far > 0)
    def _():
        qk_stage(0, 0, False)
        n_body = (n_far - 1) // 2

        def body(j, carry):
            qk_stage(2 * j + 1, 1, False)
            pv_stage(2 * j, 0)
            qk_stage(2 * j + 2, 0, False)
            pv_stage(2 * j + 1, 1)
            return carry
        lax.fori_loop(0, n_body, body, 0)
        last = 2 * n_body

        @pl.when(n_far - last == 2)
        def _():
            qk_stage(last + 1, 1, False)
        pv_stage(last, 0)

        @pl.when(n_far - last == 2)
        def _():
            pv_stage(last + 1, 1)

    @pl.when(i > 0)
    def _():
        qk_stage(i - 1, 0, True)
        qk_stage(i, 1, True)
        pv_stage(i - 1, 0)
        pv_stage(i, 1)

    @pl.when(i == 0)
    def _():
        qk_stage(0, 0, True)
        pv_stage(0, 0)

    outs = [acc_ref[h, 0:A_HEAD_DIM, :] / acc_ref[h, A_HEAD_DIM:A_HEAD_DIM + 1, :] for h in range(A_HEADS)]
    o_ref[...] = jnp.concatenate(outs, axis=0).T.astype(o_ref.dtype)


def _dsa(a_out, vt_out, idx_out, bias_tab, bsz, seq):
    nq = seq // TQ
    return pl.pallas_call(
        _dsa_kernel,
        grid=(bsz, nq),
        in_specs=[pl.BlockSpec((TQ, A_WIDTH), lambda b, i: (b * nq + i, 0)),
                  pl.BlockSpec((seq, A_WIDTH), lambda b, i: (b, 1)),
                  pl.BlockSpec((VT_ROWS, seq), lambda b, i: (0, b)),
                  pl.BlockSpec((TQ, IDX_WIDTH), lambda b, i: (b * nq + i, 0)),
                  pl.BlockSpec((TQ, LANES), lambda b, i: (b * nq + i, IDX_WIDTH // LANES)),
                  pl.BlockSpec((seq, LANES), lambda b, i: (b, IDX_WIDTH // LANES)),
                  _resident((2, A_HEADS, TQ, TQ))],
        out_specs=pl.BlockSpec((TQ, A_WIDTH), lambda b, i: (b * nq + i, 0)),
        out_shape=jax.ShapeDtypeStruct((bsz * seq, A_WIDTH), BF16),
        scratch_shapes=[pltpu.VMEM((IDX_HEADS, TQ, IDX_WIDTH), BF16),
                        pltpu.VMEM((A_HEADS, TQ, LANES), BF16),
                        pltpu.VMEM((seq, IDX_WIDTH), BF16),
                        pltpu.VMEM((seq, TQ), F32),
                        pltpu.VMEM((seq, TQ), BF16),
                        pltpu.VMEM((seq, TQ), F32),
                        pltpu.VMEM((A_HEADS, 8, TQ), F32),
                        pltpu.VMEM((A_HEADS, VT_HEAD, TQ), F32),
                        pltpu.VMEM((2, A_HEADS, TQ, TQ), F32),
                        pltpu.VMEM((2, A_HEADS, 8, TQ), F32),
                        pltpu.VMEM((2, A_HEADS, 8, TQ), F32)],
        compiler_params=pltpu.CompilerParams(dimension_semantics=("parallel", "arbitrary"),
                                             vmem_limit_bytes=VMEM_LIMIT),
        name="dsa",
    )(a_out, a_out, vt_out, idx_out, idx_out, idx_out, bias_tab)


def _gla_kernel(qk_ref, v_ref, og_ref, gd_ref, wgu_ref, bg_ref, gn_ref, mcum_ref, mtot_ref,
                o_ref, st_ref, ob_ref):
    @pl.when(pl.program_id(1) == 0)
    def _():
        st_ref[...] = jnp.zeros_like(st_ref)

    z = _dot(gd_ref[...].astype(BF16), wgu_ref[...]) + bg_ref[...]
    g = -(jnp.maximum(-z, 0.0) + jnp.log(1.0 + jnp.exp(-jnp.abs(z)))) * (1.0 / GATE_NORMALIZER)
    g_hi = g.astype(BF16)
    g_lo = (g - g_hi.astype(F32)).astype(BF16)
    mcum = mcum_ref[...]
    mtot = mtot_ref[...]
    bcum = _dot(mcum, g_hi) + _dot(mcum, g_lo)
    btot = _dot(mtot, g_hi) + _dot(mtot, g_lo)

    q = qk_ref[:, 0:B_KEY_WIDTH]
    k = qk_ref[:, B_KEY_WIDTH:2 * B_KEY_WIDTH]
    q_in = q * B_KEY_DIM ** -0.5 * jnp.exp(bcum)
    k_in = (k * jnp.exp(-bcum)).astype(BF16)
    k_dec = k * jnp.exp(btot - bcum)
    e_tot = jnp.exp(btot)

    lane = lax.broadcasted_iota(jnp.int32, (1, LANES), 1)
    halves = (lane < B_KEY_DIM, lane >= B_KEY_DIM)
    r_i = lax.broadcasted_iota(jnp.int32, (CHUNK, CHUNK), 0)
    c_i = lax.broadcasted_iota(jnp.int32, (CHUNK, CHUNK), 1)
    causal = r_i >= c_i

    n_chunks = TS_GLA // CHUNK
    rows = [slice(c * CHUNK, (c + 1) * CHUNK) for c in range(n_chunks)]
    qm, kdm, vh, att, upd = {}, {}, {}, {}, {}
    for pair in range(B_HEADS // 2):
        cs = slice(pair * LANES, (pair + 1) * LANES)
        for sub in range(2):
            h = 2 * pair + sub
            qm[h] = jnp.where(halves[sub], q_in[:, cs], 0.0).astype(BF16)
            kdm[h] = jnp.where(halves[sub], k_dec[:, cs], 0.0).astype(BF16)
            vh[h] = v_ref[:, h * B_VAL_DIM:(h + 1) * B_VAL_DIM].astype(BF16)
    for h in range(B_HEADS):
        ki = k_in[:, (h // 2) * LANES:(h // 2 + 1) * LANES]
        for c in range(n_chunks):
            att[h, c] = jnp.where(causal, _dot_nt(qm[h][rows[c]], ki[rows[c]]), 0.0).astype(BF16)
    for pair in range(B_HEADS // 2):
        for c in range(n_chunks):
            upd[pair, c] = (_dot_tn(vh[2 * pair][rows[c]], kdm[2 * pair][rows[c]])
                            + _dot_tn(vh[2 * pair + 1][rows[c]], kdm[2 * pair + 1][rows[c]]))
    for h in range(B_HEADS):
        for c in range(n_chunks):
            ob_ref[rows[c], h * B_VAL_DIM:(h + 1) * B_VAL_DIM] = _dot(att[h, c], vh[h][rows[c]])
    for pair in range(B_HEADS // 2):
        cs = slice(pair * LANES, (pair + 1) * LANES)
        state = st_ref[pair]
        for c in range(n_chunks):
            sb = state.astype(BF16)
            for h in (2 * pair, 2 * pair + 1):
                ob_ref[rows[c], h * B_VAL_DIM:(h + 1) * B_VAL_DIM] += _dot_nt(qm[h][rows[c]], sb)
            state = state * e_tot[c * CHUNK:c * CHUNK + 1, cs] + upd[pair, c]
        st_ref[pair] = state

    og = og_ref[...]
    gate = og * jax.nn.sigmoid(og)
    gn = gn_ref[...]
    for h in range(B_HEADS):
        cs = slice(h * B_VAL_DIM, (h + 1) * B_VAL_DIM)
        o_ref[:, cs] = (_rms(ob_ref[:, cs], gn[:, cs]) * gate[:, cs]).astype(o_ref.dtype)


def _gla(b_out, w_gate_up, b_gate, gla_norm, bsz, seq):
    ns = seq // TS_GLA
    idx = np.arange(TS_GLA)
    same = (idx[:, None] // CHUNK) == (idx[None, :] // CHUNK)
    mcum = jnp.asarray(same & (idx[None, :] <= idx[:, None]), BF16)
    mtot = jnp.asarray(same, BF16)
    wgu = jnp.pad(w_gate_up, ((0, LANES - GATE_RANK), (0, 0))).astype(BF16)
    return pl.pallas_call(
        _gla_kernel,
        grid=(bsz, ns),
        in_specs=[pl.BlockSpec((TS_GLA, 2 * B_KEY_WIDTH), lambda b, j: (b * ns + j, 0)),
                  pl.BlockSpec((TS_GLA, B_WIDTH), lambda b, j: (b * ns + j, 1)),
                  pl.BlockSpec((TS_GLA, B_WIDTH), lambda b, j: (b * ns + j, 2)),
                  pl.BlockSpec((TS_GLA, LANES), lambda b, j: (b * ns + j, (2 * B_KEY_WIDTH + 2 * B_WIDTH) // LANES)),
                  _resident((LANES, B_KEY_WIDTH)),
                  _resident((1, B_KEY_WIDTH)),
                  _resident((1, B_WIDTH)),
                  _resident((TS_GLA, TS_GLA)),
                  _resident((TS_GLA, TS_GLA))],
        out_specs=pl.BlockSpec((TS_GLA, B_WIDTH), lambda b, j: (b * ns + j, 0)),
        out_shape=jax.ShapeDtypeStruct((bsz * seq, B_WIDTH), BF16),
        scratch_shapes=[pltpu.VMEM((B_HEADS // 2, B_VAL_DIM, LANES), F32),
                        pltpu.VMEM((TS_GLA, B_WIDTH), F32)],
        compiler_params=pltpu.CompilerParams(dimension_semantics=("parallel", "arbitrary"),
                                             vmem_limit_bytes=VMEM_LIMIT),
        name="gla",
    )(b_out, b_out, b_out, b_out, wgu, b_gate.reshape(1, -1), gla_norm.reshape(1, -1), mcum, mtot)


def _out_ffn_kernel(x_ref, oa_ref, ob_ref, wo_ref, nf_ref, wg_ref, wu_ref, wd_ref, nl_ref, o_ref):
    halves = [slice(r * (TM_FFN // 2), (r + 1) * (TM_FFN // 2)) for r in range(2)]
    x1 = [x_ref[rs, :] + _dot(jnp.concatenate([oa_ref[rs, :], ob_ref[rs, :]], axis=1), wo_ref[...])
          for rs in halves]
    h = [_rms(v, nf_ref[...]).astype(BF16) for v in x1]
    gu = [(_dot(v, wg_ref[...]), _dot(v, wu_ref[...])) for v in h]
    a = [(g * jax.nn.sigmoid(g) * u).astype(BF16) for g, u in gu]
    y = [v + _dot(w, wd_ref[...]) for v, w in zip(x1, a)]
    for rs, v in zip(halves, y):
        o_ref[rs, :] = _rms(v, nl_ref[...])


def _out_ffn(x2d, o_a, o_b, w_out, norm_ffn, w_gate, w_up, w_down, norm_final):
    t = x2d.shape[0]
    row = lambda r: (r, 0)
    return pl.pallas_call(
        _out_ffn_kernel,
        grid=(t // TM_FFN,),
        in_specs=[pl.BlockSpec((TM_FFN, D_MODEL), row),
                  pl.BlockSpec((TM_FFN, A_WIDTH), row),
                  pl.BlockSpec((TM_FFN, B_WIDTH), row),
                  _resident((A_WIDTH + B_WIDTH, D_MODEL)),
                  _resident((1, D_MODEL)),
                  _resident((D_MODEL, D_FF)),
                  _resident((D_MODEL, D_FF)),
                  _resident((D_FF, D_MODEL)),
                  _resident((1, D_MODEL))],
        out_specs=pl.BlockSpec((TM_FFN, D_MODEL), row),
        out_shape=jax.ShapeDtypeStruct((t, D_MODEL), F32),
        compiler_params=pltpu.CompilerParams(dimension_semantics=("parallel",),
                                             vmem_limit_bytes=VMEM_LIMIT),
        name="out_ffn",
    )(x2d, o_a, o_b, w_out, norm_ffn, w_gate, w_up, w_down, norm_final)


def kernel(x, norm_mix, w_in, w_gate_up, b_gate, gla_norm, w_out, rel_bias,
           norm_ffn, w_ffn_gate, w_ffn_up, w_ffn_down, norm_final):
    bsz, seq, d = x.shape
    assert w_in.shape[0] == 1, "fused final norm assumes a single layer"
    assert d == D_MODEL and seq % TS_GLA == 0 and seq % TQ == 0 and seq // 4 >= TOPK_MAX
    bias_tab = _bias_tables(rel_bias)
    x2d = x.reshape(bsz * seq, d)
    a_out, vt_out, idx_out, b_out = _in_proj(x2d, norm_mix[0].reshape(1, d), *_pack_w_in(w_in[0]))
    o_a = _dsa(a_out, vt_out, idx_out, bias_tab, bsz, seq)
    o_b = _gla(b_out, w_gate_up[0], b_gate[0], gla_norm[0], bsz, seq)
    out = _out_ffn(x2d, o_a, o_b, w_out[0].astype(BF16), norm_ffn[0].reshape(1, d),
                   w_ffn_gate[0].astype(BF16), w_ffn_up[0].astype(BF16),
                   w_ffn_down[0].astype(BF16), norm_final.reshape(1, d))
    return out.reshape(bsz, seq, d)
```

```python
import math

import numpy as np
import jax
import jax.numpy as jnp
from jax import lax
from jax.experimental import pallas as pl
from jax.experimental.pallas import tpu as pltpu

F32 = jnp.float32
BF16 = jnp.bfloat16

D_MODEL = 1024
CHUNK = 64
A_HEADS = 8
A_HEAD_DIM = 64
A_WIDTH = A_HEADS * A_HEAD_DIM
IDX_HEADS = 8
IDX_DIM = 32
IDX_WIDTH = IDX_HEADS * IDX_DIM
TOPK_MAX = 256
B_HEADS = 4
B_KEY_DIM = 64
B_VAL_DIM = 128
B_WIDTH = B_HEADS * B_VAL_DIM
B_KEY_WIDTH = B_HEADS * B_KEY_DIM
GATE_RANK = 16
GATE_NORMALIZER = 16.0
N_BUCKETS = 32
MAX_DISTANCE = 128
D_FF = 2816
EPS = 1e-6
NEG = -1e30
LOG2E = math.log2(math.e)

LANES = 128
VMEM_LIMIT = 56 * 1024 * 1024

TM_PROJ = 512
TQ = 256
TS_GLA = 512
TM_FFN = 512

A_COLS = 2 * A_WIDTH
VT_HEAD = A_HEAD_DIM + 16
VT_ROWS = A_HEADS * VT_HEAD
IDX_COLS = IDX_WIDTH + LANES
IDX_W_LANE = IDX_DIM
B_COLS = 2 * B_KEY_WIDTH + B_WIDTH + B_WIDTH + LANES
W_COLS = A_COLS + IDX_COLS + B_COLS

NT_DIMS = (((1,), (1,)), ((), ()))
TN_DIMS = (((0,), (0,)), ((), ()))


def _dot(a, b):
    return jnp.dot(a, b, preferred_element_type=F32)


def _dot_nt(a, b):
    return lax.dot_general(a, b, NT_DIMS, preferred_element_type=F32)


def _dot_tn(a, b):
    return lax.dot_general(a, b, TN_DIMS, preferred_element_type=F32)


def _rms(x, g):
    return x * lax.rsqrt(jnp.mean(x * x, axis=-1, keepdims=True) + EPS) * g


def _resident(shape):
    return pl.BlockSpec(shape, lambda *_: (0,) * len(shape), pipeline_mode=pl.Buffered(1))


def _t5_bucket_np(rel):
    half = N_BUCKETS // 2
    max_exact = half // 2
    ret = np.where(rel > 0, half, 0)
    n = np.abs(rel)
    nf = np.maximum(n, 1).astype(np.float32)
    large = max_exact + (np.log(nf / np.float32(max_exact))
                         / np.float32(math.log(MAX_DISTANCE / max_exact))
                         * np.float32(half - max_exact)).astype(np.int32)
    large = np.minimum(large, half - 1)
    return (ret + np.where(n < max_exact, n, large)).astype(np.int32)


FAR_BUCKET = int(_t5_bucket_np(np.array([-2 * TQ]))[0])


def _bias_kernel(rb_ref, bkt_ref, o_ref):
    bk = bkt_ref[0]
    for h in range(A_HEADS):
        acc = jnp.zeros(bk.shape, F32)
        for b in range(N_BUCKETS):
            acc = jnp.where(bk == b, rb_ref[b, h], acc)
        o_ref[0, h] = (acc - rb_ref[FAR_BUCKET, h]) * LOG2E


def _bias_tables(rel_bias):
    kl = np.arange(TQ)[:, None]
    ql = np.arange(TQ)[None, :]
    buckets = np.stack([_t5_bucket_np(kl - ql - TQ), _t5_bucket_np(kl - ql)])
    return pl.pallas_call(
        _bias_kernel,
        grid=(2,),
        in_specs=[pl.BlockSpec(memory_space=pltpu.SMEM),
                  pl.BlockSpec((1, TQ, TQ), lambda t: (t, 0, 0))],
        out_specs=pl.BlockSpec((1, A_HEADS, TQ, TQ), lambda t: (t, 0, 0, 0)),
        out_shape=jax.ShapeDtypeStruct((2, A_HEADS, TQ, TQ), F32),
        name="bias_tables",
    )(rel_bias, jnp.asarray(buckets))


def _in_proj_kernel(x_ref, g_ref, w_ref, wv_ref, a_ref, vt_ref, i_ref, b_ref):
    h = _rms(x_ref[...], g_ref[...]).astype(BF16)
    c0 = 0
    a_ref[:, 0:A_WIDTH] = (_dot(h, w_ref[:, c0:c0 + A_WIDTH]) * (A_HEAD_DIM ** -0.5 * LOG2E)).astype(BF16)
    c0 += A_WIDTH
    a_ref[:, A_WIDTH:2 * A_WIDTH] = _dot(h, w_ref[:, c0:c0 + A_WIDTH]).astype(BF16)
    c0 += A_WIDTH
    vt = _dot_nt(wv_ref[...], h).astype(BF16)
    ones = jnp.ones((VT_HEAD - A_HEAD_DIM, vt.shape[1]), BF16)
    for hd in range(A_HEADS):
        vt_ref[hd * VT_HEAD:hd * VT_HEAD + A_HEAD_DIM, :] = vt[hd * A_HEAD_DIM:(hd + 1) * A_HEAD_DIM, :]
        vt_ref[hd * VT_HEAD + A_HEAD_DIM:(hd + 1) * VT_HEAD, :] = ones
    i_ref[...] = _dot(h, w_ref[:, c0:c0 + IDX_COLS])
    c0 += IDX_COLS
    b_ref[...] = _dot(h, w_ref[:, c0:c0 + B_COLS])


def _pack_w_in(w_in):
    sizes = (A_WIDTH, A_WIDTH, A_WIDTH, IDX_WIDTH, IDX_DIM, IDX_HEADS,
             B_KEY_WIDTH, B_KEY_WIDTH, B_WIDTH, GATE_RANK, B_WIDTH)
    offs = np.cumsum((0,) + sizes)
    qa, ka, va, qi, ki, wi, qb, kb, vb, gd, og = [w_in[:, offs[j]:offs[j + 1]] for j in range(len(sizes))]
    d = w_in.shape[0]
    cols = [qa, ka,
            qi, jnp.pad(jnp.concatenate([ki, wi], axis=1), ((0, 0), (0, LANES - IDX_DIM - IDX_HEADS))),
            qb, kb, vb, og, jnp.pad(gd, ((0, 0), (0, LANES - GATE_RANK)))]
    return jnp.concatenate(cols, axis=1).astype(BF16), va.T.astype(BF16)


def _in_proj(x2d, g, w_packed, wv_t):
    t = x2d.shape[0]
    return pl.pallas_call(
        _in_proj_kernel,
        grid=(t // TM_PROJ,),
        in_specs=[pl.BlockSpec((TM_PROJ, D_MODEL), lambda r: (r, 0)),
                  _resident((1, D_MODEL)),
                  _resident((D_MODEL, W_COLS)),
                  _resident((A_WIDTH, D_MODEL))],
        out_specs=[pl.BlockSpec((TM_PROJ, A_COLS), lambda r: (r, 0)),
                   pl.BlockSpec((VT_ROWS, TM_PROJ), lambda r: (0, r)),
                   pl.BlockSpec((TM_PROJ, IDX_COLS), lambda r: (r, 0)),
                   pl.BlockSpec((TM_PROJ, B_COLS), lambda r: (r, 0))],
        out_shape=[jax.ShapeDtypeStruct((t, A_COLS), BF16),
                   jax.ShapeDtypeStruct((VT_ROWS, t), BF16),
                   jax.ShapeDtypeStruct((t, IDX_COLS), F32),
                   jax.ShapeDtypeStruct((t, B_COLS), F32)],
        compiler_params=pltpu.CompilerParams(dimension_semantics=("parallel",),
                                             vmem_limit_bytes=VMEM_LIMIT),
        name="in_proj",
    )(x2d, g, w_packed, wv_t)


def _rows8(x, op):
    r, c = x.shape
    if r % 32 == 0 and r > 32:
        return op(op(x.reshape(r // 32, 32, c), axis=0).reshape(4, 8, c), axis=0)
    return op(x.reshape(r // 8, 8, c), axis=0)


def _tree_add(parts):
    while len(parts) > 1:
        parts = [a + b for a, b in zip(parts[0::2], parts[1::2])] + ([parts[-1]] if len(parts) % 2 else [])
    return parts[0]


def _colmax(x):
    return jnp.max(_rows8(x, jnp.max), axis=0, keepdims=True)


BF16_ULP_REL = 2.0 ** -7
F32_TINY = 1e-37
N_COARSE = 10
N_BISECT = 10


def _dsa_kernel(q_ref, k_ref, vt_ref, iq_ref, iw_ref, ik_ref, bias_ref, o_ref,
                qi_ref, qa_ref, kt_ref, s_ref, sb_ref, mb_ref, m_ref, acc_ref, lg_ref, mo_ref, mn_ref):
    i = pl.program_id(1)
    nk = i + 1
    npair = (nk + 1) // 2
    big = jnp.float32(3e38)
    kk = jnp.float32(TOPK_MAX)

    def blk(kc):
        start = kc * TQ
        return pl.ds(start if isinstance(kc, int) else pl.multiple_of(start, TQ), TQ)

    def blk2(j):
        return pl.ds(pl.multiple_of(j * 2 * TQ, 2 * TQ), 2 * TQ)

    row = lax.broadcasted_iota(jnp.int32, (TQ, TQ), 0)
    col = lax.broadcasted_iota(jnp.int32, (TQ, TQ), 1)
    adm_diag = row < (col // CHUNK + 1) * CHUNK

    lane_i = lax.broadcasted_iota(jnp.int32, (1, IDX_WIDTH), 1)
    iq = iq_ref[...]
    for h in range(IDX_HEADS):
        qi_ref[h] = jnp.where(lane_i // IDX_DIM == h, iq, 0.0).astype(BF16)
    lane_p = lax.broadcasted_iota(jnp.int32, (1, LANES), 1)
    for h in range(A_HEADS):
        q_pair = q_ref[:, (h // 2) * LANES:(h // 2 + 1) * LANES]
        qa_ref[h] = jnp.where((lane_p < A_HEAD_DIM) == (h % 2 == 0), q_pair, jnp.zeros_like(q_pair))
    w_t = iw_ref[...].T * (IDX_DIM ** -0.5 * IDX_HEADS ** -0.5)

    @pl.when(i == 0)
    def _():
        src = lax.broadcasted_iota(jnp.int32, (LANES, IDX_WIDTH), 0)
        dst = lax.broadcasted_iota(jnp.int32, (LANES, IDX_WIDTH), 1)
        spread = jnp.logical_and(src < IDX_DIM, dst % IDX_DIM == src).astype(BF16)
        for kb in range(kt_ref.shape[0] // TQ):
            kt_ref[kb * TQ:(kb + 1) * TQ, :] = _dot(ik_ref[kb * TQ:(kb + 1) * TQ, :].astype(BF16),
                                                    spread).astype(BF16)

    def score_blk(kc, c):
        kic = kt_ref[blk(kc), :]
        acc = jnp.zeros((TQ, TQ), F32)
        for h in range(IDX_HEADS):
            w_h = w_t[IDX_W_LANE + h:IDX_W_LANE + h + 1, :]
            acc = acc + jnp.maximum(_dot_nt(kic, qi_ref[h]), 0.0) * w_h
        adm = jnp.logical_or(kc < i, adm_diag)
        s = jnp.where(adm, acc, NEG)
        s_ref[blk(kc), :] = s
        sb_ref[blk(kc), :] = s.astype(BF16)
        return (jnp.minimum(c[0], _rows8(jnp.where(adm, acc, big), jnp.min)),
                jnp.maximum(c[1], _rows8(s, jnp.max)))

    def score_four(j, c):
        for u in range(4):
            c = score_blk(4 * j + u, c)
        return c

    c0 = lax.fori_loop(0, nk // 4, score_four, (jnp.full((8, TQ), big), jnp.full((8, TQ), -big)))
    done4 = 4 * (nk // 4)
    c0 = lax.cond(nk - done4 >= 2, lambda c: score_blk(done4 + 1, score_blk(done4, c)), lambda c: c, c0)
    lo8, hi8 = lax.cond(nk % 2 == 1, lambda c: score_blk(nk - 1, c), lambda c: c, c0)

    @pl.when(i == 0)
    def _():
        mb_ref[0:TQ, :] = jnp.where(adm_diag, 0.0, NEG)

    @pl.when(i > 0)
    def _():
        @pl.when(nk % 2 == 1)
        def _():
            s_ref[blk(nk), :] = jnp.full((TQ, TQ), NEG, F32)
            sb_ref[blk(nk), :] = jnp.full((TQ, TQ), NEG, BF16)

        lo = jnp.min(lo8, axis=0, keepdims=True)
        hi = jnp.max(hi8, axis=0, keepdims=True)
        q_col = lax.broadcasted_iota(jnp.int32, (1, TQ), 1)
        c_lo = (i * TQ + (q_col // CHUNK + 1) * CHUNK).astype(F32)

        def count_ge(t):
            def body(j, acc):
                return acc + _rows8(jnp.where(s_ref[blk2(j), :] >= t, 1.0, 0.0), jnp.sum)
            return jnp.sum(lax.fori_loop(0, npair, body, jnp.zeros((8, TQ), F32)), axis=0, keepdims=True)

        def count_ge16(t16):
            def body(j, acc):
                x = sb_ref[blk2(j), :].reshape(2 * TQ // 64, 64, TQ)
                hit = jnp.where(x >= t16, jnp.ones_like(x), jnp.zeros_like(x))
                part = _tree_add([hit[k] for k in range(hit.shape[0])])
                return acc + _tree_add([part[16 * k:16 * (k + 1)] for k in range(4)])
            acc = lax.fori_loop(0, npair, body, jnp.zeros((16, TQ), BF16))
            return jnp.sum(acc.astype(F32), axis=0, keepdims=True)

        def coarse(_, c):
            lo, hi, moved = c
            mid = (0.5 * lo + 0.5 * hi).astype(BF16).astype(F32)
            t16 = jnp.tile(jnp.broadcast_to(mid, (16, TQ)).astype(BF16), (4, 1))[None]
            ge = count_ge16(t16) >= kk
            return jnp.where(ge, mid, lo), jnp.where(ge, hi, mid), jnp.where(ge, 1.0, moved)
        lo_c, hi, moved = lax.fori_loop(0, N_COARSE, coarse, (lo, hi, jnp.zeros((1, TQ), F32)))
        lo = jnp.where(moved > 0.5, lo_c - jnp.abs(lo_c) * BF16_ULP_REL - F32_TINY, lo)
        c_lo = jnp.where(moved > 0.5, big, c_lo)

        def bisect(_, c):
            lo, hi, c_lo = c
            mid = 0.5 * lo + 0.5 * hi
            cnt = count_ge(mid)
            ge = cnt >= kk
            return jnp.where(ge, mid, lo), jnp.where(ge, hi, mid), jnp.where(ge, cnt, c_lo)
        lo, hi, c_lo = lax.fori_loop(0, N_BISECT, bisect, (lo, hi, c_lo))
        def finish(ls, lo, c_lo):
            width = ls.stop - ls.start
            unresolved = jnp.max(c_lo)

            @pl.when(unresolved <= kk)
            def _():
                def select(kc, carry):
                    mb_ref[blk(kc), ls] = jnp.where(s_ref[blk(kc), ls] >= lo, 0.0, NEG)
                    return carry
                lax.fori_loop(0, nk, select, 0)

            @pl.when(unresolved > kk)
            def _():
                def min_ge(j, acc):
                    s = s_ref[blk2(j), ls]
                    return jnp.minimum(acc, _rows8(jnp.where(s >= lo, s, big), jnp.min))
                v0 = jnp.min(lax.fori_loop(0, npair, min_ge, jnp.full((8, width), big)), axis=0, keepdims=True)

                def above(v):
                    def body(j, c):
                        s = s_ref[blk2(j), ls]
                        gt = s > v
                        return (c[0] + _rows8(jnp.where(gt, 1.0, 0.0), jnp.sum),
                                jnp.minimum(c[1], _rows8(jnp.where(gt, s, big), jnp.min)))
                    c8, n8 = lax.fori_loop(0, npair, body,
                                           (jnp.zeros((8, width), F32), jnp.full((8, width), big)))
                    return jnp.sum(c8, axis=0, keepdims=True), jnp.min(n8, axis=0, keepdims=True)

                def peel_cond(c):
                    return c[2] > 0.5

                def peel(c):
                    v, _, _ = c
                    c_gt, nxt = above(v)
                    done = c_gt < kk
                    v_new = jnp.where(done, v, nxt)
                    pending = jnp.max(jnp.where(done, 0.0, 1.0))
                    return v_new, c_gt, pending
                thr, c_gt, _ = lax.while_loop(peel_cond, peel,
                                              (v0, jnp.zeros((1, width), F32), jnp.float32(1.0)))
                need = kk - c_gt

                ltri = (row >= col).astype(BF16)

                def select_blk(kc, carry):
                    s = s_ref[blk(kc), ls]
                    tie = s == thr
                    pref = _dot(ltri, jnp.where(tie, 1.0, 0.0).astype(BF16))
                    rank = jnp.where(tie, carry + pref, big)
                    mb_ref[blk(kc), ls] = jnp.where(s > thr, 0.0, jnp.where(rank <= need, 0.0, NEG))
                    return carry + pref[TQ - 1:TQ, :]

                def select_two(j, carry):
                    return select_blk(2 * j + 1, select_blk(2 * j, carry))
                lax.fori_loop(0, npair, select_two, jnp.zeros((1, width), F32))

        for half in range(TQ // LANES):
            ls = slice(half * LANES, (half + 1) * LANES)
            finish(ls, lo[:, ls], c_lo[:, ls])

    m_ref[...] = jnp.full(m_ref.shape, -big, F32)
    acc_ref[...] = jnp.zeros(acc_ref.shape, F32)

    def qk_stage(kc, slot, near):
        mb = mb_ref[blk(kc), :]
        for h in range(A_HEADS):
            cs = slice((h // 2) * LANES, (h // 2 + 1) * LANES)
            l = _dot_nt(k_ref[blk(kc), cs], qa_ref[h]) + mb
            if near:
                l = l + bias_ref[kc - i + 1, h]
            lg_ref[slot, h] = l
            m_old = m_ref[h, 0:1, :]
            m_new = jnp.maximum(m_old, _colmax(l))
            mo_ref[slot, h, 0:1, :] = m_old
            mn_ref[slot, h, 0:1, :] = m_new
            m_ref[h, 0:1, :] = m_new

    def pv_stage(kc, slot):
        for h in range(A_HEADS):
            m_new = mn_ref[slot, h, 0:1, :]
            p = jnp.exp2(lg_ref[slot, h] - m_new).astype(BF16)
            acc_ref[h] = (acc_ref[h] * jnp.exp2(mo_ref[slot, h, 0:1, :] - m_new)
                          + _dot(vt_ref[h * VT_HEAD:(h + 1) * VT_HEAD, blk(kc)], p))

    n_far = jnp.maximum(i - 1, 0)

    @pl.when(n_far > 0)
    def _():
        qk_stage(0, 0, False)
        n_body = (n_far - 1) // 2

        def body(j, carry):
            qk_stage(2 * j + 1, 1, False)
            pv_stage(2 * j, 0)
            qk_stage(2 * j + 2, 0, False)
            pv_stage(2 * j + 1, 1)
            return carry
        lax.fori_loop(0, n_body, body, 0)
        last = 2 * n_body

        @pl.when(n_far - last == 2)
        def _():
            qk_stage(last + 1, 1, False)
        pv_stage(last, 0)

        @pl.when(n_far - last == 2)
        def _():
            pv_stage(last + 1, 1)

    @pl.when(i > 0)
    def _():
        qk_stage(i - 1, 0, True)
        qk_stage(i, 1, True)
        pv_stage(i - 1, 0)
        pv_stage(i, 1)

    @pl.when(i == 0)
    def _():
        qk_stage(0, 0, True)
        pv_stage(0, 0)

    outs = [acc_ref[h, 0:A_HEAD_DIM, :] / acc_ref[h, A_HEAD_DIM:A_HEAD_DIM + 1, :] for h in range(A_HEADS)]
    o_ref[...] = jnp.concatenate(outs, axis=0).T.astype(o_ref.dtype)


def _dsa(a_out, vt_out, idx_out, bias_tab, bsz, seq):
    nq = seq // TQ
    return pl.pallas_call(
        _dsa_kernel,
        grid=(bsz, nq),
        in_specs=[pl.BlockSpec((TQ, A_WIDTH), lambda b, i: (b * nq + i, 0)),
                  pl.BlockSpec((seq, A_WIDTH), lambda b, i: (b, 1)),
                  pl.BlockSpec((VT_ROWS, seq), lambda b, i: (0, b)),
                  pl.BlockSpec((TQ, IDX_WIDTH), lambda b, i: (b * nq + i, 0)),
                  pl.BlockSpec((TQ, LANES), lambda b, i: (b * nq + i, IDX_WIDTH // LANES)),
                  pl.BlockSpec((seq, LANES), lambda b, i: (b, IDX_WIDTH // LANES)),
                  _resident((2, A_HEADS, TQ, TQ))],
        out_specs=pl.BlockSpec((TQ, A_WIDTH), lambda b, i: (b * nq + i, 0)),
        out_shape=jax.ShapeDtypeStruct((bsz * seq, A_WIDTH), BF16),
        scratch_shapes=[pltpu.VMEM((IDX_HEADS, TQ, IDX_WIDTH), BF16),
                        pltpu.VMEM((A_HEADS, TQ, LANES), BF16),
                        pltpu.VMEM((seq, IDX_WIDTH), BF16),
                        pltpu.VMEM((seq, TQ), F32),
                        pltpu.VMEM((seq, TQ), BF16),
                        pltpu.VMEM((seq, TQ), F32),
                        pltpu.VMEM((A_HEADS, 8, TQ), F32),
                        pltpu.VMEM((A_HEADS, VT_HEAD, TQ), F32),
                        pltpu.VMEM((2, A_HEADS, TQ, TQ), F32),
                        pltpu.VMEM((2, A_HEADS, 8, TQ), F32),
                        pltpu.VMEM((2, A_HEADS, 8, TQ), F32)],
        compiler_params=pltpu.CompilerParams(dimension_semantics=("parallel", "arbitrary"),
                                             vmem_limit_bytes=VMEM_LIMIT),
        name="dsa",
    )(a_out, a_out, vt_out, idx_out, idx_out, idx_out, bias_tab)


def _gla_kernel(qk_ref, v_ref, og_ref, gd_ref, wgu_ref, bg_ref, gn_ref, mcum_ref, mtot_ref,
                o_ref, st_ref, ob_ref):
    @pl.when(pl.program_id(1) == 0)
    def _():
        st_ref[...] = jnp.zeros_like(st_ref)

    z = _dot(gd_ref[...].astype(BF16), wgu_ref[...]) + bg_ref[...]
    g = -(jnp.maximum(-z, 0.0) + jnp.log(1.0 + jnp.exp(-jnp.abs(z)))) * (1.0 / GATE_NORMALIZER)
    g_hi = g.astype(BF16)
    g_lo = (g - g_hi.astype(F32)).astype(BF16)
    mcum = mcum_ref[...]
    mtot = mtot_ref[...]
    bcum = _dot(mcum, g_hi) + _dot(mcum, g_lo)
    btot = _dot(mtot, g_hi) + _dot(mtot, g_lo)

    q = qk_ref[:, 0:B_KEY_WIDTH]
    k = qk_ref[:, B_KEY_WIDTH:2 * B_KEY_WIDTH]
    q_in = q * B_KEY_DIM ** -0.5 * jnp.exp(bcum)
    k_in = (k * jnp.exp(-bcum)).astype(BF16)
    k_dec = k * jnp.exp(btot - bcum)
    e_tot = jnp.exp(btot)

    lane = lax.broadcasted_iota(jnp.int32, (1, LANES), 1)
    halves = (lane < B_KEY_DIM, lane >= B_KEY_DIM)
    r_i = lax.broadcasted_iota(jnp.int32, (CHUNK, CHUNK), 0)
    c_i = lax.broadcasted_iota(jnp.int32, (CHUNK, CHUNK), 1)
    causal = r_i >= c_i

    n_chunks = TS_GLA // CHUNK
    rows = [slice(c * CHUNK, (c + 1) * CHUNK) for c in range(n_chunks)]
    qm, kdm, vh, att, upd = {}, {}, {}, {}, {}
    for pair in range(B_HEADS // 2):
        cs = slice(pair * LANES, (pair + 1) * LANES)
        for sub in range(2):
            h = 2 * pair + sub
            qm[h] = jnp.where(halves[sub], q_in[:, cs], 0.0).astype(BF16)
            kdm[h] = jnp.where(halves[sub], k_dec[:, cs], 0.0).astype(BF16)
            vh[h] = v_ref[:, h * B_VAL_DIM:(h + 1) * B_VAL_DIM].astype(BF16)
    for h in range(B_HEADS):
        ki = k_in[:, (h // 2) * LANES:(h // 2 + 1) * LANES]
        for c in range(n_chunks):
            att[h, c] = jnp.where(causal, _dot_nt(qm[h][rows[c]], ki[rows[c]]), 0.0).astype(BF16)
    for pair in range(B_HEADS // 2):
        for c in range(n_chunks):
            upd[pair, c] = (_dot_tn(vh[2 * pair][rows[c]], kdm[2 * pair][rows[c]])
                            + _dot_tn(vh[2 * pair + 1][rows[c]], kdm[2 * pair + 1][rows[c]]))
    for h in range(B_HEADS):
        for c in range(n_chunks):
            ob_ref[rows[c], h * B_VAL_DIM:(h + 1) * B_VAL_DIM] = _dot(att[h, c], vh[h][rows[c]])
    for pair in range(B_HEADS // 2):
        cs = slice(pair * LANES, (pair + 1) * LANES)
        state = st_ref[pair]
        for c in range(n_chunks):
            sb = state.astype(BF16)
            for h in (2 * pair, 2 * pair + 1):
                ob_ref[rows[c], h * B_VAL_DIM:(h + 1) * B_VAL_DIM] += _dot_nt(qm[h][rows[c]], sb)
            state = state * e_tot[c * CHUNK:c * CHUNK + 1, cs] + upd[pair, c]
        st_ref[pair] = state

    og = og_ref[...]
    gate = og * jax.nn.sigmoid(og)
    gn = gn_ref[...]
    for h in range(B_HEADS):
        cs = slice(h * B_VAL_DIM, (h + 1) * B_VAL_DIM)
        o_ref[:, cs] = (_rms(ob_ref[:, cs], gn[:, cs]) * gate[:, cs]).astype(o_ref.dtype)


def _gla(b_out, w_gate_up, b_gate, gla_norm, bsz, seq):
    ns = seq // TS_GLA
    idx = np.arange(TS_GLA)
    same = (idx[:, None] // CHUNK) == (idx[None, :] // CHUNK)
    mcum = jnp.asarray(same & (idx[None, :] <= idx[:, None]), BF16)
    mtot = jnp.asarray(same, BF16)
    wgu = jnp.pad(w_gate_up, ((0, LANES - GATE_RANK), (0, 0))).astype(BF16)
    return pl.pallas_call(
        _gla_kernel,
        grid=(bsz, ns),
        in_specs=[pl.BlockSpec((TS_GLA, 2 * B_KEY_WIDTH), lambda b, j: (b * ns + j, 0)),
                  pl.BlockSpec((TS_GLA, B_WIDTH), lambda b, j: (b * ns + j, 1)),
                  pl.BlockSpec((TS_GLA, B_WIDTH), lambda b, j: (b * ns + j, 2)),
                  pl.BlockSpec((TS_GLA, LANES), lambda b, j: (b * ns + j, (2 * B_KEY_WIDTH + 2 * B_WIDTH) // LANES)),
                  _resident((LANES, B_KEY_WIDTH)),
                  _resident((1, B_KEY_WIDTH)),
                  _resident((1, B_WIDTH)),
                  _resident((TS_GLA, TS_GLA)),
                  _resident((TS_GLA, TS_GLA))],
        out_specs=pl.BlockSpec((TS_GLA, B_WIDTH), lambda b, j: (b * ns + j, 0)),
        out_shape=jax.ShapeDtypeStruct((bsz * seq, B_WIDTH), BF16),
        scratch_shapes=[pltpu.VMEM((B_HEADS // 2, B_VAL_DIM, LANES), F32),
                        pltpu.VMEM((TS_GLA, B_WIDTH), F32)],
        compiler_params=pltpu.CompilerParams(dimension_semantics=("parallel", "arbitrary"),
                                             vmem_limit_bytes=VMEM_LIMIT),
        name="gla",
    )(b_out, b_out, b_out, b_out, wgu, b_gate.reshape(1, -1), gla_norm.reshape(1, -1), mcum, mtot)


def _out_ffn_kernel(x_ref, oa_ref, ob_ref, wo_ref, nf_ref, wg_ref, wu_ref, wd_ref, nl_ref, o_ref):
    halves = [slice(r * (TM_FFN // 2), (r + 1) * (TM_FFN // 2)) for r in range(2)]
    x1 = [x_ref[rs, :] + _dot(jnp.concatenate([oa_ref[rs, :], ob_ref[rs, :]], axis=1), wo_ref[...])
          for rs in halves]
    h = [_rms(v, nf_ref[...]).astype(BF16) for v in x1]
    gu = [(_dot(v, wg_ref[...]), _dot(v, wu_ref[...])) for v in h]
    a = [(g * jax.nn.sigmoid(g) * u).astype(BF16) for g, u in gu]
    y = [v + _dot(w, wd_ref[...]) for v, w in zip(x1, a)]
    for rs, v in zip(halves, y):
        o_ref[rs, :] = _rms(v, nl_ref[...])


def _out_ffn(x2d, o_a, o_b, w_out, norm_ffn, w_gate, w_up, w_down, norm_final):
    t = x2d.shape[0]
    row = lambda r: (r, 0)
    return pl.pallas_call(
        _out_ffn_kernel,
        grid=(t // TM_FFN,),
        in_specs=[pl.BlockSpec((TM_FFN, D_MODEL), row),
                  pl.BlockSpec((TM_FFN, A_WIDTH), row),
                  pl.BlockSpec((TM_FFN, B_WIDTH), row),
                  _resident((A_WIDTH + B_WIDTH, D_MODEL)),
                  _resident((1, D_MODEL)),
                  _resident((D_MODEL, D_FF)),
                  _resident((D_MODEL, D_FF)),
                  _resident((D_FF, D_MODEL)),
                  _resident((1, D_MODEL))],
        out_specs=pl.BlockSpec((TM_FFN, D_MODEL), row),
        out_shape=jax.ShapeDtypeStruct((t, D_MODEL), F32),
        compiler_params=pltpu.CompilerParams(dimension_semantics=("parallel",),
                                             vmem_limit_bytes=VMEM_LIMIT),
        name="out_ffn",
    )(x2d, o_a, o_b, w_out, norm_ffn, w_gate, w_up, w_down, norm_final)


def kernel(x, norm_mix, w_in, w_gate_up, b_gate, gla_norm, w_out, rel_bias,
           norm_ffn, w_ffn_gate, w_ffn_up, w_ffn_down, norm_final):
    bsz, seq, d = x.shape
    assert w_in.shape[0] == 1, "fused final norm assumes a single layer"
    assert d == D_MODEL and seq % TS_GLA == 0 and seq % TQ == 0 and seq // 4 >= TOPK_MAX
    bias_tab = _bias_tables(rel_bias)
    x2d = x.reshape(bsz * seq, d)
    a_out, vt_out, idx_out, b_out = _in_proj(x2d, norm_mix[0].reshape(1, d), *_pack_w_in(w_in[0]))
    o_a = _dsa(a_out, vt_out, idx_out, bias_tab, bsz, seq)
    o_b = _gla(b_out, w_gate_up[0], b_gate[0], gla_norm[0], bsz, seq)
    out = _out_ffn(x2d, o_a, o_b, w_out[0].astype(BF16), norm_ffn[0].reshape(1, d),
                   w_ffn_gate[0].astype(BF16), w_ffn_up[0].astype(BF16),
                   w_ffn_down[0].astype(BF16), norm_final.reshape(1, d))
    return out.reshape(bsz, seq, d)
```
